```python
import jax, jax.numpy as jnp
from jax import lax
import numpy as np

D_MODEL = 1024
BATCH = 4
SEQ = 4096
DEPTH = 4

GRID_W = 64
CTX_LEN = 256
N_MIXERS = 3
ROPE_THETA = 10000.0
DEEPNORM_ALPHA = (2 * DEPTH) ** 0.25
DEEPNORM_BETA = (8 * DEPTH) ** -0.25
LN_EPS = 1e-5
RMS_EPS = 1e-6
ATTN_HEAD_DIM = 128
ATTN_Q_HEADS = D_MODEL // ATTN_HEAD_DIM
ATTN_KV_HEADS = 2
ATTN_GROUP = ATTN_Q_HEADS // ATTN_KV_HEADS
Q_BLOCK = 128
CMLP_CHUNK = 128
CMLP_INNER = 2 * D_MODEL
CMLP_GROUPS = 8
CMLP_GROUP_DIM = CMLP_INNER // CMLP_GROUPS
RET_HEADS = 4
RET_KEY_DIM = D_MODEL // RET_HEADS
RET_VALUE_DIM = 2 * RET_KEY_DIM
RET_CHUNK = 128
FFN_DIM = (7 * D_MODEL) // 2
MOE_EXPERTS = 8
MOE_TOP_K = 2

kernel_name = 'hybrid_diffusion_interleaved_gqa_gmlp_retention_moe'


def layer_norm(x, g, b):
    xf = x.astype(jnp.float32)
    mu = jnp.mean(xf, axis=-1, keepdims=True)
    var = jnp.mean(jnp.square(xf - mu), axis=-1, keepdims=True)
    y = (xf - mu) * lax.rsqrt(var + LN_EPS) * g.astype(jnp.float32) + b.astype(jnp.float32)
    return y.astype(x.dtype)


def rms_norm(x, g=None):
    xf = x.astype(jnp.float32)
    y = xf * lax.rsqrt(jnp.mean(jnp.square(xf), axis=-1, keepdims=True) + RMS_EPS)
    if g is not None:
        y = y * g.astype(jnp.float32)
    return y.astype(x.dtype)


def axial_rope(x, row, col):
    d = x.shape[-1]
    nf = d // 4
    inv_freq = ROPE_THETA ** (-jnp.arange(nf, dtype=jnp.float32) / nf)
    xf = x.astype(jnp.float32)

    def rotate(seg, pos):
        ang = pos.astype(jnp.float32)[:, None] * inv_freq
        cos = jnp.cos(ang)[None, :, None, :]
        sin = jnp.sin(ang)[None, :, None, :]
        a, b = seg[..., :nf], seg[..., nf:]
        return jnp.concatenate([a * cos - b * sin, b * cos + a * sin], axis=-1)

    out = jnp.concatenate([rotate(xf[..., : d // 2], row), rotate(xf[..., d // 2:], col)], axis=-1)
    return out.astype(x.dtype)


def adaln(cond, w, b):
    return jnp.split(jax.nn.silu(cond) @ w + b, 6, axis=-1)


def modulate(x, shift, scale):
    return x * (1.0 + scale) + shift


def attend(q, k, v):
    s = jnp.einsum('bqhgd,bkhd->bhgqk', q, k).astype(jnp.float32)
    p = jax.nn.softmax(s, axis=-1).astype(v.dtype)
    return jnp.einsum('bhgqk,bkhd->bqhgd', p, v)


def attention_mixer(hc, hx, row, col, wqkv, q_norm, k_norm, wo, need_ctx):
    B, S, _ = hx.shape
    C = hc.shape[1]
    nq = ATTN_Q_HEADS * ATTN_HEAD_DIM
    nkv = ATTN_KV_HEADS * ATTN_HEAD_DIM
    scale = ATTN_HEAD_DIM ** -0.5

    def project(h):
        b_, L, _ = h.shape
        t = h @ wqkv
        q = rms_norm(t[..., :nq].reshape(b_, L, ATTN_Q_HEADS, ATTN_HEAD_DIM), q_norm)
        k = rms_norm(t[..., nq:nq + nkv].reshape(b_, L, ATTN_KV_HEADS, ATTN_HEAD_DIM), k_norm)
        v = t[..., nq + nkv:].reshape(b_, L, ATTN_KV_HEADS, ATTN_HEAD_DIM)
        return q, k, v

    qc, kc, vc = project(hc)
    qx, kx, vx = project(hx)
    qx = axial_rope(qx, row, col)
    kx = axial_rope(kx, row, col)
    k_all = jnp.concatenate([kc, kx], axis=1)
    v_all = jnp.concatenate([vc, vx], axis=1)
    n_blk = S // Q_BLOCK
    q_blocks = (qx * scale).reshape(B, n_blk, Q_BLOCK, ATTN_KV_HEADS, ATTN_GROUP, ATTN_HEAD_DIM).swapaxes(0, 1)
    o_blocks = lax.map(lambda qb: attend(qb, k_all, v_all), q_blocks)
    yx = o_blocks.swapaxes(0, 1).reshape(B, S, nq) @ wo
    yc = None
    if need_ctx:
        oc = attend((qc * scale).reshape(B, C, ATTN_KV_HEADS, ATTN_GROUP, ATTN_HEAD_DIM), kc, vc)
        yc = oc.reshape(B, C, nq) @ wo
    return yc, yx


def chunk_mlp_mixer(hc, hx, w_in, b_in, v_norm_g, v_norm_b, w_s, b_s, w_out, b_out, need_ctx):
    def run(h):
        B, L, _ = h.shape
        z = jax.nn.gelu(h @ w_in + b_in)
        u, v = z[..., :CMLP_INNER], z[..., CMLP_INNER:]
        v = layer_norm(v, v_norm_g, v_norm_b)
        v = v.reshape(B, L // CMLP_CHUNK, CMLP_CHUNK, CMLP_GROUPS, CMLP_GROUP_DIM)
        v = jnp.einsum('gpq,bnqgc->bnpgc', w_s, v) + b_s.T[:, :, None]
        return (u * v.reshape(B, L, CMLP_INNER)) @ w_out + b_out

    yx = run(hx)
    yc = run(hc) if need_ctx else None
    return yc, yx


def retention_scan(q, k, v, log_gamma, state0):
    B, L, H, _ = q.shape
    n = L // RET_CHUNK

    def to_chunks(t):
        return t.reshape(B, n, RET_CHUNK, H, t.shape[-1]).transpose(1, 0, 3, 2, 4)

    idx = jnp.arange(RET_CHUNK, dtype=jnp.float32)
    diff = idx[:, None] - idx[None, :]
    inner_decay = jnp.where(diff >= 0, jnp.exp(jnp.maximum(diff, 0.0) * log_gamma[:, None, None]), 0.0)
    cross_decay = jnp.exp((idx + 1.0) * log_gamma[:, None])
    state_weight = jnp.exp((RET_CHUNK - 1.0 - idx) * log_gamma[:, None])
    chunk_decay = jnp.exp(RET_CHUNK * log_gamma)

    def step(state, qkv):
        qc, kc, vc = qkv
        scores = jnp.einsum('bhid,bhjd->bhij', qc, kc) * inner_decay
        out = (jnp.einsum('bhij,bhje->bhie', scores, vc)
               + jnp.einsum('bhid,bhde->bhie', qc, state) * cross_decay[..., None])
        state = (state * chunk_decay[:, None, None]
                 + jnp.einsum('bhjd,bhje->bhde', kc * state_weight[..., None], vc))
        return state, out

    state, out = lax.scan(step, state0, (to_chunks(q), to_chunks(k), to_chunks(v)))
    out = out.transpose(1, 0, 3, 2, 4).reshape(B, L, H, v.shape[-1])
    return out, state


def retention_mixer(hc, hx, row, col, wqkvg, decay_raw, wo, need_ctx):
    nk = RET_HEADS * RET_KEY_DIM
    nv = RET_HEADS * RET_VALUE_DIM
    log_gamma = -jnp.exp(decay_raw.astype(jnp.float32))

    def project(h, rotate):
        B, L, _ = h.shape
        t = h @ wqkvg
        q = t[..., :nk].reshape(B, L, RET_HEADS, RET_KEY_DIM)
        k = t[..., nk:2 * nk].reshape(B, L, RET_HEADS, RET_KEY_DIM) * (RET_KEY_DIM ** -0.5)
        v = t[..., 2 * nk:2 * nk + nv].reshape(B, L, RET_HEADS, RET_VALUE_DIM)
        g = t[..., 2 * nk + nv:]
        if rotate:
            q = axial_rope(q, row, col)
            k = axial_rope(k, row, col)
        return q.astype(jnp.float32), k.astype(jnp.float32), v.astype(jnp.float32), g

    def flip(t):
        return jnp.flip(t, axis=1)

    def finish(o_fwd, o_bwd_rev, g):
        o = rms_norm(o_fwd + flip(o_bwd_rev))
        B, L = o.shape[:2]
        return (jax.nn.silu(g) * o.reshape(B, L, nv).astype(g.dtype)) @ wo

    qc, kc, vc, gc = project(hc, False)
    qx, kx, vx, gx = project(hx, True)
    zero = jnp.zeros((hx.shape[0], RET_HEADS, RET_KEY_DIM, RET_VALUE_DIM), jnp.float32)
    oc_f, state_f = retention_scan(qc, kc, vc, log_gamma[0], zero)
    oc_b, state_b = retention_scan(flip(qc), flip(kc), flip(vc), log_gamma[1], zero)
    ox_f, _ = retention_scan(qx, kx, vx, log_gamma[0], state_f)
    ox_b, _ = retention_scan(flip(qx), flip(kx), flip(vx), log_gamma[1], state_b)
    yx = finish(ox_f, ox_b, gx)
    yc = finish(oc_f, oc_b, gc) if need_ctx else None
    return yc, yx


def swiglu(h, w_gu, w_down):
    g, u = jnp.split(h @ w_gu, 2, axis=-1)
    return (jax.nn.silu(g) * u) @ w_down


def moe_swiglu(h, router, w_gu, w_down):
    shp = h.shape
    t = h.reshape(-1, shp[-1])
    logits = (t @ router).astype(jnp.float32)
    top_logit, top_idx = lax.top_k(logits, MOE_TOP_K)
    top_w = jax.nn.softmax(top_logit, axis=-1)
    gates = jnp.sum(jax.nn.one_hot(top_idx, MOE_EXPERTS, dtype=jnp.float32) * top_w[..., None], axis=1)
    out = jnp.zeros_like(t)
    for e in range(MOE_EXPERTS):
        out = out + gates[:, e:e + 1].astype(t.dtype) * swiglu(t, w_gu[e], w_down[e])
    return out.reshape(shp)


def setup_inputs(seed: int = 0) -> dict:
    key = jax.random.key(seed)
    keys = iter(jax.random.split(key, 96))

    def nrm(shape, scale):
        return scale * jax.random.normal(next(keys), shape, jnp.float32)

    D = D_MODEL
    p = {}
    p['x'] = nrm((BATCH, SEQ, D), 1.0)
    p['c'] = nrm((BATCH, D), 1.0)
    p['ctx'] = nrm((BATCH, CTX_LEN, D), 1.0)
    p['c_ctx'] = nrm((D,), 1.0)
    for i in range(DEPTH):
        pre = 'l%d_' % i
        p[pre + 'ada_w'] = nrm((D, 6 * D), D ** -0.5)
        p[pre + 'ada_b'] = nrm((6 * D,), 0.02)
        p[pre + 'ln1_g'] = 1.0 + nrm((D,), 0.02)
        p[pre + 'ln1_b'] = nrm((D,), 0.02)
        p[pre + 'ln2_g'] = 1.0 + nrm((D,), 0.02)
        p[pre + 'ln2_b'] = nrm((D,), 0.02)
        kind = i % N_MIXERS
        if kind == 0:
            nq = ATTN_Q_HEADS * ATTN_HEAD_DIM
            p[pre + 'attn_wqkv'] = nrm((D, nq + 2 * ATTN_KV_HEADS * ATTN_HEAD_DIM), D ** -0.5)
            p[pre + 'attn_q_norm'] = 1.0 + nrm((ATTN_HEAD_DIM,), 0.02)
            p[pre + 'attn_k_norm'] = 1.0 + nrm((ATTN_HEAD_DIM,), 0.02)
            p[pre + 'attn_wo'] = nrm((nq, D), nq ** -0.5 * DEEPNORM_BETA)
        elif kind == 1:
            p[pre + 'cmlp_w_in'] = nrm((D, 2 * CMLP_INNER), D ** -0.5)
            p[pre + 'cmlp_b_in'] = nrm((2 * CMLP_INNER,), 0.02)
            p[pre + 'cmlp_v_norm_g'] = 1.0 + nrm((CMLP_INNER,), 0.02)
            p[pre + 'cmlp_v_norm_b'] = nrm((CMLP_INNER,), 0.02)
            p[pre + 'cmlp_w_s'] = nrm((CMLP_GROUPS, CMLP_CHUNK, CMLP_CHUNK), CMLP_CHUNK ** -0.5)
            p[pre + 'cmlp_b_s'] = 1.0 + nrm((CMLP_GROUPS, CMLP_CHUNK), 0.02)
            p[pre + 'cmlp_w_out'] = nrm((CMLP_INNER, D), CMLP_INNER ** -0.5 * DEEPNORM_BETA)
            p[pre + 'cmlp_b_out'] = nrm((D,), 0.02)
        else:
            nk = RET_HEADS * RET_KEY_DIM
            nv = RET_HEADS * RET_VALUE_DIM
            p[pre + 'ret_wqkvg'] = nrm((D, 2 * nk + 2 * nv), D ** -0.5)
            gamma = 1.0 - jnp.exp2(-5.0 - jnp.arange(RET_HEADS, dtype=jnp.float32))
            base = jnp.log(-jnp.log(gamma))
            p[pre + 'ret_decay'] = base[None, :] + nrm((2, RET_HEADS), 0.1)
            p[pre + 'ret_wo'] = nrm((nv, D), nv ** -0.5 * DEEPNORM_BETA)
        if i % 2 == 0:
            p[pre + 'ffn_w_gu'] = nrm((D, 2 * FFN_DIM), D ** -0.5)
            p[pre + 'ffn_w_down'] = nrm((FFN_DIM, D), FFN_DIM ** -0.5 * DEEPNORM_BETA)
        else:
            p[pre + 'moe_router'] = nrm((D, MOE_EXPERTS), D ** -0.5)
            p[pre + 'moe_w_gu'] = nrm((MOE_EXPERTS, D, 2 * FFN_DIM), D ** -0.5)
            p[pre + 'moe_w_down'] = nrm((MOE_EXPERTS, FFN_DIM, D), FFN_DIM ** -0.5 * DEEPNORM_BETA)
    return p


def reference(x, c, ctx, c_ctx,
              l0_ada_w, l0_ada_b, l0_ln1_g, l0_ln1_b, l0_ln2_g, l0_ln2_b,
              l0_attn_wqkv, l0_attn_q_norm, l0_attn_k_norm, l0_attn_wo,
              l0_ffn_w_gu, l0_ffn_w_down,
              l1_ada_w, l1_ada_b, l1_ln1_g, l1_ln1_b, l1_ln2_g, l1_ln2_b,
              l1_cmlp_w_in, l1_cmlp_b_in, l1_cmlp_v_norm_g, l1_cmlp_v_norm_b,
              l1_cmlp_w_s, l1_cmlp_b_s, l1_cmlp_w_out, l1_cmlp_b_out,
              l1_moe_router, l1_moe_w_gu, l1_moe_w_down,
              l2_ada_w, l2_ada_b, l2_ln1_g, l2_ln1_b, l2_ln2_g, l2_ln2_b,
              l2_ret_wqkvg, l2_ret_decay, l2_ret_wo,
              l2_ffn_w_gu, l2_ffn_w_down,
              l3_ada_w, l3_ada_b, l3_ln1_g, l3_ln1_b, l3_ln2_g, l3_ln2_b,
              l3_attn_wqkv, l3_attn_q_norm, l3_attn_k_norm, l3_attn_wo,
              l3_moe_router, l3_moe_w_gu, l3_moe_w_down):
    layers = [
        (l0_ada_w, l0_ada_b, (l0_ln1_g, l0_ln1_b), (l0_ln2_g, l0_ln2_b),
         (l0_attn_wqkv, l0_attn_q_norm, l0_attn_k_norm, l0_attn_wo),
         (l0_ffn_w_gu, l0_ffn_w_down)),
        (l1_ada_w, l1_ada_b, (l1_ln1_g, l1_ln1_b), (l1_ln2_g, l1_ln2_b),
         (l1_cmlp_w_in, l1_cmlp_b_in, l1_cmlp_v_norm_g, l1_cmlp_v_norm_b,
          l1_cmlp_w_s, l1_cmlp_b_s, l1_cmlp_w_out, l1_cmlp_b_out),
         (l1_moe_router, l1_moe_w_gu, l1_moe_w_down)),
        (l2_ada_w, l2_ada_b, (l2_ln1_g, l2_ln1_b), (l2_ln2_g, l2_ln2_b),
         (l2_ret_wqkvg, l2_ret_decay, l2_ret_wo),
         (l2_ffn_w_gu, l2_ffn_w_down)),
        (l3_ada_w, l3_ada_b, (l3_ln1_g, l3_ln1_b), (l3_ln2_g, l3_ln2_b),
         (l3_attn_wqkv, l3_attn_q_norm, l3_attn_k_norm, l3_attn_wo),
         (l3_moe_router, l3_moe_w_gu, l3_moe_w_down)),
    ]
    n_tok = x.shape[1]
    ROWS = n_tok // GRID_W
    row = jnp.repeat(jnp.arange(ROWS, dtype=jnp.int32), GRID_W)
    col = jnp.arange(ROWS * GRID_W, dtype=jnp.int32) % GRID_W
    C = ctx.shape[1]
    xc = ctx
    for i in range(DEPTH):
        ada_w, ada_b, ln1, ln2, mix_p, ffn_p = layers[i]
        need_ctx = i < DEPTH - 1
        sh1x, sc1x, g1x, sh2x, sc2x, g2x = [t[:, None, :] for t in adaln(c, ada_w, ada_b)]
        sh1c, sc1c, g1c, sh2c, sc2c, g2c = adaln(c_ctx, ada_w, ada_b)

        hx = modulate(x, sh1x, sc1x)
        hc = modulate(xc, sh1c, sc1c)
        kind = i % N_MIXERS
        if kind == 0:
            yc, yx = attention_mixer(hc, hx, row, col, *mix_p, need_ctx=need_ctx)
        elif kind == 1:
            yc, yx = chunk_mlp_mixer(hc, hx, *mix_p, need_ctx=need_ctx)
        else:
            yc, yx = retention_mixer(hc, hx, row, col, *mix_p, need_ctx=need_ctx)
        x = layer_norm(DEEPNORM_ALPHA * x + g1x * yx, *ln1)
        if need_ctx:
            xc = layer_norm(DEEPNORM_ALPHA * xc + g1c * yc, *ln1)

        channel = swiglu if i % 2 == 0 else moe_swiglu
        hx = modulate(x, sh2x, sc2x)
        if need_ctx:
            hc = modulate(xc, sh2c, sc2c)
            f = channel(jnp.concatenate([hc, hx], axis=1), *ffn_p)
            fc, fx = f[:, :C], f[:, C:]
            xc = layer_norm(DEEPNORM_ALPHA * xc + g2c * fc, *ln2)
        else:
            fx = channel(hx, *ffn_p)
        x = layer_norm(DEEPNORM_ALPHA * x + g2x * fx, *ln2)
    return x
```

```python
import functools

import jax
import jax.numpy as jnp
from jax import lax
from jax.experimental import pallas as pl
from jax.experimental.pallas import tpu as pltpu

F32 = jnp.float32
BF16 = jnp.bfloat16

DEPTH = 4
GRID_W = 64
ROPE_THETA = 10000.0
DEEPNORM_ALPHA = (2 * DEPTH) ** 0.25
LN_EPS = 1e-5
RMS_EPS = 1e-6
ATTN_HEAD_DIM = 128
ATTN_KV_HEADS = 2
CMLP_CHUNK = 128
CMLP_GROUPS = 8
RET_HEADS = 4
RET_CHUNK = 128
MOE_EXPERTS = 8

LANES = 128
VMEM_LIMIT_BYTES = 56 * 1024 * 1024
NEG_BIG = -1e30


def _cparams(*sem):
    return pltpu.CompilerParams(dimension_semantics=sem, vmem_limit_bytes=VMEM_LIMIT_BYTES)


def _resident(shape):
    nd = len(shape)
    return pl.BlockSpec(shape, lambda *_: (0,) * nd)


def _layer_norm(z, g, b):
    mu = jnp.mean(z, axis=-1, keepdims=True)
    zc = z - mu
    var = jnp.mean(zc * zc, axis=-1, keepdims=True)
    return zc * lax.rsqrt(var + LN_EPS) * g + b


def _pick_tile(candidates, *extents):
    for t in candidates:
        if all(e % t == 0 for e in extents):
            return t
    raise ValueError(f"no tile in {candidates} divides {extents}")


class _Rows:
    def __init__(self, batch, seq, ctx_len, dim):
        self.B, self.S, self.C, self.D = batch, seq, ctx_len, dim
        self.n_lat = batch * seq
        self.n_ctx = batch * ctx_len
        self.n = self.n_lat + self.n_ctx

    def tile(self, candidates):
        return _pick_tile(candidates, self.S, self.n_ctx)

    def mod_spec(self, part, tm):
        S, B = self.S, self.B
        return pl.BlockSpec((1, 1, self.D), lambda i, *_: (jnp.minimum(i * tm // S, B) * 6 + part, 0, 0))

    def pos_spec(self, tm, width):
        n_lat_tiles, per_seq = self.n_lat // tm, self.S // tm
        return pl.BlockSpec((tm, width), lambda i, *_: (jnp.where(i < n_lat_tiles, i % per_seq, per_seq), 0))


def _ada_kernel(c_ref, w_ref, b_ref, o_ref):
    c = c_ref[...]
    s = c * jax.nn.sigmoid(c)
    o_ref[...] = jnp.dot(s, w_ref[...], preferred_element_type=F32, precision=lax.Precision.HIGHEST) + b_ref[...]


def _ada(cond, w, b):
    r, d = cond.shape
    n = w.shape[1]
    tn = _pick_tile((1536, 1024, 512, 128), n)
    return pl.pallas_call(
        _ada_kernel,
        grid=(n // tn,),
        in_specs=[_resident((r, d)), pl.BlockSpec((d, tn), lambda j: (0, j)), pl.BlockSpec((1, tn), lambda j: (0, j))],
        out_specs=pl.BlockSpec((r, tn), lambda j: (0, j)),
        out_shape=jax.ShapeDtypeStruct((r, n), F32),
        compiler_params=_cparams("parallel"),
        name="ada",
    )(cond, w, b.reshape(1, n))


def _rope_tables(seq, head_dim, pad_rows):
    nf = head_dim // 4
    inv_freq = ROPE_THETA ** (-jnp.arange(nf, dtype=F32) / nf)
    t = jnp.arange(seq, dtype=jnp.int32)
    row = (t // GRID_W).astype(F32)[:, None] * inv_freq
    col = (t % GRID_W).astype(F32)[:, None] * inv_freq
    cos = jnp.concatenate([jnp.cos(row), jnp.cos(row), jnp.cos(col), jnp.cos(col)], axis=-1)
    sin = jnp.concatenate([-jnp.sin(row), jnp.sin(row), -jnp.sin(col), jnp.sin(col)], axis=-1)
    cos = jnp.concatenate([cos, jnp.ones((pad_rows, head_dim), F32)], axis=0)
    sin = jnp.concatenate([sin, jnp.zeros((pad_rows, head_dim), F32)], axis=0)
    return cos, sin


def _swap_pairs(y, half):
    if 2 * half == LANES:
        return pltpu.roll(y, half, 1)
    lane = lax.broadcasted_iota(jnp.int32, y.shape, 1)
    return jnp.where(lane % (2 * half) < half, pltpu.roll(y, LANES - half, 1), pltpu.roll(y, half, 1))


def _qkv_kernel(x_ref, sh_ref, sc_ref, w_ref, qg_ref, kg_ref, cos_ref, sin_ref, q_ref, k_ref, v_ref, *, nq, nkv):
    dh = ATTN_HEAD_DIM
    h = (x_ref[...] * (1.0 + sc_ref[0]) + sh_ref[0]).astype(BF16)
    cos, sin = cos_ref[...], sin_ref[...]
    scale = dh ** -0.5

    def normed_rope(t, g):
        y = t * lax.rsqrt(jnp.mean(t * t, axis=-1, keepdims=True) + RMS_EPS) * g
        return y * cos + _swap_pairs(y, dh // 4) * sin

    t = jnp.dot(h, w_ref[...], preferred_element_type=F32)
    for hd in range(nq // dh):
        q_ref[:, hd * dh:(hd + 1) * dh] = (normed_rope(t[:, hd * dh:(hd + 1) * dh], qg_ref[...]) * scale).astype(BF16)
    for hd in range(nkv // dh):
        k_ref[:, hd * dh:(hd + 1) * dh] = normed_rope(t[:, nq + hd * dh:nq + (hd + 1) * dh], kg_ref[...]).astype(BF16)
    v_ref[...] = t[:, nq + nkv:].astype(BF16)


def _attn_project(rows, xa, mod, w, qg, kg, cos, sin):
    D = rows.D
    nkv = ATTN_KV_HEADS * ATTN_HEAD_DIM
    nq = w.shape[1] - 2 * nkv
    tm = rows.tile((512, 256, 128))
    row_spec = lambda width: pl.BlockSpec((tm, width), lambda i: (i, 0))
    return pl.pallas_call(
        functools.partial(_qkv_kernel, nq=nq, nkv=nkv),
        grid=(rows.n // tm,),
        in_specs=[row_spec(D), rows.mod_spec(0, tm), rows.mod_spec(1, tm), _resident(w.shape),
                  _resident((1, ATTN_HEAD_DIM)), _resident((1, ATTN_HEAD_DIM)),
                  rows.pos_spec(tm, ATTN_HEAD_DIM), rows.pos_spec(tm, ATTN_HEAD_DIM)],
        out_specs=[row_spec(nq), row_spec(nkv), row_spec(nkv)],
        out_shape=[jax.ShapeDtypeStruct((rows.n, nq), BF16), jax.ShapeDtypeStruct((rows.n, nkv), BF16),
                   jax.ShapeDtypeStruct((rows.n, nkv), BF16)],
        compiler_params=_cparams("parallel"),
        name="attn_qkv",
    )(xa, mod, mod, w, qg.reshape(1, -1), kg.reshape(1, -1), cos, sin)


def _attn_kernel(q_ref, kc_ref, vc_ref, kl_ref, vl_ref, o_ref, m_sc, l_sc, acc_sc, *, tq, tk, n_lat_tiles, seq):
    dh = ATTN_HEAD_DIM
    group = q_ref.shape[1] // dh // ATTN_KV_HEADS
    is_latent = pl.program_id(1) < n_lat_tiles
    n_lat_chunks = jnp.where(is_latent, seq // tk, 0)

    for g in range(ATTN_KV_HEADS):
        ksl = slice(g * dh, (g + 1) * dh)
        qg = jnp.concatenate([q_ref[:, (g * group + j) * dh:(g * group + j + 1) * dh] for j in range(group)], axis=0)
        m_sc[...] = jnp.full(m_sc.shape, NEG_BIG, F32)
        l_sc[...] = jnp.zeros(l_sc.shape, F32)
        acc_sc[...] = jnp.zeros(acc_sc.shape, F32)

        def online_softmax_step(k, v):
            s = lax.dot_general(qg, k, (((1,), (1,)), ((), ())), preferred_element_type=F32)
            m_old = m_sc[...]
            m_new = jnp.maximum(m_old, jnp.max(s, axis=-1, keepdims=True))
            alpha = jnp.exp(m_old - m_new)
            p = jnp.exp(s - m_new)
            l_sc[...] = alpha * l_sc[...] + jnp.sum(p, axis=-1, keepdims=True)
            acc_sc[...] = alpha * acc_sc[...] + jnp.dot(p.astype(BF16), v, preferred_element_type=F32)
            m_sc[...] = m_new

        for c in range(kc_ref.shape[0] // tk):
            online_softmax_step(kc_ref[c * tk:(c + 1) * tk, ksl], vc_ref[c * tk:(c + 1) * tk, ksl])

        def latent_chunk(c, carry):
            r0 = pl.multiple_of(c * tk, tk)
            online_softmax_step(kl_ref[pl.ds(r0, tk), ksl], vl_ref[pl.ds(r0, tk), ksl])
            return carry

        lax.fori_loop(0, n_lat_chunks, latent_chunk, 0)
        o = acc_sc[...] / l_sc[...]
        for j in range(group):
            o_ref[:, (g * group + j) * dh:(g * group + j + 1) * dh] = o[j * tq:(j + 1) * tq].astype(BF16)


def _attention(rows, q, k, v, with_ctx):
    B, S, C = rows.B, rows.S, rows.C
    nq = q.shape[1]
    nkv = k.shape[1]
    tq = _pick_tile((256, 128), S, C)
    tk = _pick_tile((256, 128), S, C)
    lat_tiles, ctx_tiles = S // tq, C // tq
    tiles = lat_tiles + (ctx_tiles if with_ctx else 0)
    group = nq // nkv

    def q_map(b, i):
        return (jnp.where(i < lat_tiles, b * lat_tiles + i, rows.n_lat // tq + b * ctx_tiles + (i - lat_tiles)), 0)

    ctx_spec = pl.BlockSpec((C, nkv), lambda b, i: (rows.n_lat // C + b, 0))
    lat_spec = pl.BlockSpec((S, nkv), lambda b, i: (b, 0))
    return pl.pallas_call(
        functools.partial(_attn_kernel, tq=tq, tk=tk, n_lat_tiles=lat_tiles, seq=S),
        grid=(B, tiles),
        in_specs=[pl.BlockSpec((tq, nq), q_map), ctx_spec, ctx_spec, lat_spec, lat_spec],
        out_specs=pl.BlockSpec((tq, nq), q_map),
        out_shape=jax.ShapeDtypeStruct((rows.n if with_ctx else rows.n_lat, nq), BF16),
        scratch_shapes=[pltpu.VMEM((group * tq, 1), F32), pltpu.VMEM((group * tq, 1), F32),
                        pltpu.VMEM((group * tq, ATTN_HEAD_DIM), F32)],
        compiler_params=_cparams("parallel", "parallel"),
        name="attn_core",
    )(q, k, v, k, v)


def _proj_ln_kernel(o_ref, w_ref, bias_ref, x_ref, gate_ref, g_ref, b_ref, out_ref):
    y = jnp.dot(o_ref[...], w_ref[...], preferred_element_type=F32) + bias_ref[...]
    out_ref[...] = _layer_norm(DEEPNORM_ALPHA * x_ref[...] + gate_ref[0] * y, g_ref[...], b_ref[...])


def _proj_ln(rows, o, w, bias, xa, mod, ln_g, ln_b, n_rows):
    D = rows.D
    kdim = o.shape[1]
    tm = rows.tile((512, 256, 128))
    return pl.pallas_call(
        _proj_ln_kernel,
        grid=(n_rows // tm,),
        in_specs=[pl.BlockSpec((tm, kdim), lambda i: (i, 0)), _resident(w.shape), _resident((1, D)),
                  pl.BlockSpec((tm, D), lambda i: (i, 0)), rows.mod_spec(2, tm), _resident((1, D)), _resident((1, D))],
        out_specs=pl.BlockSpec((tm, D), lambda i: (i, 0)),
        out_shape=jax.ShapeDtypeStruct((n_rows, D), F32),
        compiler_params=_cparams("parallel"),
        name="proj_ln",
    )(o, w, bias.reshape(1, D), xa, mod, ln_g.reshape(1, D), ln_b.reshape(1, D))


def _ffn_kernel(x_ref, sh_ref, sc_ref, gate_ref, wg_ref, wu_ref, wd_ref, g_ref, b_ref, out_ref, h_sc, acc_sc):
    f = pl.program_id(1)

    @pl.when(f == 0)
    def _():
        h_sc[...] = (x_ref[...] * (1.0 + sc_ref[0]) + sh_ref[0]).astype(BF16)
        acc_sc[...] = jnp.zeros(acc_sc.shape, F32)

    h = h_sc[...]
    gt = jnp.dot(h, wg_ref[...], preferred_element_type=F32)
    up = jnp.dot(h, wu_ref[...], preferred_element_type=F32)
    act = (gt * jax.nn.sigmoid(gt) * up).astype(BF16)
    acc_sc[...] += jnp.dot(act, wd_ref[...], preferred_element_type=F32)

    @pl.when(f == pl.num_programs(1) - 1)
    def _():
        out_ref[...] = _layer_norm(DEEPNORM_ALPHA * x_ref[...] + gate_ref[0] * acc_sc[...], g_ref[...], b_ref[...])


def _ffn_ln(rows, xa, mod, w_gu, w_down, ln_g, ln_b, n_rows):
    D = rows.D
    F = w_down.shape[0]
    tm = rows.tile((1024, 512, 256, 128))
    tf = _pick_tile((512, 256, 128), F)
    nf = F // tf
    return pl.pallas_call(
        _ffn_kernel,
        grid=(n_rows // tm, nf),
        in_specs=[pl.BlockSpec((tm, D), lambda i, f: (i, 0)),
                  rows.mod_spec(3, tm), rows.mod_spec(4, tm), rows.mod_spec(5, tm),
                  pl.BlockSpec((D, tf), lambda i, f: (0, f)), pl.BlockSpec((D, tf), lambda i, f: (0, nf + f)),
                  pl.BlockSpec((tf, D), lambda i, f: (f, 0)), _resident((1, D)), _resident((1, D))],
        out_specs=pl.BlockSpec((tm, D), lambda i, f: (i, 0)),
        out_shape=jax.ShapeDtypeStruct((n_rows, D), F32),
        scratch_shapes=[pltpu.VMEM((tm, D), BF16), pltpu.VMEM((tm, D), F32)],
        compiler_params=_cparams("parallel", "arbitrary"),
        name="ffn_ln",
    )(xa, mod, mod, mod, w_gu, w_gu, w_down, ln_g.reshape(1, D), ln_b.reshape(1, D))


def _router_kernel(x_ref, sh_ref, sc_ref, r_ref, gates_ref, *, n_experts):
    h = x_ref[...] * (1.0 + sc_ref[0]) + sh_ref[0]
    logits = jnp.dot(h, r_ref[...], preferred_element_type=F32, precision=lax.Precision.HIGHEST)
    lane = lax.broadcasted_iota(jnp.int32, logits.shape, 1)
    lowest = jnp.finfo(F32).min
    lg = jnp.where(lane < n_experts, logits, lowest)
    m1 = jnp.max(lg, axis=-1, keepdims=True)
    i1 = jnp.min(jnp.where(lg == m1, lane, LANES), axis=-1, keepdims=True)
    lg2 = jnp.where(lane == i1, lowest, lg)
    m2 = jnp.max(lg2, axis=-1, keepdims=True)
    i2 = jnp.min(jnp.where(lg2 == m2, lane, LANES), axis=-1, keepdims=True)
    e2 = jnp.exp(m2 - m1)
    denom = 1.0 + e2
    gates_ref[...] = jnp.where(lane == i1, 1.0 / denom, 0.0) + jnp.where(lane == i2, e2 / denom, 0.0)


def _router(rows, xa, mod, router, n_rows):
    D = rows.D
    n_experts = router.shape[1]
    tm = rows.tile((512, 256, 128))
    router_pad = jnp.pad(router, ((0, 0), (0, LANES - n_experts)))
    return pl.pallas_call(
        functools.partial(_router_kernel, n_experts=n_experts),
        grid=(n_rows // tm,),
        in_specs=[pl.BlockSpec((tm, D), lambda i: (i, 0)), rows.mod_spec(3, tm), rows.mod_spec(4, tm),
                  _resident((D, LANES))],
        out_specs=pl.BlockSpec((tm, LANES), lambda i: (i, 0)),
        out_shape=jax.ShapeDtypeStruct((n_rows, LANES), F32),
        compiler_params=_cparams("parallel"),
        name="moe_router",
    )(xa, mod, mod, router_pad)


def _moe_kernel(x_ref, sh_ref, sc_ref, gate_ref, gates_ref, wg_ref, wu_ref, wd_ref, g_ref, b_ref, out_ref,
                h_sc, acc_sc):
    e, f = pl.program_id(1), pl.program_id(2)

    @pl.when((e == 0) & (f == 0))
    def _():
        h_sc[...] = (x_ref[...] * (1.0 + sc_ref[0]) + sh_ref[0]).astype(BF16)
        acc_sc[...] = jnp.zeros(acc_sc.shape, F32)

    h = h_sc[...]
    gt = jnp.dot(h, wg_ref[0], preferred_element_type=F32)
    up = jnp.dot(h, wu_ref[0], preferred_element_type=F32)
    act = (gt * jax.nn.sigmoid(gt) * up).astype(BF16)
    lane = lax.broadcasted_iota(jnp.int32, gates_ref.shape, 1)
    expert_gate = jnp.sum(jnp.where(lane == e, gates_ref[...], 0.0), axis=-1, keepdims=True)
    acc_sc[...] += expert_gate * jnp.dot(act, wd_ref[0], preferred_element_type=F32)

    @pl.when((e == pl.num_programs(1) - 1) & (f == pl.num_programs(2) - 1))
    def _():
        out_ref[...] = _layer_norm(DEEPNORM_ALPHA * x_ref[...] + gate_ref[0] * acc_sc[...], g_ref[...], b_ref[...])


def _moe_ln(rows, xa, mod, gates, w_gu, w_down, ln_g, ln_b, n_rows):
    D = rows.D
    E, F = w_down.shape[0], w_down.shape[1]
    tm = rows.tile((1024, 512, 256, 128))
    tf = _pick_tile((512, 256, 128), F)
    nf = F // tf
    return pl.pallas_call(
        _moe_kernel,
        grid=(n_rows // tm, E, nf),
        in_specs=[pl.BlockSpec((tm, D), lambda i, e, f: (i, 0)),
                  rows.mod_spec(3, tm), rows.mod_spec(4, tm), rows.mod_spec(5, tm),
                  pl.BlockSpec((tm, LANES), lambda i, e, f: (i, 0)),
                  pl.BlockSpec((1, D, tf), lambda i, e, f: (e, 0, f)),
                  pl.BlockSpec((1, D, tf), lambda i, e, f: (e, 0, nf + f)),
                  pl.BlockSpec((1, tf, D), lambda i, e, f: (e, f, 0)), _resident((1, D)), _resident((1, D))],
        out_specs=pl.BlockSpec((tm, D), lambda i, e, f: (i, 0)),
        out_shape=jax.ShapeDtypeStruct((n_rows, D), F32),
        scratch_shapes=[pltpu.VMEM((tm, D), BF16), pltpu.VMEM((tm, D), F32)],
        compiler_params=_cparams("parallel", "arbitrary", "arbitrary"),
        name="moe_ln",
    )(xa, mod, mod, mod, gates, w_gu, w_gu, w_down, ln_g.reshape(1, D), ln_b.reshape(1, D))


def _gmlp_kernel(x_ref, sh_ref, sc_ref, gate_ref, win_ref, bin_ref, vg_ref, vb_ref, ws_ref, bs_ref, wout_ref,
                 bout_ref, g_ref, b_ref, out_ref, gated_sc):
    tm = x_ref.shape[0]
    inner = wout_ref.shape[0]
    gdim = inner // CMLP_GROUPS
    x = x_ref[...]
    h = (x * (1.0 + sc_ref[0]) + sh_ref[0]).astype(BF16)
    v = jax.nn.gelu(jnp.dot(h, win_ref[:, inner:], preferred_element_type=F32) + bin_ref[:, inner:])
    v = _layer_norm(v, vg_ref[...], vb_ref[...]).astype(BF16)
    for gi in range(CMLP_GROUPS):
        cs = slice(gi * gdim, (gi + 1) * gdim)
        u = jax.nn.gelu(jnp.dot(h, win_ref[:, cs], preferred_element_type=F32) + bin_ref[:, cs])
        w_s = ws_ref[gi]
        bias = bs_ref[:, gi:gi + 1]
        for c in range(tm // CMLP_CHUNK):
            rs = slice(c * CMLP_CHUNK, (c + 1) * CMLP_CHUNK)
            mixed = jnp.dot(w_s, v[rs, cs], preferred_element_type=F32) + bias
            gated_sc[rs, cs] = (u[rs] * mixed).astype(BF16)
    y = jnp.dot(gated_sc[...], wout_ref[...], preferred_element_type=F32) + bout_ref[...]
    out_ref[...] = _layer_norm(DEEPNORM_ALPHA * x + gate_ref[0] * y, g_ref[...], b_ref[...])


def _gmlp_ln(rows, xa, mod, w_in, b_in, vg, vb, w_s, b_s, w_out, b_out, ln_g, ln_b, n_rows):
    D = rows.D
    inner = w_out.shape[0]
    tm = rows.tile((256, 128))
    return pl.pallas_call(
        _gmlp_kernel,
        grid=(n_rows // tm,),
        in_specs=[pl.BlockSpec((tm, D), lambda i: (i, 0)), rows.mod_spec(0, tm), rows.mod_spec(1, tm),
                  rows.mod_spec(2, tm), _resident(w_in.shape), _resident((1, 2 * inner)), _resident((1, inner)),
                  _resident((1, inner)), _resident(w_s.shape), _resident((CMLP_CHUNK, CMLP_GROUPS)),
                  _resident(w_out.shape), _resident((1, D)), _resident((1, D)), _resident((1, D))],
        out_specs=pl.BlockSpec((tm, D), lambda i: (i, 0)),
        out_shape=jax.ShapeDtypeStruct((n_rows, D), F32),
        scratch_shapes=[pltpu.VMEM((tm, inner), BF16)],
        compiler_params=_cparams("parallel"),
        name="gmlp_ln",
    )(xa, mod, mod, mod, w_in, b_in.reshape(1, -1), vg.reshape(1, -1), vb.reshape(1, -1), w_s, b_s.T, w_out,
      b_out.reshape(1, D), ln_g.reshape(1, D), ln_b.reshape(1, D))


def _ret_proj_kernel(x_ref, sh_ref, sc_ref, w_ref, cos_ref, sin_ref, q_ref, k_ref, v_ref, sg_ref, *, nk, nv, dk):
    h = (x_ref[...] * (1.0 + sc_ref[0]) + sh_ref[0]).astype(BF16)
    cos, sin = cos_ref[...], sin_ref[...]
    k_scale = dk ** -0.5

    def rope_store(dst_ref, col0, scale):
        full = jnp.dot(h, w_ref[:, col0:col0 + nk], preferred_element_type=F32) * scale
        for j in range(nk // LANES):
            t = full[:, j * LANES:(j + 1) * LANES]
            ts = slice((j % (dk // LANES)) * LANES, (j % (dk // LANES) + 1) * LANES)
            dst_ref[:, j * LANES:(j + 1) * LANES] = (t * cos[:, ts] + _swap_pairs(t, dk // 4) * sin[:, ts]).astype(BF16)

    rope_store(q_ref, 0, 1.0)
    rope_store(k_ref, nk, k_scale)
    blk = 512
    for j in range(nv // blk):
        v_ref[:, j * blk:(j + 1) * blk] = jnp.dot(
            h, w_ref[:, 2 * nk + j * blk:2 * nk + (j + 1) * blk], preferred_element_type=F32).astype(BF16)
        gt = jnp.dot(h, w_ref[:, 2 * nk + nv + j * blk:2 * nk + nv + (j + 1) * blk], preferred_element_type=F32)
        sg_ref[:, j * blk:(j + 1) * blk] = (gt * jax.nn.sigmoid(gt)).astype(BF16)


def _ret_project(rows, xa, mod, w, cos, sin):
    D = rows.D
    nk = D
    nv = (w.shape[1] - 2 * nk) // 2
    dk = nk // RET_HEADS
    tm = rows.tile((512, 256, 128))
    row_spec = lambda width: pl.BlockSpec((tm, width), lambda i: (i, 0))
    return pl.pallas_call(
        functools.partial(_ret_proj_kernel, nk=nk, nv=nv, dk=dk),
        grid=(rows.n // tm,),
        in_specs=[row_spec(D), rows.mod_spec(0, tm), rows.mod_spec(1, tm), _resident(w.shape),
                  rows.pos_spec(tm, dk), rows.pos_spec(tm, dk)],
        out_specs=[row_spec(nk), row_spec(nk), row_spec(nv), row_spec(nv)],
        out_shape=[jax.ShapeDtypeStruct((rows.n, nk), BF16), jax.ShapeDtypeStruct((rows.n, nk), BF16),
                   jax.ShapeDtypeStruct((rows.n, nv), BF16), jax.ShapeDtypeStruct((rows.n, nv), BF16)],
        compiler_params=_cparams("parallel"),
        name="ret_proj",
    )(xa, mod, mod, w, cos, sin)


def _ret_scan_kernel(lg_ref, q_ref, k_ref, v_ref, *rest, reverse):
    if reverse:
        of_ref, sg_ref, o_ref, state_sc = rest
    else:
        o_ref, state_sc = rest
    ch = q_ref.shape[0]
    head = pl.program_id(1)
    lg = lg_ref[head]

    @pl.when(pl.program_id(2) == 0)
    def _():
        state_sc[...] = jnp.zeros(state_sc.shape, F32)

    q, k, v = q_ref[...], k_ref[...], v_ref[...]
    ii = lax.broadcasted_iota(jnp.int32, (ch, ch), 0)
    jj = lax.broadcasted_iota(jnp.int32, (ch, ch), 1)
    dist = (jj - ii) if reverse else (ii - jj)
    inner_decay = jnp.where(dist >= 0, jnp.exp(jnp.maximum(dist, 0).astype(F32) * lg), 0.0)
    pos = lax.broadcasted_iota(jnp.int32, (ch, 1), 0)
    order = (ch - 1 - pos) if reverse else pos
    cross_decay = jnp.exp((order + 1).astype(F32) * lg)
    state_weight = jnp.exp((ch - 1 - order).astype(F32) * lg)

    scores = lax.dot_general(q, k, (((1,), (1,)), ((), ())), preferred_element_type=F32) * inner_decay
    state = state_sc[...]
    out = (jnp.dot(scores.astype(BF16), v, preferred_element_type=F32)
           + jnp.dot(q, state.astype(BF16), preferred_element_type=F32) * cross_decay)
    kw = (k.astype(F32) * state_weight).astype(BF16)
    chunk_decay = jnp.exp(jnp.full((1, 1), ch, F32) * lg)
    state_sc[...] = state * chunk_decay + lax.dot_general(
        kw, v, (((0,), (0,)), ((), ())), preferred_element_type=F32)

    if reverse:
        o = out + of_ref[...]
        o = o * lax.rsqrt(jnp.mean(o * o, axis=-1, keepdims=True) + RMS_EPS)
        o_ref[...] = (sg_ref[...].astype(F32) * o).astype(BF16)
    else:
        o_ref[...] = out


def _ret_scan(rows, log_gamma, q, k, v, o_fwd=None, sgate=None, *, reverse):
    B, S, C = rows.B, rows.S, rows.C
    ch = RET_CHUNK
    dk = q.shape[1] // RET_HEADS
    dv = v.shape[1] // RET_HEADS
    n_c, n_l = C // ch, S // ch

    def row_block(b, t):
        if reverse:
            return jnp.where(t < n_c, rows.n_lat // ch + b * n_c + (n_c - 1 - t), b * n_l + (n_l - 1 - (t - n_c)))
        return jnp.where(t < n_c, rows.n_lat // ch + b * n_c + t, b * n_l + (t - n_c))

    kspec = pl.BlockSpec((ch, dk), lambda b, h, t: (row_block(b, t), h))
    vspec = pl.BlockSpec((ch, dv), lambda b, h, t: (row_block(b, t), h))
    in_specs = [pl.BlockSpec(memory_space=pltpu.SMEM), kspec, kspec, vspec]
    args = [log_gamma, q, k, v]
    if reverse:
        in_specs += [vspec, vspec]
        args += [o_fwd, sgate]
    return pl.pallas_call(
        functools.partial(_ret_scan_kernel, reverse=reverse),
        grid=(B, RET_HEADS, n_c + n_l),
        in_specs=in_specs,
        out_specs=vspec,
        out_shape=jax.ShapeDtypeStruct((rows.n, v.shape[1]), BF16 if reverse else F32),
        scratch_shapes=[pltpu.VMEM((dk, dv), F32)],
        compiler_params=_cparams("parallel", "parallel", "arbitrary"),
        name="ret_scan_bwd" if reverse else "ret_scan_fwd",
    )(*args)


def kernel(x, c, ctx, c_ctx, l0_ada_w, l0_ada_b, l0_ln1_g, l0_ln1_b, l0_ln2_g, l0_ln2_b, l0_attn_wqkv, l0_attn_q_norm, l0_attn_k_norm, l0_attn_wo, l0_ffn_w_gu, l0_ffn_w_down, l1_ada_w, l1_ada_b, l1_ln1_g, l1_ln1_b, l1_ln2_g, l1_ln2_b, l1_cmlp_w_in, l1_cmlp_b_in, l1_cmlp_v_norm_g, l1_cmlp_v_norm_b, l1_cmlp_w_s, l1_cmlp_b_s, l1_cmlp_w_out, l1_cmlp_b_out, l1_moe_router, l1_moe_w_gu, l1_moe_w_down, l2_ada_w, l2_ada_b, l2_ln1_g, l2_ln1_b, l2_ln2_g, l2_ln2_b, l2_ret_wqkvg, l2_ret_decay, l2_ret_wo, l2_ffn_w_gu, l2_ffn_w_down, l3_ada_w, l3_ada_b, l3_ln1_g, l3_ln1_b, l3_ln2_g, l3_ln2_b, l3_attn_wqkv, l3_attn_q_norm, l3_attn_k_norm, l3_attn_wo, l3_moe_router, l3_moe_w_gu, l3_moe_w_down):
    B, S, D = x.shape
    C = ctx.shape[1]
    rows = _Rows(B, S, C, D)
    assert S % GRID_W == 0 and C % RET_CHUNK == 0 and S % RET_CHUNK == 0

    xa = jnp.concatenate([x.reshape(rows.n_lat, D), ctx.reshape(rows.n_ctx, D)], axis=0)
    n_cond = -(-(B + 1) // 8) * 8
    cond = jnp.concatenate([c, c_ctx[None, :], jnp.zeros((n_cond - B - 1, D), F32)], axis=0)

    def modulation(ada_w, ada_b):
        return _ada(cond, ada_w, ada_b).reshape(n_cond * 6, 1, D)

    pad = rows.tile((512, 256, 128))
    attn_cos, attn_sin = _rope_tables(S, ATTN_HEAD_DIM, pad)
    ret_cos, ret_sin = _rope_tables(S, D // RET_HEADS, pad)
    zero_bias = jnp.zeros((D,), F32)
    bf = lambda w: w.astype(BF16)

    def attention_layer(xa, mod, wqkv, qg, kg, wo, ln_g, ln_b, with_ctx):
        n_rows = rows.n if with_ctx else rows.n_lat
        q, k, v = _attn_project(rows, xa, mod, bf(wqkv), qg, kg, attn_cos, attn_sin)
        o = _attention(rows, q, k, v, with_ctx)
        return _proj_ln(rows, o, bf(wo), zero_bias, xa, mod, ln_g, ln_b, n_rows)

    def retention_layer(xa, mod, wqkvg, decay, wo, ln_g, ln_b):
        log_gamma = -jnp.exp(decay.astype(F32))
        q, k, v, sg = _ret_project(rows, xa, mod, bf(wqkvg), ret_cos, ret_sin)
        o_f = _ret_scan(rows, log_gamma[0], q, k, v, reverse=False)
        o = _ret_scan(rows, log_gamma[1], q, k, v, o_f, sg, reverse=True)
        return _proj_ln(rows, o, bf(wo), zero_bias, xa, mod, ln_g, ln_b, rows.n)

    mod = modulation(l0_ada_w, l0_ada_b)
    xa = attention_layer(xa, mod, l0_attn_wqkv, l0_attn_q_norm, l0_attn_k_norm, l0_attn_wo, l0_ln1_g, l0_ln1_b, True)
    xa = _ffn_ln(rows, xa, mod, bf(l0_ffn_w_gu), bf(l0_ffn_w_down), l0_ln2_g, l0_ln2_b, rows.n)

    mod = modulation(l1_ada_w, l1_ada_b)
    xa = _gmlp_ln(rows, xa, mod, bf(l1_cmlp_w_in), l1_cmlp_b_in, l1_cmlp_v_norm_g, l1_cmlp_v_norm_b,
                  bf(l1_cmlp_w_s), l1_cmlp_b_s, bf(l1_cmlp_w_out), l1_cmlp_b_out, l1_ln1_g, l1_ln1_b, rows.n)
    gates = _router(rows, xa, mod, l1_moe_router, rows.n)
    xa = _moe_ln(rows, xa, mod, gates, bf(l1_moe_w_gu), bf(l1_moe_w_down), l1_ln2_g, l1_ln2_b, rows.n)

    mod = modulation(l2_ada_w, l2_ada_b)
    xa = retention_layer(xa, mod, l2_ret_wqkvg, l2_ret_decay, l2_ret_wo, l2_ln1_g, l2_ln1_b)
    xa = _ffn_ln(rows, xa, mod, bf(l2_ffn_w_gu), bf(l2_ffn_w_down), l2_ln2_g, l2_ln2_b, rows.n)

    mod = modulation(l3_ada_w, l3_ada_b)
    xl = attention_layer(xa, mod, l3_attn_wqkv, l3_attn_q_norm, l3_attn_k_norm, l3_attn_wo, l3_ln1_g, l3_ln1_b, False)
    gates = _router(rows, xl, mod, l3_moe_router, rows.n_lat)
    xl = _moe_ln(rows, xl, mod, gates, bf(l3_moe_w_gu), bf(l3_moe_w_down), l3_ln2_g, l3_ln2_b, rows.n_lat)
    return xl.reshape(B, S, D)
```

```python
import functools

import jax
import jax.numpy as jnp
from jax import lax
from jax.experimental import pallas as pl
from jax.experimental.pallas import tpu as pltpu

F32 = jnp.float32
BF16 = jnp.bfloat16

DEPTH = 4
GRID_W = 64
ROPE_THETA = 10000.0
DEEPNORM_ALPHA = (2 * DEPTH) ** 0.25
LN_EPS = 1e-5
RMS_EPS = 1e-6
ATTN_HEAD_DIM = 128
ATTN_KV_HEADS = 2
CMLP_CHUNK = 128
CMLP_GROUPS = 8
RET_HEADS = 4
RET_CHUNK = 128
MOE_EXPERTS = 8

LANES = 128
VMEM_LIMIT_BYTES = 56 * 1024 * 1024
NEG_BIG = -1e30


def _cparams(*sem):
    return pltpu.CompilerParams(dimension_semantics=sem, vmem_limit_bytes=VMEM_LIMIT_BYTES)


def _resident(shape):
    nd = len(shape)
    return pl.BlockSpec(shape, lambda *_: (0,) * nd)


def _layer_norm(z, g, b):
    mu = jnp.mean(z, axis=-1, keepdims=True)
    zc = z - mu
    var = jnp.mean(zc * zc, axis=-1, keepdims=True)
    return zc * lax.rsqrt(var + LN_EPS) * g + b


def _pick_tile(candidates, *extents):
    for t in candidates:
        if all(e % t == 0 for e in extents):
            return t
    raise ValueError(f"no tile in {candidates} divides {extents}")


class _Rows:
    def __init__(self, batch, seq, ctx_len, dim):
        self.B, self.S, self.C, self.D = batch, seq, ctx_len, dim
        self.n_lat = batch * seq
        self.n_ctx = batch * ctx_len
        self.n = self.n_lat + self.n_ctx

    def tile(self, candidates):
        return _pick_tile(candidates, self.S, self.n_ctx)

    def mod_spec(self, part, tm):
        S, B = self.S, self.B
        return pl.BlockSpec((1, 1, self.D), lambda i, *_: (jnp.minimum(i * tm // S, B) * 6 + part, 0, 0))

    def pos_spec(self, tm, width):
        n_lat_tiles, per_seq = self.n_lat // tm, self.S // tm
        return pl.BlockSpec((tm, width), lambda i, *_: (jnp.where(i < n_lat_tiles, i % per_seq, per_seq), 0))


def _ada_kernel(c_ref, w_ref, b_ref, o_ref):
    c = c_ref[...]
    s = c * jax.nn.sigmoid(c)
    o_ref[...] = jnp.dot(s, w_ref[...], preferred_element_type=F32, precision=lax.Precision.HIGHEST) + b_ref[...]


def _ada(cond, w, b):
    r, d = cond.shape
    n = w.shape[1]
    tn = _pick_tile((1536, 1024, 512, 128), n)
    return pl.pallas_call(
        _ada_kernel,
        grid=(n // tn,),
        in_specs=[_resident((r, d)), pl.BlockSpec((d, tn), lambda j: (0, j)), pl.BlockSpec((1, tn), lambda j: (0, j))],
        out_specs=pl.BlockSpec((r, tn), lambda j: (0, j)),
        out_shape=jax.ShapeDtypeStruct((r, n), F32),
        compiler_params=_cparams("parallel"),
        name="ada",
    )(cond, w, b.reshape(1, n))


def _rope_tables(seq, head_dim, pad_rows):
    nf = head_dim // 4
    inv_freq = ROPE_THETA ** (-jnp.arange(nf, dtype=F32) / nf)
    t = jnp.arange(seq, dtype=jnp.int32)
    row = (t // GRID_W).astype(F32)[:, None] * inv_freq
    col = (t % GRID_W).astype(F32)[:, None] * inv_freq
    cos = jnp.concatenate([jnp.cos(row), jnp.cos(row), jnp.cos(col), jnp.cos(col)], axis=-1)
    sin = jnp.concatenate([-jnp.sin(row), jnp.sin(row), -jnp.sin(col), jnp.sin(col)], axis=-1)
    cos = jnp.concatenate([cos, jnp.ones((pad_rows, head_dim), F32)], axis=0)
    sin = jnp.concatenate([sin, jnp.zeros((pad_rows, head_dim), F32)], axis=0)
    return cos, sin


def _swap_pairs(y, half):
    if 2 * half == LANES:
        return pltpu.roll(y, half, 1)
    lane = lax.broadcasted_iota(jnp.int32, y.shape, 1)
    return jnp.where(lane % (2 * half) < half, pltpu.roll(y, LANES - half, 1), pltpu.roll(y, half, 1))


KEY_BLOCK = 256
LOG2_E = 1.4426950408889634


def _qkv_kernel(x_ref, sh_ref, sc_ref, w_ref, qg_ref, kg_ref, cos_ref, sin_ref, q_ref, kt_ref, v_ref, *, nq, nkv):
    dh = ATTN_HEAD_DIM
    h = (x_ref[...] * (1.0 + sc_ref[0]) + sh_ref[0]).astype(BF16)
    cos, sin = cos_ref[...], sin_ref[...]
    scale = dh ** -0.5 * LOG2_E

    def normed_rope(t, g):
        y = t * lax.rsqrt(jnp.mean(t * t, axis=-1, keepdims=True) + RMS_EPS) * g
        return y * cos + _swap_pairs(y, dh // 4) * sin

    t = jnp.dot(h, w_ref[...], preferred_element_type=F32)
    for hd in range(nq // dh):
        q_ref[:, hd * dh:(hd + 1) * dh] = (normed_rope(t[:, hd * dh:(hd + 1) * dh], qg_ref[...]) * scale).astype(BF16)
    k = jnp.concatenate([normed_rope(t[:, nq + hd * dh:nq + (hd + 1) * dh], kg_ref[...]) for hd in range(nkv // dh)],
                        axis=1)
    for j in range(kt_ref.shape[0]):
        kt_ref[j] = k[j * KEY_BLOCK:(j + 1) * KEY_BLOCK, :].T.astype(BF16)
    v_ref[...] = t[:, nq + nkv:].astype(BF16)


def _attn_project(rows, xa, mod, w, qg, kg, cos, sin):
    D = rows.D
    nkv = ATTN_KV_HEADS * ATTN_HEAD_DIM
    nq = w.shape[1] - 2 * nkv
    tm = rows.tile((512, 256))
    row_spec = lambda width: pl.BlockSpec((tm, width), lambda i: (i, 0))
    return pl.pallas_call(
        functools.partial(_qkv_kernel, nq=nq, nkv=nkv),
        grid=(rows.n // tm,),
        in_specs=[row_spec(D), rows.mod_spec(0, tm), rows.mod_spec(1, tm), _resident(w.shape),
                  _resident((1, ATTN_HEAD_DIM)), _resident((1, ATTN_HEAD_DIM)),
                  rows.pos_spec(tm, ATTN_HEAD_DIM), rows.pos_spec(tm, ATTN_HEAD_DIM)],
        out_specs=[row_spec(nq), pl.BlockSpec((tm // KEY_BLOCK, nkv, KEY_BLOCK), lambda i: (i, 0, 0)), row_spec(nkv)],
        out_shape=[jax.ShapeDtypeStruct((rows.n, nq), BF16),
                   jax.ShapeDtypeStruct((rows.n // KEY_BLOCK, nkv, KEY_BLOCK), BF16),
                   jax.ShapeDtypeStruct((rows.n, nkv), BF16)],
        compiler_params=_cparams("parallel"),
        name="attn_qkv",
    )(xa, mod, mod, w, qg.reshape(1, -1), kg.reshape(1, -1), cos, sin)


def _attn_kernel(q_ref, kc_ref, vc_ref, kl_ref, vl_ref, o_ref, m_sc, acc_sc, *, tq, blocks_per_chunk, n_lat_tiles):
    dh = ATTN_HEAD_DIM
    group = q_ref.shape[1] // dh // ATTN_KV_HEADS
    is_latent = pl.program_id(1) < n_lat_tiles
    n_lat_chunks = jnp.where(is_latent, kl_ref.shape[0] // blocks_per_chunk, 0)
    chunk = blocks_per_chunk * KEY_BLOCK

    for g in range(ATTN_KV_HEADS):
        gs = slice(g * dh, (g + 1) * dh)
        qg = jnp.concatenate([q_ref[:, (g * group + j) * dh:(g * group + j + 1) * dh] for j in range(group)], axis=0)
        m_sc[...] = jnp.full(m_sc.shape, NEG_BIG, F32)
        acc_sc[...] = jnp.zeros(acc_sc.shape, F32)

        def online_softmax_step(kt, v):
            width = kt.shape[1]
            s = jnp.dot(qg, kt, preferred_element_type=F32)
            m_old = m_sc[...]
            m_new = jnp.maximum(m_old, jnp.max(s, axis=-1, keepdims=True))
            p = jnp.exp2(s - jnp.concatenate([m_new] * (width // LANES), axis=1)).astype(BF16)
            alpha = jnp.exp2(m_old - m_new)
            v_ones = jnp.concatenate([v, jnp.ones((width, dh), BF16)], axis=1)
            acc_sc[...] = (jnp.concatenate([alpha, alpha], axis=1) * acc_sc[...]
                           + jnp.dot(p, v_ones, preferred_element_type=F32))
            m_sc[...] = m_new

        for c in range(kc_ref.shape[0]):
            online_softmax_step(kc_ref[c, gs, :], vc_ref[c * KEY_BLOCK:(c + 1) * KEY_BLOCK, gs])

        def latent_chunk(c, carry):
            kt = jnp.concatenate([kl_ref[c * blocks_per_chunk + j, gs, :] for j in range(blocks_per_chunk)], axis=1)
            r0 = pl.multiple_of(c * chunk, chunk)
            online_softmax_step(kt, vl_ref[pl.ds(r0, chunk), gs])
            return carry

        lax.fori_loop(0, n_lat_chunks, latent_chunk, 0)
        acc = acc_sc[...]
        o = acc[:, :dh] / acc[:, dh:]
        for j in range(group):
            o_ref[:, (g * group + j) * dh:(g * group + j + 1) * dh] = o[j * tq:(j + 1) * tq].astype(BF16)


def _attention(rows, q, kt, v, with_ctx):
    B, S, C = rows.B, rows.S, rows.C
    nq = q.shape[1]
    nkv = v.shape[1]
    assert S % KEY_BLOCK == 0 and C % KEY_BLOCK == 0 and rows.n_lat % C == 0
    tq = _pick_tile((256, 128), S, C)
    blocks_per_chunk = 2 if S % (2 * KEY_BLOCK) == 0 else 1
    lat_tiles, ctx_tiles = S // tq, C // tq
    tiles = lat_tiles + (ctx_tiles if with_ctx else 0)
    group = nq // nkv

    def q_map(b, i):
        return (jnp.where(i < lat_tiles, b * lat_tiles + i, rows.n_lat // tq + b * ctx_tiles + (i - lat_tiles)), 0)

    kt_ctx = pl.BlockSpec((C // KEY_BLOCK, nkv, KEY_BLOCK), lambda b, i: (rows.n_lat // C + b, 0, 0))
    kt_lat = pl.BlockSpec((S // KEY_BLOCK, nkv, KEY_BLOCK), lambda b, i: (b, 0, 0))
    v_ctx = pl.BlockSpec((C, nkv), lambda b, i: (rows.n_lat // C + b, 0))
    v_lat = pl.BlockSpec((S, nkv), lambda b, i: (b, 0))
    return pl.pallas_call(
        functools.partial(_attn_kernel, tq=tq, blocks_per_chunk=blocks_per_chunk, n_lat_tiles=lat_tiles),
        grid=(B, tiles),
        in_specs=[pl.BlockSpec((tq, nq), q_map), kt_ctx, v_ctx, kt_lat, v_lat],
        out_specs=pl.BlockSpec((tq, nq), q_map),
        out_shape=jax.ShapeDtypeStruct((rows.n if with_ctx else rows.n_lat, nq), BF16),
        scratch_shapes=[pltpu.VMEM((group * tq, LANES), F32), pltpu.VMEM((group * tq, 2 * ATTN_HEAD_DIM), F32)],
        compiler_params=_cparams("parallel", "parallel"),
        name="attn_core",
    )(q, kt, v, kt, v)


def _proj_ln_kernel(o_ref, w_ref, bias_ref, x_ref, gate_ref, g_ref, b_ref, out_ref):
    y = jnp.dot(o_ref[...], w_ref[...], preferred_element_type=F32) + bias_ref[...]
    out_ref[...] = _layer_norm(DEEPNORM_ALPHA * x_ref[...] + gate_ref[0] * y, g_ref[...], b_ref[...])


def _proj_ln(rows, o, w, bias, xa, mod, ln_g, ln_b, n_rows):
    D = rows.D
    kdim = o.shape[1]
    tm = rows.tile((512, 256, 128))
    return pl.pallas_call(
        _proj_ln_kernel,
        grid=(n_rows // tm,),
        in_specs=[pl.BlockSpec((tm, kdim), lambda i: (i, 0)), _resident(w.shape), _resident((1, D)),
                  pl.BlockSpec((tm, D), lambda i: (i, 0)), rows.mod_spec(2, tm), _resident((1, D)), _resident((1, D))],
        out_specs=pl.BlockSpec((tm, D), lambda i: (i, 0)),
        out_shape=jax.ShapeDtypeStruct((n_rows, D), F32),
        compiler_params=_cparams("parallel"),
        name="proj_ln",
    )(o, w, bias.reshape(1, D), xa, mod, ln_g.reshape(1, D), ln_b.reshape(1, D))


def _ffn_kernel(x_ref, sh_ref, sc_ref, gate_ref, wg_ref, wu_ref, wd_ref, g_ref, b_ref, out_ref, h_sc, acc_sc):
    f = pl.program_id(1)

    @pl.when(f == 0)
    def _():
        h_sc[...] = (x_ref[...] * (1.0 + sc_ref[0]) + sh_ref[0]).astype(BF16)
        acc_sc[...] = jnp.zeros(acc_sc.shape, F32)

    h = h_sc[...]
    gt = jnp.dot(h, wg_ref[...], preferred_element_type=F32)
    up = jnp.dot(h, wu_ref[...], preferred_element_type=F32)
    act = (gt * jax.nn.sigmoid(gt) * up).astype(BF16)
    acc_sc[...] += jnp.dot(act, wd_ref[...], preferred_element_type=F32)

    @pl.when(f == pl.num_programs(1) - 1)
    def _():
        out_ref[...] = _layer_norm(DEEPNORM_ALPHA * x_ref[...] + gate_ref[0] * acc_sc[...], g_ref[...], b_ref[...])


def _ffn_ln(rows, xa, mod, w_gu, w_down, ln_g, ln_b, n_rows):
    D = rows.D
    F = w_down.shape[0]
    tm = rows.tile((1024, 512, 256, 128))
    tf = _pick_tile((512, 256, 128), F)
    nf = F // tf
    return pl.pallas_call(
        _ffn_kernel,
        grid=(n_rows // tm, nf),
        in_specs=[pl.BlockSpec((tm, D), lambda i, f: (i, 0)),
                  rows.mod_spec(3, tm), rows.mod_spec(4, tm), rows.mod_spec(5, tm),
                  pl.BlockSpec((D, tf), lambda i, f: (0, f)), pl.BlockSpec((D, tf), lambda i, f: (0, nf + f)),
                  pl.BlockSpec((tf, D), lambda i, f: (f, 0)), _resident((1, D)), _resident((1, D))],
        out_specs=pl.BlockSpec((tm, D), lambda i, f: (i, 0)),
        out_shape=jax.ShapeDtypeStruct((n_rows, D), F32),
        scratch_shapes=[pltpu.VMEM((tm, D), BF16), pltpu.VMEM((tm, D), F32)],
        compiler_params=_cparams("parallel", "arbitrary"),
        name="ffn_ln",
    )(xa, mod, mod, mod, w_gu, w_gu, w_down, ln_g.reshape(1, D), ln_b.reshape(1, D))


MOE_ROW_BLOCK = 256


def _router_kernel(x_ref, sh_ref, sc_ref, r_ref, h_ref, gates_ref, *, n_experts):
    h = x_ref[...] * (1.0 + sc_ref[0]) + sh_ref[0]
    h_ref[...] = h.astype(BF16)
    logits = jnp.dot(h, r_ref[...], preferred_element_type=F32, precision=lax.Precision.HIGHEST)
    lane = lax.broadcasted_iota(jnp.int32, logits.shape, 1)
    lowest = jnp.finfo(F32).min
    lg = jnp.where(lane < n_experts, logits, lowest)
    m1 = jnp.max(lg, axis=-1, keepdims=True)
    i1 = jnp.min(jnp.where(lg == m1, lane, LANES), axis=-1, keepdims=True)
    lg2 = jnp.where(lane == i1, lowest, lg)
    m2 = jnp.max(lg2, axis=-1, keepdims=True)
    i2 = jnp.min(jnp.where(lg2 == m2, lane, LANES), axis=-1, keepdims=True)
    e2 = jnp.exp(m2 - m1)
    denom = 1.0 + e2
    gates = jnp.where(lane == i1, 1.0 / denom, 0.0) + jnp.where(lane == i2, e2 / denom, 0.0)
    gates_ref[...] = gates.T[:n_experts, :]


def _router(rows, xa, mod, router, n_rows):
    D = rows.D
    n_experts = router.shape[1]
    tm = rows.tile((512, 256, 128))
    router_pad = jnp.pad(router, ((0, 0), (0, LANES - n_experts)))
    return pl.pallas_call(
        functools.partial(_router_kernel, n_experts=n_experts),
        grid=(n_rows // tm,),
        in_specs=[pl.BlockSpec((tm, D), lambda i: (i, 0)), rows.mod_spec(3, tm), rows.mod_spec(4, tm),
                  _resident((D, LANES))],
        out_specs=[pl.BlockSpec((tm, D), lambda i: (i, 0)), pl.BlockSpec((n_experts, tm), lambda i: (0, i))],
        out_shape=[jax.ShapeDtypeStruct((n_rows, D), BF16), jax.ShapeDtypeStruct((n_experts, n_rows), F32)],
        compiler_params=_cparams("parallel"),
        name="moe_router",
    )(xa, mod, mod, router_pad)


def _plan_kernel(g_ref, rank_ref, cnt_ref):
    n_experts, tile = g_ref.shape
    routed = g_ref[...] > 0.0
    ones = jnp.where(routed, 1.0, 0.0)
    before = jnp.where(lax.broadcasted_iota(jnp.int32, (LANES, LANES), 0)
                       < lax.broadcasted_iota(jnp.int32, (LANES, LANES), 1), 1.0, 0.0)
    seen = jnp.zeros((n_experts, 1), F32)
    for c in range(tile // LANES):
        cs = slice(c * LANES, (c + 1) * LANES)
        rank = jnp.dot(ones[:, cs], before, preferred_element_type=F32) + seen
        rank_ref[:, cs] = jnp.where(routed[:, cs], rank, -1.0).astype(jnp.int32)
        seen = seen + jnp.sum(ones[:, cs], axis=1, keepdims=True)
    cnt_ref[0] = jnp.broadcast_to(seen, (n_experts, LANES)).astype(jnp.int32)


def _moe_plan(gates_t, tile):
    n_experts, n_rows = gates_t.shape
    n_tiles = n_rows // tile
    rank, counts = pl.pallas_call(
        _plan_kernel,
        grid=(n_tiles,),
        in_specs=[pl.BlockSpec((n_experts, tile), lambda s: (0, s))],
        out_specs=[pl.BlockSpec((n_experts, tile), lambda s: (0, s)),
                   pl.BlockSpec((1, n_experts, LANES), lambda s: (s, 0, 0))],
        out_shape=[jax.ShapeDtypeStruct((n_experts, n_rows), jnp.int32),
                   jax.ShapeDtypeStruct((n_tiles, n_experts, LANES), jnp.int32)],
        compiler_params=_cparams("parallel"),
        name="moe_plan",
    )(gates_t)
    n_blocks = (counts[:, :, 0] + MOE_ROW_BLOCK - 1) // MOE_ROW_BLOCK
    return rank, n_blocks.reshape(-1)


def _moe_kernel(nblk_ref, h_ref, rank_ref, gate_ref, wg_ref, wu_ref, wd_ref, out_ref, xg_sc, y_sc):
    s, e, f = pl.program_id(0), pl.program_id(1), pl.program_id(2)
    n_experts = pl.num_programs(1)
    tile = h_ref.shape[0]
    br = MOE_ROW_BLOCK
    n_blocks = nblk_ref[s * n_experts + e]

    @pl.when((e == 0) & (f == 0))
    def _():
        out_ref[...] = jnp.zeros(out_ref.shape, F32)

    def block_rows(r):
        return pl.ds(pl.multiple_of(r * br, br), br)

    def slot_matches(r):
        slot = lax.broadcasted_iota(jnp.int32, (br, tile), 0) + r * br
        return slot == rank_ref[pl.ds(e, 1), :]

    @pl.when(f == 0)
    def _():
        def gather(r, carry):
            onehot = jnp.where(slot_matches(r), 1.0, 0.0).astype(BF16)
            xg_sc[block_rows(r), :] = jnp.dot(onehot, h_ref[...], preferred_element_type=F32).astype(BF16)
            y_sc[block_rows(r), :] = jnp.zeros((br, y_sc.shape[1]), F32)
            return carry

        lax.fori_loop(0, n_blocks, gather, 0)

    def expert(r, carry):
        xb = xg_sc[block_rows(r), :]
        gt = jnp.dot(xb, wg_ref[0], preferred_element_type=F32)
        up = jnp.dot(xb, wu_ref[0], preferred_element_type=F32)
        act = (gt * jax.nn.sigmoid(gt) * up).astype(BF16)
        y_sc[block_rows(r), :] += jnp.dot(act, wd_ref[0], preferred_element_type=F32)
        return carry

    lax.fori_loop(0, n_blocks, expert, 0)

    @pl.when(f == pl.num_programs(2) - 1)
    def _():
        def scatter(r, carry):
            weighted = jnp.where(slot_matches(r), gate_ref[pl.ds(e, 1), :], 0.0).astype(BF16)
            out_ref[...] += lax.dot_general(weighted, y_sc[block_rows(r), :].astype(BF16),
                                            (((0,), (0,)), ((), ())), preferred_element_type=F32)
            return carry

        lax.fori_loop(0, n_blocks, scatter, 0)


def _moe(h, gates_t, w_gu, w_down):
    n_rows, D = h.shape
    E, F = w_down.shape[0], w_down.shape[1]
    tile = _pick_tile((2176, 2048, 1024, 512, 256, 128), n_rows)
    tf = _pick_tile((512, 256, 128), F)
    nf = F // tf
    rank, n_blocks = _moe_plan(gates_t, tile)
    max_rows = -(-tile // MOE_ROW_BLOCK) * MOE_ROW_BLOCK
    grid_spec = pltpu.PrefetchScalarGridSpec(
        num_scalar_prefetch=1,
        grid=(n_rows // tile, E, nf),
        in_specs=[pl.BlockSpec((tile, D), lambda s, e, f, nb: (s, 0)),
                  pl.BlockSpec((E, tile), lambda s, e, f, nb: (0, s)),
                  pl.BlockSpec((E, tile), lambda s, e, f, nb: (0, s)),
                  pl.BlockSpec((1, D, tf), lambda s, e, f, nb: (e, 0, f)),
                  pl.BlockSpec((1, D, tf), lambda s, e, f, nb: (e, 0, nf + f)),
                  pl.BlockSpec((1, tf, D), lambda s, e, f, nb: (e, f, 0))],
        out_specs=pl.BlockSpec((tile, D), lambda s, e, f, nb: (s, 0)),
        scratch_shapes=[pltpu.VMEM((max_rows, D), BF16), pltpu.VMEM((max_rows, D), F32)],
    )
    return pl.pallas_call(
        _moe_kernel,
        grid_spec=grid_spec,
        out_shape=jax.ShapeDtypeStruct((n_rows, D), F32),
        compiler_params=_cparams("parallel", "arbitrary", "arbitrary"),
        name="moe_experts",
    )(n_blocks, h, rank, gates_t, w_gu, w_gu, w_down)


def _residual_ln_kernel(x_ref, f_ref, gate_ref, g_ref, b_ref, out_ref):
    out_ref[...] = _layer_norm(DEEPNORM_ALPHA * x_ref[...] + gate_ref[0] * f_ref[...], g_ref[...], b_ref[...])


def _residual_ln(rows, xa, f, mod, part, ln_g, ln_b, n_rows):
    D = rows.D
    tm = rows.tile((512, 256, 128))
    return pl.pallas_call(
        _residual_ln_kernel,
        grid=(n_rows // tm,),
        in_specs=[pl.BlockSpec((tm, D), lambda i: (i, 0)), pl.BlockSpec((tm, D), lambda i: (i, 0)),
                  rows.mod_spec(part, tm), _resident((1, D)), _resident((1, D))],
        out_specs=pl.BlockSpec((tm, D), lambda i: (i, 0)),
        out_shape=jax.ShapeDtypeStruct((n_rows, D), F32),
        compiler_params=_cparams("parallel"),
        name="residual_ln",
    )(xa, f, mod, ln_g.reshape(1, D), ln_b.reshape(1, D))


def _moe_ln(rows, xa, mod, router, w_gu, w_down, ln_g, ln_b, n_rows):
    h, gates_t = _router(rows, xa, mod, router, n_rows)
    return _residual_ln(rows, xa, _moe(h, gates_t, w_gu, w_down), mod, 5, ln_g, ln_b, n_rows)


def _gmlp_kernel(x_ref, sh_ref, sc_ref, gate_ref, win_ref, bin_ref, vg_ref, vb_ref, ws_ref, bs_ref, wout_ref,
                 bout_ref, g_ref, b_ref, out_ref, gated_sc):
    tm = x_ref.shape[0]
    inner = wout_ref.shape[0]
    gdim = inner // CMLP_GROUPS
    x = x_ref[...]
    h = (x * (1.0 + sc_ref[0]) + sh_ref[0]).astype(BF16)
    v = jax.nn.gelu(jnp.dot(h, win_ref[:, inner:], preferred_element_type=F32) + bin_ref[:, inner:])
    v = _layer_norm(v, vg_ref[...], vb_ref[...]).astype(BF16)
    for gi in range(CMLP_GROUPS):
        cs = slice(gi * gdim, (gi + 1) * gdim)
        u = jax.nn.gelu(jnp.dot(h, win_ref[:, cs], preferred_element_type=F32) + bin_ref[:, cs])
        w_s = ws_ref[gi]
        bias = bs_ref[:, gi:gi + 1]
        for c in range(tm // CMLP_CHUNK):
            rs = slice(c * CMLP_CHUNK, (c + 1) * CMLP_CHUNK)
            mixed = jnp.dot(w_s, v[rs, cs], preferred_element_type=F32) + bias
            gated_sc[rs, cs] = (u[rs] * mixed).astype(BF16)
    y = jnp.dot(gated_sc[...], wout_ref[...], preferred_element_type=F32) + bout_ref[...]
    out_ref[...] = _layer_norm(DEEPNORM_ALPHA * x + gate_ref[0] * y, g_ref[...], b_ref[...])


def _gmlp_ln(rows, xa, mod, w_in, b_in, vg, vb, w_s, b_s, w_out, b_out, ln_g, ln_b, n_rows):
    D = rows.D
    inner = w_out.shape[0]
    tm = rows.tile((256, 128))
    return pl.pallas_call(
        _gmlp_kernel,
        grid=(n_rows // tm,),
        in_specs=[pl.BlockSpec((tm, D), lambda i: (i, 0)), rows.mod_spec(0, tm), rows.mod_spec(1, tm),
                  rows.mod_spec(2, tm), _resident(w_in.shape), _resident((1, 2 * inner)), _resident((1, inner)),
                  _resident((1, inner)), _resident(w_s.shape), _resident((CMLP_CHUNK, CMLP_GROUPS)),
                  _resident(w_out.shape), _resident((1, D)), _resident((1, D)), _resident((1, D))],
        out_specs=pl.BlockSpec((tm, D), lambda i: (i, 0)),
        out_shape=jax.ShapeDtypeStruct((n_rows, D), F32),
        scratch_shapes=[pltpu.VMEM((tm, inner), BF16)],
        compiler_params=_cparams("parallel"),
        name="gmlp_ln",
    )(xa, mod, mod, mod, w_in, b_in.reshape(1, -1), vg.reshape(1, -1), vb.reshape(1, -1), w_s, b_s.T, w_out,
      b_out.reshape(1, D), ln_g.reshape(1, D), ln_b.reshape(1, D))


def _ret_proj_kernel(x_ref, sh_ref, sc_ref, w_ref, cos_ref, sin_ref, q_ref, k_ref, v_ref, sg_ref, *, nk, nv, dk):
    h = (x_ref[...] * (1.0 + sc_ref[0]) + sh_ref[0]).astype(BF16)
    cos, sin = cos_ref[...], sin_ref[...]
    k_scale = dk ** -0.5

    def rope_store(dst_ref, col0, scale):
        full = jnp.dot(h, w_ref[:, col0:col0 + nk], preferred_element_type=F32) * scale
        for j in range(nk // LANES):
            t = full[:, j * LANES:(j + 1) * LANES]
            ts = slice((j % (dk // LANES)) * LANES, (j % (dk // LANES) + 1) * LANES)
            dst_ref[:, j * LANES:(j + 1) * LANES] = (t * cos[:, ts] + _swap_pairs(t, dk // 4) * sin[:, ts]).astype(BF16)

    rope_store(q_ref, 0, 1.0)
    rope_store(k_ref, nk, k_scale)
    blk = 512
    for j in range(nv // blk):
        v_ref[:, j * blk:(j + 1) * blk] = jnp.dot(
            h, w_ref[:, 2 * nk + j * blk:2 * nk + (j + 1) * blk], preferred_element_type=F32).astype(BF16)
        gt = jnp.dot(h, w_ref[:, 2 * nk + nv + j * blk:2 * nk + nv + (j + 1) * blk], preferred_element_type=F32)
        sg_ref[:, j * blk:(j + 1) * blk] = (gt * jax.nn.sigmoid(gt)).astype(BF16)


def _ret_project(rows, xa, mod, w, cos, sin):
    D = rows.D
    nk = D
    nv = (w.shape[1] - 2 * nk) // 2
    dk = nk // RET_HEADS
    tm = rows.tile((512, 256))
    row_spec = lambda width: pl.BlockSpec((tm, width), lambda i: (i, 0))
    return pl.pallas_call(
        functools.partial(_ret_proj_kernel, nk=nk, nv=nv, dk=dk),
        grid=(rows.n // tm,),
        in_specs=[row_spec(D), rows.mod_spec(0, tm), rows.mod_spec(1, tm), _resident(w.shape),
                  rows.pos_spec(tm, dk), rows.pos_spec(tm, dk)],
        out_specs=[row_spec(nk), row_spec(nk), row_spec(nv), row_spec(nv)],
        out_shape=[jax.ShapeDtypeStruct((rows.n, nk), BF16), jax.ShapeDtypeStruct((rows.n, nk), BF16),
                   jax.ShapeDtypeStruct((rows.n, nv), BF16), jax.ShapeDtypeStruct((rows.n, nv), BF16)],
        compiler_params=_cparams("parallel"),
        name="ret_proj",
    )(xa, mod, mod, w, cos, sin)


def _ret_scan_kernel(lg_ref, q_ref, k_ref, v_ref, *rest, reverse):
    if reverse:
        of_ref, sg_ref, o_ref, state_sc = rest
    else:
        o_ref, state_sc = rest
    ch = q_ref.shape[0]
    head = pl.program_id(1)
    lg = lg_ref[head]

    @pl.when(pl.program_id(2) == 0)
    def _():
        state_sc[...] = jnp.zeros(state_sc.shape, F32)

    q, k, v = q_ref[...], k_ref[...], v_ref[...]
    ii = lax.broadcasted_iota(jnp.int32, (ch, ch), 0)
    jj = lax.broadcasted_iota(jnp.int32, (ch, ch), 1)
    dist = (jj - ii) if reverse else (ii - jj)
    inner_decay = jnp.where(dist >= 0, jnp.exp(jnp.maximum(dist, 0).astype(F32) * lg), 0.0)
    pos = lax.broadcasted_iota(jnp.int32, (ch, 1), 0)
    order = (ch - 1 - pos) if reverse else pos
    cross_decay = jnp.exp((order + 1).astype(F32) * lg)
    state_weight = jnp.exp((ch - 1 - order).astype(F32) * lg)

    scores = lax.dot_general(q, k, (((1,), (1,)), ((), ())), preferred_element_type=F32) * inner_decay
    state = state_sc[...]
    out = (jnp.dot(scores.astype(BF16), v, preferred_element_type=F32)
           + jnp.dot(q, state.astype(BF16), preferred_element_type=F32) * cross_decay)
    kw = (k.astype(F32) * state_weight).astype(BF16)
    chunk_decay = jnp.exp(jnp.full((1, 1), ch, F32) * lg)
    state_sc[...] = state * chunk_decay + lax.dot_general(
        kw, v, (((0,), (0,)), ((), ())), preferred_element_type=F32)

    if reverse:
        o = out + of_ref[...]
        o = o * lax.rsqrt(jnp.mean(o * o, axis=-1, keepdims=True) + RMS_EPS)
        o_ref[...] = (sg_ref[...].astype(F32) * o).astype(BF16)
    else:
        o_ref[...] = out


def _ret_scan(rows, log_gamma, q, k, v, o_fwd=None, sgate=None, *, reverse):
    B, S, C = rows.B, rows.S, rows.C
    ch = RET_CHUNK
    dk = q.shape[1] // RET_HEADS
    dv = v.shape[1] // RET_HEADS
    n_c, n_l = C // ch, S // ch

    def row_block(b, t):
        if reverse:
            return jnp.where(t < n_c, rows.n_lat // ch + b * n_c + (n_c - 1 - t), b * n_l + (n_l - 1 - (t - n_c)))
        return jnp.where(t < n_c, rows.n_lat // ch + b * n_c + t, b * n_l + (t - n_c))

    kspec = pl.BlockSpec((ch, dk), lambda b, h, t: (row_block(b, t), h))
    vspec = pl.BlockSpec((ch, dv), lambda b, h, t: (row_block(b, t), h))
    in_specs = [pl.BlockSpec(memory_space=pltpu.SMEM), kspec, kspec, vspec]
    args = [log_gamma, q, k, v]
    if reverse:
        in_specs += [vspec, vspec]
        args += [o_fwd, sgate]
    return pl.pallas_call(
        functools.partial(_ret_scan_kernel, reverse=reverse),
        grid=(B, RET_HEADS, n_c + n_l),
        in_specs=in_specs,
        out_specs=vspec,
        out_shape=jax.ShapeDtypeStruct((rows.n, v.shape[1]), BF16 if reverse else F32),
        scratch_shapes=[pltpu.VMEM((dk, dv), F32)],
        compiler_params=_cparams("parallel", "parallel", "arbitrary"),
        name="ret_scan_bwd" if reverse else "ret_scan_fwd",
    )(*args)


def kernel(x, c, ctx, c_ctx, l0_ada_w, l0_ada_b, l0_ln1_g, l0_ln1_b, l0_ln2_g, l0_ln2_b, l0_attn_wqkv, l0_attn_q_norm, l0_attn_k_norm, l0_attn_wo, l0_ffn_w_gu, l0_ffn_w_down, l1_ada_w, l1_ada_b, l1_ln1_g, l1_ln1_b, l1_ln2_g, l1_ln2_b, l1_cmlp_w_in, l1_cmlp_b_in, l1_cmlp_v_norm_g, l1_cmlp_v_norm_b, l1_cmlp_w_s, l1_cmlp_b_s, l1_cmlp_w_out, l1_cmlp_b_out, l1_moe_router, l1_moe_w_gu, l1_moe_w_down, l2_ada_w, l2_ada_b, l2_ln1_g, l2_ln1_b, l2_ln2_g, l2_ln2_b, l2_ret_wqkvg, l2_ret_decay, l2_ret_wo, l2_ffn_w_gu, l2_ffn_w_down, l3_ada_w, l3_ada_b, l3_ln1_g, l3_ln1_b, l3_ln2_g, l3_ln2_b, l3_attn_wqkv, l3_attn_q_norm, l3_attn_k_norm, l3_attn_wo, l3_moe_router, l3_moe_w_gu, l3_moe_w_down):
    B, S, D = x.shape
    C = ctx.shape[1]
    rows = _Rows(B, S, C, D)
    assert S % GRID_W == 0 and C % RET_CHUNK == 0 and S % RET_CHUNK == 0

    xa = jnp.concatenate([x.reshape(rows.n_lat, D), ctx.reshape(rows.n_ctx, D)], axis=0)
    n_cond = -(-(B + 1) // 8) * 8
    cond = jnp.concatenate([c, c_ctx[None, :], jnp.zeros((n_cond - B - 1, D), F32)], axis=0)

    def modulation(ada_w, ada_b):
        return _ada(cond, ada_w, ada_b).reshape(n_cond * 6, 1, D)

    pad = rows.tile((512, 256))
    attn_cos, attn_sin = _rope_tables(S, ATTN_HEAD_DIM, pad)
    ret_cos, ret_sin = _rope_tables(S, D // RET_HEADS, pad)
    zero_bias = jnp.zeros((D,), F32)
    bf = lambda w: w.astype(BF16)

    def attention_layer(xa, mod, wqkv, qg, kg, wo, ln_g, ln_b, with_ctx):
        n_rows = rows.n if with_ctx else rows.n_lat
        q, k, v = _attn_project(rows, xa, mod, bf(wqkv), qg, kg, attn_cos, attn_sin)
        o = _attention(rows, q, k, v, with_ctx)
        return _proj_ln(rows, o, bf(wo), zero_bias, xa, mod, ln_g, ln_b, n_rows)

    def retention_layer(xa, mod, wqkvg, decay, wo, ln_g, ln_b):
        log_gamma = -jnp.exp(decay.astype(F32))
        q, k, v, sg = _ret_project(rows, xa, mod, bf(wqkvg), ret_cos, ret_sin)
        o_f = _ret_scan(rows, log_gamma[0], q, k, v, reverse=False)
        o = _ret_scan(rows, log_gamma[1], q, k, v, o_f, sg, reverse=True)
        return _proj_ln(rows, o, bf(wo), zero_bias, xa, mod, ln_g, ln_b, rows.n)

    mod = modulation(l0_ada_w, l0_ada_b)
    xa = attention_layer(xa, mod, l0_attn_wqkv, l0_attn_q_norm, l0_attn_k_norm, l0_attn_wo, l0_ln1_g, l0_ln1_b, True)
    xa = _ffn_ln(rows, xa, mod, bf(l0_ffn_w_gu), bf(l0_ffn_w_down), l0_ln2_g, l0_ln2_b, rows.n)

    mod = modulation(l1_ada_w, l1_ada_b)
    xa = _gmlp_ln(rows, xa, mod, bf(l1_cmlp_w_in), l1_cmlp_b_in, l1_cmlp_v_norm_g, l1_cmlp_v_norm_b,
                  bf(l1_cmlp_w_s), l1_cmlp_b_s, bf(l1_cmlp_w_out), l1_cmlp_b_out, l1_ln1_g, l1_ln1_b, rows.n)
    xa = _moe_ln(rows, xa, mod, l1_moe_router, bf(l1_moe_w_gu), bf(l1_moe_w_down), l1_ln2_g, l1_ln2_b, rows.n)

    mod = modulation(l2_ada_w, l2_ada_b)
    xa = retention_layer(xa, mod, l2_ret_wqkvg, l2_ret_decay, l2_ret_wo, l2_ln1_g, l2_ln1_b)
    xa = _ffn_ln(rows, xa, mod, bf(l2_ffn_w_gu), bf(l2_ffn_w_down), l2_ln2_g, l2_ln2_b, rows.n)

    mod = modulation(l3_ada_w, l3_ada_b)
    xl = attention_layer(xa, mod, l3_attn_wqkv, l3_attn_q_norm, l3_attn_k_norm, l3_attn_wo, l3_ln1_g, l3_ln1_b, False)
    xl = _moe_ln(rows, xl, mod, l3_moe_router, bf(l3_moe_w_gu), bf(l3_moe_w_down), l3_ln2_g, l3_ln2_b, rows.n_lat)
    return xl.reshape(B, S, D)
```

```python
import functools

import jax
import jax.numpy as jnp
from jax import lax
from jax.experimental import pallas as pl
from jax.experimental.pallas import tpu as pltpu

F32 = jnp.float32
BF16 = jnp.bfloat16

DEPTH = 4
GRID_W = 64
ROPE_THETA = 10000.0
DEEPNORM_ALPHA = (2 * DEPTH) ** 0.25
LN_EPS = 1e-5
RMS_EPS = 1e-6
ATTN_HEAD_DIM = 128
ATTN_KV_HEADS = 2
CMLP_CHUNK = 128
CMLP_GROUPS = 8
RET_HEADS = 4
RET_CHUNK = 128
MOE_EXPERTS = 8

LANES = 128
VMEM_LIMIT_BYTES = 56 * 1024 * 1024
NEG_BIG = -1e30


def _cparams(*sem):
    return pltpu.CompilerParams(dimension_semantics=sem, vmem_limit_bytes=VMEM_LIMIT_BYTES)


def _resident(shape):
    nd = len(shape)
    return pl.BlockSpec(shape, lambda *_: (0,) * nd)


def _layer_norm(z, g, b):
    mu = jnp.mean(z, axis=-1, keepdims=True)
    zc = z - mu
    var = jnp.mean(zc * zc, axis=-1, keepdims=True)
    return zc * lax.rsqrt(var + LN_EPS) * g + b


def _pick_tile(candidates, *extents):
    for t in candidates:
        if all(e % t == 0 for e in extents):
            return t
    raise ValueError(f"no tile in {candidates} divides {extents}")


class _Rows:
    def __init__(self, batch, seq, ctx_len, dim):
        self.B, self.S, self.C, self.D = batch, seq, ctx_len, dim
        self.n_lat = batch * seq
        self.n_ctx = batch * ctx_len
        self.n = self.n_lat + self.n_ctx

    def tile(self, candidates):
        return _pick_tile(candidates, self.S, self.n_ctx)

    def mod_spec(self, part, tm):
        S, B = self.S, self.B
        return pl.BlockSpec((1, 1, self.D), lambda i, *_: (jnp.minimum(i * tm // S, B) * 6 + part, 0, 0))

    def pos_spec(self, tm, width):
        n_lat_tiles, per_seq = self.n_lat // tm, self.S // tm
        return pl.BlockSpec((tm, width), lambda i, *_: (jnp.where(i < n_lat_tiles, i % per_seq, per_seq), 0))


def _ada_kernel(c_ref, w_ref, b_ref, o_ref):
    c = c_ref[...]
    s = c * jax.nn.sigmoid(c)
    o_ref[...] = jnp.dot(s, w_ref[...], preferred_element_type=F32, precision=lax.Precision.HIGHEST) + b_ref[...]


def _ada(cond, w, b):
    r, d = cond.shape
    n = w.shape[1]
    tn = _pick_tile((1536, 1024, 512, 128), n)
    return pl.pallas_call(
        _ada_kernel,
        grid=(n // tn,),
        in_specs=[_resident((r, d)), pl.BlockSpec((d, tn), lambda j: (0, j)), pl.BlockSpec((1, tn), lambda j: (0, j))],
        out_specs=pl.BlockSpec((r, tn), lambda j: (0, j)),
        out_shape=jax.ShapeDtypeStruct((r, n), F32),
        compiler_params=_cparams("parallel"),
        name="ada",
    )(cond, w, b.reshape(1, n))


def _rope_tables(seq, head_dim, pad_rows):
    nf = head_dim // 4
    inv_freq = ROPE_THETA ** (-jnp.arange(nf, dtype=F32) / nf)
    t = jnp.arange(seq, dtype=jnp.int32)
    row = (t // GRID_W).astype(F32)[:, None] * inv_freq
    col = (t % GRID_W).astype(F32)[:, None] * inv_freq
    cos = jnp.concatenate([jnp.cos(row), jnp.cos(row), jnp.cos(col), jnp.cos(col)], axis=-1)
    sin = jnp.concatenate([-jnp.sin(row), jnp.sin(row), -jnp.sin(col), jnp.sin(col)], axis=-1)
    cos = jnp.concatenate([cos, jnp.ones((pad_rows, head_dim), F32)], axis=0)
    sin = jnp.concatenate([sin, jnp.zeros((pad_rows, head_dim), F32)], axis=0)
    return cos, sin


def _swap_pairs(y, half):
    if 2 * half == LANES:
        return pltpu.roll(y, half, 1)
    lane = lax.broadcasted_iota(jnp.int32, y.shape, 1)
    return jnp.where(lane % (2 * half) < half, pltpu.roll(y, LANES - half, 1), pltpu.roll(y, half, 1))


KEY_BLOCK = 256
LOG2_E = 1.4426950408889634


def _qkv_kernel(x_ref, sh_ref, sc_ref, w_ref, qg_ref, kg_ref, cos_ref, sin_ref, q_ref, kt_ref, v_ref, *, nq, nkv):
    dh = ATTN_HEAD_DIM
    h = (x_ref[...] * (1.0 + sc_ref[0]) + sh_ref[0]).astype(BF16)
    cos, sin = cos_ref[...], sin_ref[...]
    scale = dh ** -0.5 * LOG2_E

    def normed_rope(t, g):
        y = t * lax.rsqrt(jnp.mean(t * t, axis=-1, keepdims=True) + RMS_EPS) * g
        return y * cos + _swap_pairs(y, dh // 4) * sin

    t = jnp.dot(h, w_ref[...], preferred_element_type=F32)
    for hd in range(nq // dh):
        q_ref[:, hd * dh:(hd + 1) * dh] = (normed_rope(t[:, hd * dh:(hd + 1) * dh], qg_ref[...]) * scale).astype(BF16)
    k = jnp.concatenate([normed_rope(t[:, nq + hd * dh:nq + (hd + 1) * dh], kg_ref[...]) for hd in range(nkv // dh)],
                        axis=1)
    for j in range(kt_ref.shape[0]):
        kt_ref[j] = k[j * KEY_BLOCK:(j + 1) * KEY_BLOCK, :].T.astype(BF16)
    v_ref[...] = t[:, nq + nkv:].astype(BF16)


def _attn_project(rows, xa, mod, w, qg, kg, cos, sin):
    D = rows.D
    nkv = ATTN_KV_HEADS * ATTN_HEAD_DIM
    nq = w.shape[1] - 2 * nkv
    tm = rows.tile((512, 256))
    row_spec = lambda width: pl.BlockSpec((tm, width), lambda i: (i, 0))
    return pl.pallas_call(
        functools.partial(_qkv_kernel, nq=nq, nkv=nkv),
        grid=(rows.n // tm,),
        in_specs=[row_spec(D), rows.mod_spec(0, tm), rows.mod_spec(1, tm), _resident(w.shape),
                  _resident((1, ATTN_HEAD_DIM)), _resident((1, ATTN_HEAD_DIM)),
                  rows.pos_spec(tm, ATTN_HEAD_DIM), rows.pos_spec(tm, ATTN_HEAD_DIM)],
        out_specs=[row_spec(nq), pl.BlockSpec((tm // KEY_BLOCK, nkv, KEY_BLOCK), lambda i: (i, 0, 0)), row_spec(nkv)],
        out_shape=[jax.ShapeDtypeStruct((rows.n, nq), BF16),
                   jax.ShapeDtypeStruct((rows.n // KEY_BLOCK, nkv, KEY_BLOCK), BF16),
                   jax.ShapeDtypeStruct((rows.n, nkv), BF16)],
        compiler_params=_cparams("parallel"),
        name="attn_qkv",
    )(xa, mod, mod, w, qg.reshape(1, -1), kg.reshape(1, -1), cos, sin)


def _attn_kernel(q_ref, kc_ref, vc_ref, kl_ref, vl_ref, o_ref, m_sc, acc_sc, *, tq, blocks_per_chunk, n_lat_tiles):
    dh = ATTN_HEAD_DIM
    group = q_ref.shape[1] // dh // ATTN_KV_HEADS
    is_latent = pl.program_id(1) < n_lat_tiles
    n_lat_chunks = jnp.where(is_latent, kl_ref.shape[0] // blocks_per_chunk, 0)
    chunk = blocks_per_chunk * KEY_BLOCK

    for g in range(ATTN_KV_HEADS):
        gs = slice(g * dh, (g + 1) * dh)
        qg = jnp.concatenate([q_ref[:, (g * group + j) * dh:(g * group + j + 1) * dh] for j in range(group)], axis=0)
        m_sc[...] = jnp.full(m_sc.shape, NEG_BIG, F32)
        acc_sc[...] = jnp.zeros(acc_sc.shape, F32)

        def online_softmax_step(kt, v):
            width = kt.shape[1]
            s = jnp.dot(qg, kt, preferred_element_type=F32)
            m_old = m_sc[...]
            m_new = jnp.maximum(m_old, jnp.max(s, axis=-1, keepdims=True))
            p = jnp.exp2(s - jnp.concatenate([m_new] * (width // LANES), axis=1)).astype(BF16)
            alpha = jnp.exp2(m_old - m_new)
            v_ones = jnp.concatenate([v, jnp.ones((width, dh), BF16)], axis=1)
            acc_sc[...] = (jnp.concatenate([alpha, alpha], axis=1) * acc_sc[...]
                           + jnp.dot(p, v_ones, preferred_element_type=F32))
            m_sc[...] = m_new

        for c in range(kc_ref.shape[0]):
            online_softmax_step(kc_ref[c, gs, :], vc_ref[c * KEY_BLOCK:(c + 1) * KEY_BLOCK, gs])

        def latent_chunk(c, carry):
            kt = jnp.concatenate([kl_ref[c * blocks_per_chunk + j, gs, :] for j in range(blocks_per_chunk)], axis=1)
            r0 = pl.multiple_of(c * chunk, chunk)
            online_softmax_step(kt, vl_ref[pl.ds(r0, chunk), gs])
            return carry

        lax.fori_loop(0, n_lat_chunks, latent_chunk, 0)
        acc = acc_sc[...]
        o = acc[:, :dh] / acc[:, dh:]
        for j in range(group):
            o_ref[:, (g * group + j) * dh:(g * group + j + 1) * dh] = o[j * tq:(j + 1) * tq].astype(BF16)


def _attention(rows, q, kt, v, with_ctx):
    B, S, C = rows.B, rows.S, rows.C
    nq = q.shape[1]
    nkv = v.shape[1]
    assert S % KEY_BLOCK == 0 and C % KEY_BLOCK == 0 and rows.n_lat % C == 0
    tq = _pick_tile((256, 128), S, C)
    blocks_per_chunk = 2 if S % (2 * KEY_BLOCK) == 0 else 1
    lat_tiles, ctx_tiles = S // tq, C // tq
    tiles = lat_tiles + (ctx_tiles if with_ctx else 0)
    group = nq // nkv

    def q_map(b, i):
        return (jnp.where(i < lat_tiles, b * lat_tiles + i, rows.n_lat // tq + b * ctx_tiles + (i - lat_tiles)), 0)

    kt_ctx = pl.BlockSpec((C // KEY_BLOCK, nkv, KEY_BLOCK), lambda b, i: (rows.n_lat // C + b, 0, 0))
    kt_lat = pl.BlockSpec((S // KEY_BLOCK, nkv, KEY_BLOCK), lambda b, i: (b, 0, 0))
    v_ctx = pl.BlockSpec((C, nkv), lambda b, i: (rows.n_lat // C + b, 0))
    v_lat = pl.BlockSpec((S, nkv), lambda b, i: (b, 0))
    return pl.pallas_call(
        functools.partial(_attn_kernel, tq=tq, blocks_per_chunk=blocks_per_chunk, n_lat_tiles=lat_tiles),
        grid=(B, tiles),
        in_specs=[pl.BlockSpec((tq, nq), q_map), kt_ctx, v_ctx, kt_lat, v_lat],
        out_specs=pl.BlockSpec((tq, nq), q_map),
        out_shape=jax.ShapeDtypeStruct((rows.n if with_ctx else rows.n_lat, nq), BF16),
        scratch_shapes=[pltpu.VMEM((group * tq, LANES), F32), pltpu.VMEM((group * tq, 2 * ATTN_HEAD_DIM), F32)],
        compiler_params=_cparams("parallel", "parallel"),
        name="attn_core",
    )(q, kt, v, kt, v)


def _proj_ln_kernel(o_ref, w_ref, bias_ref, x_ref, gate_ref, g_ref, b_ref, out_ref):
    y = jnp.dot(o_ref[...], w_ref[...], preferred_element_type=F32) + bias_ref[...]
    out_ref[...] = _layer_norm(DEEPNORM_ALPHA * x_ref[...] + gate_ref[0] * y, g_ref[...], b_ref[...])


def _proj_ln(rows, o, w, bias, xa, mod, ln_g, ln_b, n_rows):
    D = rows.D
    kdim = o.shape[1]
    tm = rows.tile((512, 256, 128))
    return pl.pallas_call(
        _proj_ln_kernel,
        grid=(n_rows // tm,),
        in_specs=[pl.BlockSpec((tm, kdim), lambda i: (i, 0)), _resident(w.shape), _resident((1, D)),
                  pl.BlockSpec((tm, D), lambda i: (i, 0)), rows.mod_spec(2, tm), _resident((1, D)), _resident((1, D))],
        out_specs=pl.BlockSpec((tm, D), lambda i: (i, 0)),
        out_shape=jax.ShapeDtypeStruct((n_rows, D), F32),
        compiler_params=_cparams("parallel"),
        name="proj_ln",
    )(o, w, bias.reshape(1, D), xa, mod, ln_g.reshape(1, D), ln_b.reshape(1, D))


def _ffn_kernel(x_ref, sh_ref, sc_ref, gate_ref, wg_ref, wu_ref, wd_ref, g_ref, b_ref, out_ref, h_sc, acc_sc):
    f = pl.program_id(1)

    @pl.when(f == 0)
    def _():
        h_sc[...] = (x_ref[...] * (1.0 + sc_ref[0]) + sh_ref[0]).astype(BF16)
        acc_sc[...] = jnp.zeros(acc_sc.shape, F32)

    h = h_sc[...]
    gt = jnp.dot(h, wg_ref[...], preferred_element_type=F32)
    up = jnp.dot(h, wu_ref[...], preferred_element_type=F32)
    act = (gt * jax.nn.sigmoid(gt) * up).astype(BF16)
    acc_sc[...] += jnp.dot(act, wd_ref[...], preferred_element_type=F32)

    @pl.when(f == pl.num_programs(1) - 1)
    def _():
        out_ref[...] = _layer_norm(DEEPNORM_ALPHA * x_ref[...] + gate_ref[0] * acc_sc[...], g_ref[...], b_ref[...])


def _ffn_ln(rows, xa, mod, w_gu, w_down, ln_g, ln_b, n_rows):
    D = rows.D
    F = w_down.shape[0]
    tm = rows.tile((1024, 512, 256, 128))
    tf = _pick_tile((512, 256, 128), F)
    nf = F // tf
    return pl.pallas_call(
        _ffn_kernel,
        grid=(n_rows // tm, nf),
        in_specs=[pl.BlockSpec((tm, D), lambda i, f: (i, 0)),
                  rows.mod_spec(3, tm), rows.mod_spec(4, tm), rows.mod_spec(5, tm),
                  pl.BlockSpec((D, tf), lambda i, f: (0, f)), pl.BlockSpec((D, tf), lambda i, f: (0, nf + f)),
                  pl.BlockSpec((tf, D), lambda i, f: (f, 0)), _resident((1, D)), _resident((1, D))],
        out_specs=pl.BlockSpec((tm, D), lambda i, f: (i, 0)),
        out_shape=jax.ShapeDtypeStruct((n_rows, D), F32),
        scratch_shapes=[pltpu.VMEM((tm, D), BF16), pltpu.VMEM((tm, D), F32)],
        compiler_params=_cparams("parallel", "arbitrary"),
        name="ffn_ln",
    )(xa, mod, mod, mod, w_gu, w_gu, w_down, ln_g.reshape(1, D), ln_b.reshape(1, D))


MOE_ROW_BLOCK = 256
MOE_TAIL_BLOCK = 128


def _router_kernel(x_ref, sh_ref, sc_ref, r_ref, h_ref, gates_ref, *, n_experts):
    h = x_ref[...] * (1.0 + sc_ref[0]) + sh_ref[0]
    h_ref[...] = h.astype(BF16)
    logits = jnp.dot(h, r_ref[...], preferred_element_type=F32, precision=lax.Precision.HIGHEST)
    lane = lax.broadcasted_iota(jnp.int32, logits.shape, 1)
    lowest = jnp.finfo(F32).min
    lg = jnp.where(lane < n_experts, logits, lowest)
    m1 = jnp.max(lg, axis=-1, keepdims=True)
    i1 = jnp.min(jnp.where(lg == m1, lane, LANES), axis=-1, keepdims=True)
    lg2 = jnp.where(lane == i1, lowest, lg)
    m2 = jnp.max(lg2, axis=-1, keepdims=True)
    i2 = jnp.min(jnp.where(lg2 == m2, lane, LANES), axis=-1, keepdims=True)
    e2 = jnp.exp(m2 - m1)
    denom = 1.0 + e2
    gates = jnp.where(lane == i1, 1.0 / denom, 0.0) + jnp.where(lane == i2, e2 / denom, 0.0)
    gates_ref[...] = gates.T[:n_experts, :]


def _router(rows, xa, mod, router, n_rows):
    D = rows.D
    n_experts = router.shape[1]
    tm = rows.tile((512, 256, 128))
    router_pad = jnp.pad(router, ((0, 0), (0, LANES - n_experts)))
    return pl.pallas_call(
        functools.partial(_router_kernel, n_experts=n_experts),
        grid=(n_rows // tm,),
        in_specs=[pl.BlockSpec((tm, D), lambda i: (i, 0)), rows.mod_spec(3, tm), rows.mod_spec(4, tm),
                  _resident((D, LANES))],
        out_specs=[pl.BlockSpec((tm, D), lambda i: (i, 0)), pl.BlockSpec((n_experts, tm), lambda i: (0, i))],
        out_shape=[jax.ShapeDtypeStruct((n_rows, D), BF16), jax.ShapeDtypeStruct((n_experts, n_rows), F32)],
        compiler_params=_cparams("parallel"),
        name="moe_router",
    )(xa, mod, mod, router_pad)


def _plan_kernel(g_ref, rank_ref, cnt_ref):
    n_experts, tile = g_ref.shape
    routed = g_ref[...] > 0.0
    ones = jnp.where(routed, 1.0, 0.0)
    before = jnp.where(lax.broadcasted_iota(jnp.int32, (LANES, LANES), 0)
                       < lax.broadcasted_iota(jnp.int32, (LANES, LANES), 1), 1.0, 0.0)
    seen = jnp.zeros((n_experts, 1), F32)
    for c in range(tile // LANES):
        cs = slice(c * LANES, (c + 1) * LANES)
        rank = jnp.dot(ones[:, cs], before, preferred_element_type=F32) + seen
        rank_ref[:, cs] = jnp.where(routed[:, cs], rank, -1.0).astype(jnp.int32)
        seen = seen + jnp.sum(ones[:, cs], axis=1, keepdims=True)
    cnt_ref[0] = jnp.broadcast_to(seen, (n_experts, LANES)).astype(jnp.int32)


def _moe_plan(gates_t, tile):
    n_experts, n_rows = gates_t.shape
    n_tiles = n_rows // tile
    rank, counts = pl.pallas_call(
        _plan_kernel,
        grid=(n_tiles,),
        in_specs=[pl.BlockSpec((n_experts, tile), lambda s: (0, s))],
        out_specs=[pl.BlockSpec((n_experts, tile), lambda s: (0, s)),
                   pl.BlockSpec((1, n_experts, LANES), lambda s: (s, 0, 0))],
        out_shape=[jax.ShapeDtypeStruct((n_experts, n_rows), jnp.int32),
                   jax.ShapeDtypeStruct((n_tiles, n_experts, LANES), jnp.int32)],
        compiler_params=_cparams("parallel"),
        name="moe_plan",
    )(gates_t)
    count = counts[:, :, 0].reshape(-1)
    remainder = count % MOE_ROW_BLOCK
    has_tail = ((remainder > 0) & (remainder <= MOE_TAIL_BLOCK)).astype(jnp.int32)
    n_full = count // MOE_ROW_BLOCK + (remainder > MOE_TAIL_BLOCK).astype(jnp.int32)
    return rank, n_full, has_tail


def _moe_kernel(nfull_ref, tail_ref, h_ref, rank_ref, gate_ref, wg_ref, wu_ref, wd_ref, out_ref, xg_sc, y_sc):
    s, e, f = pl.program_id(0), pl.program_id(1), pl.program_id(2)
    tile = h_ref.shape[0]
    pair = s * pl.num_programs(1) + e
    n_full = nfull_ref[pair]
    has_tail = tail_ref[pair] > 0

    @pl.when((e == 0) & (f == 0))
    def _():
        out_ref[...] = jnp.zeros(out_ref.shape, F32)

    def for_each_block(fn):
        def body(r, carry):
            fn(r * MOE_ROW_BLOCK, MOE_ROW_BLOCK)
            return carry

        lax.fori_loop(0, n_full, body, 0)

        @pl.when(has_tail)
        def _():
            fn(n_full * MOE_ROW_BLOCK, MOE_TAIL_BLOCK)

    def rows_at(start, size):
        return pl.ds(pl.multiple_of(start, MOE_TAIL_BLOCK), size)

    def slot_matches(start, size):
        slot = lax.broadcasted_iota(jnp.int32, (size, tile), 0) + start
        return slot == rank_ref[pl.ds(e, 1), :]

    def gather(start, size):
        onehot = jnp.where(slot_matches(start, size), 1.0, 0.0).astype(BF16)
        xg_sc[rows_at(start, size), :] = jnp.dot(onehot, h_ref[...], preferred_element_type=F32).astype(BF16)
        y_sc[rows_at(start, size), :] = jnp.zeros((size, y_sc.shape[1]), F32)

    def expert(start, size):
        xb = xg_sc[rows_at(start, size), :]
        gt = jnp.dot(xb, wg_ref[0], preferred_element_type=F32)
        up = jnp.dot(xb, wu_ref[0], preferred_element_type=F32)
        act = (gt * jax.nn.sigmoid(gt) * up).astype(BF16)
        y_sc[rows_at(start, size), :] += jnp.dot(act, wd_ref[0], preferred_element_type=F32)

    def scatter(start, size):
        weighted = jnp.where(slot_matches(start, size), gate_ref[pl.ds(e, 1), :], 0.0).astype(BF16)
        out_ref[...] += lax.dot_general(weighted, y_sc[rows_at(start, size), :].astype(BF16),
                                        (((0,), (0,)), ((), ())), preferred_element_type=F32)

    @pl.when(f == 0)
    def _():
        for_each_block(gather)

    for_each_block(expert)

    @pl.when(f == pl.num_programs(2) - 1)
    def _():
        for_each_block(scatter)


def _moe(h, gates_t, w_gu, w_down):
    n_rows, D = h.shape
    E, F = w_down.shape[0], w_down.shape[1]
    tile = _pick_tile((2176, 2048, 1024, 512, 256, 128), n_rows)
    tf = _pick_tile((512, 256, 128), F)
    nf = F // tf
    rank, n_full, has_tail = _moe_plan(gates_t, tile)
    max_rows = -(-tile // MOE_ROW_BLOCK) * MOE_ROW_BLOCK
    grid_spec = pltpu.PrefetchScalarGridSpec(
        num_scalar_prefetch=2,
        grid=(n_rows // tile, E, nf),
        in_specs=[pl.BlockSpec((tile, D), lambda s, e, f, *_: (s, 0)),
                  pl.BlockSpec((E, tile), lambda s, e, f, *_: (0, s)),
                  pl.BlockSpec((E, tile), lambda s, e, f, *_: (0, s)),
                  pl.BlockSpec((1, D, tf), lambda s, e, f, *_: (e, 0, f)),
                  pl.BlockSpec((1, D, tf), lambda s, e, f, *_: (e, 0, nf + f)),
                  pl.BlockSpec((1, tf, D), lambda s, e, f, *_: (e, f, 0))],
        out_specs=pl.BlockSpec((tile, D), lambda s, e, f, *_: (s, 0)),
        scratch_shapes=[pltpu.VMEM((max_rows, D), BF16), pltpu.VMEM((max_rows, D), F32)],
    )
    return pl.pallas_call(
        _moe_kernel,
        grid_spec=grid_spec,
        out_shape=jax.ShapeDtypeStruct((n_rows, D), F32),
        compiler_params=_cparams("parallel", "arbitrary", "arbitrary"),
        name="moe_experts",
    )(n_full, has_tail, h, rank, gates_t, w_gu, w_gu, w_down)


def _residual_ln_kernel(x_ref, f_ref, gate_ref, g_ref, b_ref, out_ref):
    out_ref[...] = _layer_norm(DEEPNORM_ALPHA * x_ref[...] + gate_ref[0] * f_ref[...], g_ref[...], b_ref[...])


def _residual_ln(rows, xa, f, mod, part, ln_g, ln_b, n_rows):
    D = rows.D
    tm = rows.tile((512, 256, 128))
    return pl.pallas_call(
        _residual_ln_kernel,
        grid=(n_rows // tm,),
        in_specs=[pl.BlockSpec((tm, D), lambda i: (i, 0)), pl.BlockSpec((tm, D), lambda i: (i, 0)),
                  rows.mod_spec(part, tm), _resident((1, D)), _resident((1, D))],
        out_specs=pl.BlockSpec((tm, D), lambda i: (i, 0)),
        out_shape=jax.ShapeDtypeStruct((n_rows, D), F32),
        compiler_params=_cparams("parallel"),
        name="residual_ln",
    )(xa, f, mod, ln_g.reshape(1, D), ln_b.reshape(1, D))


def _moe_ln(rows, xa, mod, router, w_gu, w_down, ln_g, ln_b, n_rows):
    h, gates_t = _router(rows, xa, mod, router, n_rows)
    return _residual_ln(rows, xa, _moe(h, gates_t, w_gu, w_down), mod, 5, ln_g, ln_b, n_rows)


def _gmlp_kernel(x_ref, sh_ref, sc_ref, gate_ref, win_ref, bin_ref, vg_ref, vb_ref, ws_ref, bs_ref, wout_ref,
                 bout_ref, g_ref, b_ref, out_ref, gated_sc):
    tm = x_ref.shape[0]
    inner = wout_ref.shape[0]
    gdim = inner // CMLP_GROUPS
    x = x_ref[...]
    h = (x * (1.0 + sc_ref[0]) + sh_ref[0]).astype(BF16)
    v = jax.nn.gelu(jnp.dot(h, win_ref[:, inner:], preferred_element_type=F32) + bin_ref[:, inner:])
    v = _layer_norm(v, vg_ref[...], vb_ref[...]).astype(BF16)
    for gi in range(CMLP_GROUPS):
        cs = slice(gi * gdim, (gi + 1) * gdim)
        u = jax.nn.gelu(jnp.dot(h, win_ref[:, cs], preferred_element_type=F32) + bin_ref[:, cs])
        w_s = ws_ref[gi]
        bias = bs_ref[:, gi:gi + 1]
        for c in range(tm // CMLP_CHUNK):
            rs = slice(c * CMLP_CHUNK, (c + 1) * CMLP_CHUNK)
            mixed = jnp.dot(w_s, v[rs, cs], preferred_element_type=F32) + bias
            gated_sc[rs, cs] = (u[rs] * mixed).astype(BF16)
    y = jnp.dot(gated_sc[...], wout_ref[...], preferred_element_type=F32) + bout_ref[...]
    out_ref[...] = _layer_norm(DEEPNORM_ALPHA * x + gate_ref[0] * y, g_ref[...], b_ref[...])


def _gmlp_ln(rows, xa, mod, w_in, b_in, vg, vb, w_s, b_s, w_out, b_out, ln_g, ln_b, n_rows):
    D = rows.D
    inner = w_out.shape[0]
    tm = rows.tile((256, 128))
    return pl.pallas_call(
        _gmlp_kernel,
        grid=(n_rows // tm,),
        in_specs=[pl.BlockSpec((tm, D), lambda i: (i, 0)), rows.mod_spec(0, tm), rows.mod_spec(1, tm),
                  rows.mod_spec(2, tm), _resident(w_in.shape), _resident((1, 2 * inner)), _resident((1, inner)),
                  _resident((1, inner)), _resident(w_s.shape), _resident((CMLP_CHUNK, CMLP_GROUPS)),
                  _resident(w_out.shape), _resident((1, D)), _resident((1, D)), _resident((1, D))],
        out_specs=pl.BlockSpec((tm, D), lambda i: (i, 0)),
        out_shape=jax.ShapeDtypeStruct((n_rows, D), F32),
        scratch_shapes=[pltpu.VMEM((tm, inner), BF16)],
        compiler_params=_cparams("parallel"),
        name="gmlp_ln",
    )(xa, mod, mod, mod, w_in, b_in.reshape(1, -1), vg.reshape(1, -1), vb.reshape(1, -1), w_s, b_s.T, w_out,
      b_out.reshape(1, D), ln_g.reshape(1, D), ln_b.reshape(1, D))


def _ret_proj_kernel(x_ref, sh_ref, sc_ref, w_ref, cos_ref, sin_ref, q_ref, k_ref, v_ref, sg_ref, *, nk, nv, dk):
    h = (x_ref[...] * (1.0 + sc_ref[0]) + sh_ref[0]).astype(BF16)
    cos, sin = cos_ref[...], sin_ref[...]
    k_scale = dk ** -0.5

    def rope_store(dst_ref, col0, scale):
        full = jnp.dot(h, w_ref[:, col0:col0 + nk], preferred_element_type=F32) * scale
        for j in range(nk // LANES):
            t = full[:, j * LANES:(j + 1) * LANES]
            ts = slice((j % (dk // LANES)) * LANES, (j % (dk // LANES) + 1) * LANES)
            dst_ref[:, j * LANES:(j + 1) * LANES] = (t * cos[:, ts] + _swap_pairs(t, dk // 4) * sin[:, ts]).astype(BF16)

    rope_store(q_ref, 0, 1.0)
    rope_store(k_ref, nk, k_scale)
    blk = 512
    for j in range(nv // blk):
        v_ref[:, j * blk:(j + 1) * blk] = jnp.dot(
            h, w_ref[:, 2 * nk + j * blk:2 * nk + (j + 1) * blk], preferred_element_type=F32).astype(BF16)
        gt = jnp.dot(h, w_ref[:, 2 * nk + nv + j * blk:2 * nk + nv + (j + 1) * blk], preferred_element_type=F32)
        sg_ref[:, j * blk:(j + 1) * blk] = (gt * jax.nn.sigmoid(gt)).astype(BF16)


def _ret_project(rows, xa, mod, w, cos, sin):
    D = rows.D
    nk = D
    nv = (w.shape[1] - 2 * nk) // 2
    dk = nk // RET_HEADS
    tm = rows.tile((512, 256))
    row_spec = lambda width: pl.BlockSpec((tm, width), lambda i: (i, 0))
    return pl.pallas_call(
        functools.partial(_ret_proj_kernel, nk=nk, nv=nv, dk=dk),
        grid=(rows.n // tm,),
        in_specs=[row_spec(D), rows.mod_spec(0, tm), rows.mod_spec(1, tm), _resident(w.shape),
                  rows.pos_spec(tm, dk), rows.pos_spec(tm, dk)],
        out_specs=[row_spec(nk), row_spec(nk), row_spec(nv), row_spec(nv)],
        out_shape=[jax.ShapeDtypeStruct((rows.n, nk), BF16), jax.ShapeDtypeStruct((rows.n, nk), BF16),
                   jax.ShapeDtypeStruct((rows.n, nv), BF16), jax.ShapeDtypeStruct((rows.n, nv), BF16)],
        compiler_params=_cparams("parallel"),
        name="ret_proj",
    )(xa, mod, mod, w, cos, sin)


def _ret_scan_kernel(lg_ref, q_ref, k_ref, v_ref, sg_ref, sf_ref, sb_ref, *rest, aliased):
    if aliased:
        rest = rest[1:]
    o_ref, sf_out, sb_out, state_sc, part_sc, inner_sc, cross_sc, weight_sc = rest
    ch = RET_CHUNK
    n_chunks = q_ref.shape[0] // ch
    dk, dv = q_ref.shape[1], v_ref.shape[1]
    head = pl.program_id(1)

    chunk_decay = []
    for d in range(2):
        lg = lg_ref[d, head]
        ii = lax.broadcasted_iota(jnp.int32, (ch, ch), 0)
        jj = lax.broadcasted_iota(jnp.int32, (ch, ch), 1)
        dist = (ii - jj) if d == 0 else (jj - ii)
        inner_sc[d] = jnp.where(dist >= 0, jnp.exp(jnp.maximum(dist, 0).astype(F32) * lg), 0.0)
        row_v = lax.broadcasted_iota(jnp.int32, (ch, dv), 0)
        row_k = lax.broadcasted_iota(jnp.int32, (ch, dk), 0)
        cross_sc[d] = jnp.exp(((row_v if d == 0 else ch - 1 - row_v) + 1).astype(F32) * lg)
        weight_sc[d] = jnp.exp((ch - 1 - (row_k if d == 0 else ch - 1 - row_k)).astype(F32) * lg)
        chunk_decay.append(jnp.exp(jnp.full((1, 1), ch, F32) * lg))
    state_sc[0] = sf_ref[0, 0]
    state_sc[1] = sb_ref[0, 0]

    def visit(d, c):
        rows = pl.ds(pl.multiple_of(c * ch, ch), ch)
        q, k, v = q_ref[rows, :], k_ref[rows, :], v_ref[rows, :]
        scores = lax.dot_general(q, k, (((1,), (1,)), ((), ())), preferred_element_type=F32) * inner_sc[d]
        state = state_sc[d]
        out = (jnp.dot(scores.astype(BF16), v, preferred_element_type=F32)
               + jnp.dot(q, state.astype(BF16), preferred_element_type=F32) * cross_sc[d])
        kw = (k.astype(F32) * weight_sc[d]).astype(BF16)
        state_sc[d] = state * chunk_decay[d] + lax.dot_general(
            kw, v, (((0,), (0,)), ((), ())), preferred_element_type=F32)
        return rows, out

    def finish(rows, o):
        o = o * lax.rsqrt(jnp.mean(o * o, axis=-1, keepdims=True) + RMS_EPS)
        o_ref[rows, :] = (sg_ref[rows, :].astype(F32) * o).astype(BF16)

    def first_visits(t, carry):
        for d, c in ((0, t), (1, n_chunks - 1 - t)):
            rows, out = visit(d, c)
            part_sc[rows, :] = out
        return carry

    def second_visits(t, carry):
        for d, c in ((0, t), (1, n_chunks - 1 - t)):
            rows, out = visit(d, c)
            finish(rows, out + part_sc[rows, :])
        return carry

    lax.fori_loop(0, n_chunks // 2, first_visits, 0)
    lax.fori_loop(n_chunks // 2, n_chunks, second_visits, 0)
    sf_out[0, 0] = state_sc[0]
    sb_out[0, 0] = state_sc[1]


def _ret_scan_segment(rows, log_gamma, q, k, v, sg, state_f, state_b, o_prev, *, seg_len, first_block):
    B = rows.B
    dk = q.shape[1] // RET_HEADS
    dv = v.shape[1] // RET_HEADS
    assert seg_len % (2 * RET_CHUNK) == 0
    kspec = pl.BlockSpec((seg_len, dk), lambda b, h: (first_block + b, h))
    vspec = pl.BlockSpec((seg_len, dv), lambda b, h: (first_block + b, h))
    sspec = pl.BlockSpec((1, 1, dk, dv), lambda b, h: (b, h, 0, 0))
    in_specs = [pl.BlockSpec(memory_space=pltpu.SMEM), kspec, kspec, vspec, vspec, sspec, sspec]
    args = [log_gamma, q, k, v, sg, state_f, state_b]
    aliases = {}
    if o_prev is not None:
        in_specs.append(pl.BlockSpec(memory_space=pl.ANY))
        args.append(o_prev)
        aliases = {len(args) - 1: 0}
    state_shape = jax.ShapeDtypeStruct((B, RET_HEADS, dk, dv), F32)
    return pl.pallas_call(
        functools.partial(_ret_scan_kernel, aliased=o_prev is not None),
        grid=(B, RET_HEADS),
        in_specs=in_specs,
        out_specs=[vspec, sspec, sspec],
        out_shape=[jax.ShapeDtypeStruct((rows.n, v.shape[1]), BF16), state_shape, state_shape],
        scratch_shapes=[pltpu.VMEM((2, dk, dv), F32), pltpu.VMEM((seg_len, dv), F32),
                        pltpu.VMEM((2, RET_CHUNK, RET_CHUNK), F32), pltpu.VMEM((2, RET_CHUNK, dv), F32),
                        pltpu.VMEM((2, RET_CHUNK, dk), F32)],
        input_output_aliases=aliases,
        compiler_params=_cparams("parallel", "parallel"),
        name="ret_scan",
    )(*args)


def _retention(rows, log_gamma, q, k, v, sg):
    B, S, C = rows.B, rows.S, rows.C
    assert rows.n_lat % C == 0
    dk, dv = q.shape[1] // RET_HEADS, v.shape[1] // RET_HEADS
    zeros = jnp.zeros((B, RET_HEADS, dk, dv), F32)
    o, state_f, state_b = _ret_scan_segment(rows, log_gamma, q, k, v, sg, zeros, zeros, None,
                                            seg_len=C, first_block=rows.n_lat // C)
    o, _, _ = _ret_scan_segment(rows, log_gamma, q, k, v, sg, state_f, state_b, o, seg_len=S, first_block=0)
    return o


def kernel(x, c, ctx, c_ctx, l0_ada_w, l0_ada_b, l0_ln1_g, l0_ln1_b, l0_ln2_g, l0_ln2_b, l0_attn_wqkv, l0_attn_q_norm, l0_attn_k_norm, l0_attn_wo, l0_ffn_w_gu, l0_ffn_w_down, l1_ada_w, l1_ada_b, l1_ln1_g, l1_ln1_b, l1_ln2_g, l1_ln2_b, l1_cmlp_w_in, l1_cmlp_b_in, l1_cmlp_v_norm_g, l1_cmlp_v_norm_b, l1_cmlp_w_s, l1_cmlp_b_s, l1_cmlp_w_out, l1_cmlp_b_out, l1_moe_router, l1_moe_w_gu, l1_moe_w_down, l2_ada_w, l2_ada_b, l2_ln1_g, l2_ln1_b, l2_ln2_g, l2_ln2_b, l2_ret_wqkvg, l2_ret_decay, l2_ret_wo, l2_ffn_w_gu, l2_ffn_w_down, l3_ada_w, l3_ada_b, l3_ln1_g, l3_ln1_b, l3_ln2_g, l3_ln2_b, l3_attn_wqkv, l3_attn_q_norm, l3_attn_k_norm, l3_attn_wo, l3_moe_router, l3_moe_w_gu, l3_moe_w_down):
    B, S, D = x.shape
    C = ctx.shape[1]
    rows = _Rows(B, S, C, D)
    assert S % GRID_W == 0 and C % RET_CHUNK == 0 and S % RET_CHUNK == 0

    xa = jnp.concatenate([x.reshape(rows.n_lat, D), ctx.reshape(rows.n_ctx, D)], axis=0)
    n_cond = -(-(B + 1) // 8) * 8
    cond = jnp.concatenate([c, c_ctx[None, :], jnp.zeros((n_cond - B - 1, D), F32)], axis=0)

    def modulation(ada_w, ada_b):
        return _ada(cond, ada_w, ada_b).reshape(n_cond * 6, 1, D)

    pad = rows.tile((512, 256))
    attn_cos, attn_sin = _rope_tables(S, ATTN_HEAD_DIM, pad)
    ret_cos, ret_sin = _rope_tables(S, D // RET_HEADS, pad)
    zero_bias = jnp.zeros((D,), F32)
    bf = lambda w: w.astype(BF16)

    def attention_layer(xa, mod, wqkv, qg, kg, wo, ln_g, ln_b, with_ctx):
        n_rows = rows.n if with_ctx else rows.n_lat
        q, k, v = _attn_project(rows, xa, mod, bf(wqkv), qg, kg, attn_cos, attn_sin)
        o = _attention(rows, q, k, v, with_ctx)
        return _proj_ln(rows, o, bf(wo), zero_bias, xa, mod, ln_g, ln_b, n_rows)

    def retention_layer(xa, mod, wqkvg, decay, wo, ln_g, ln_b):
        log_gamma = -jnp.exp(decay.astype(F32))
        q, k, v, sg = _ret_project(rows, xa, mod, bf(wqkvg), ret_cos, ret_sin)
        o = _retention(rows, log_gamma, q, k, v, sg)
        return _proj_ln(rows, o, bf(wo), zero_bias, xa, mod, ln_g, ln_b, rows.n)

    mod = modulation(l0_ada_w, l0_ada_b)
    xa = attention_layer(xa, mod, l0_attn_wqkv, l0_attn_q_norm, l0_attn_k_norm, l0_attn_wo, l0_ln1_g, l0_ln1_b, True)
    xa = _ffn_ln(rows, xa, mod, bf(l0_ffn_w_gu), bf(l0_ffn_w_down), l0_ln2_g, l0_ln2_b, rows.n)

    mod = modulation(l1_ada_w, l1_ada_b)
    xa = _gmlp_ln(rows, xa, mod, bf(l1_cmlp_w_in), l1_cmlp_b_in, l1_cmlp_v_norm_g, l1_cmlp_v_norm_b,
                  bf(l1_cmlp_w_s), l1_cmlp_b_s, bf(l1_cmlp_w_out), l1_cmlp_b_out, l1_ln1_g, l1_ln1_b, rows.n)
    xa = _moe_ln(rows, xa, mod, l1_moe_router, bf(l1_moe_w_gu), bf(l1_moe_w_down), l1_ln2_g, l1_ln2_b, rows.n)

    mod = modulation(l2_ada_w, l2_ada_b)
    xa = retention_layer(xa, mod, l2_ret_wqkvg, l2_ret_decay, l2_ret_wo, l2_ln1_g, l2_ln1_b)
    xa = _ffn_ln(rows, xa, mod, bf(l2_ffn_w_gu), bf(l2_ffn_w_down), l2_ln2_g, l2_ln2_b, rows.n)

    mod = modulation(l3_ada_w, l3_ada_b)
    xl = attention_layer(xa, mod, l3_attn_wqkv, l3_attn_q_norm, l3_attn_k_norm, l3_attn_wo, l3_ln1_g, l3_ln1_b, False)
    xl = _moe_ln(rows, xl, mod, l3_moe_router, bf(l3_moe_w_gu), bf(l3_moe_w_down), l3_ln2_g, l3_ln2_b, rows.n_lat)
    return xl.reshape(B, S, D)
```

```python
import functools

import jax
import jax.numpy as jnp
from jax import lax
from jax.experimental import pallas as pl
from jax.experimental.pallas import tpu as pltpu

F32 = jnp.float32
BF16 = jnp.bfloat16

DEPTH = 4
GRID_W = 64
ROPE_THETA = 10000.0
DEEPNORM_ALPHA = (2 * DEPTH) ** 0.25
LN_EPS = 1e-5
RMS_EPS = 1e-6
ATTN_HEAD_DIM = 128
ATTN_KV_HEADS = 2
CMLP_CHUNK = 128
CMLP_GROUPS = 8
RET_HEADS = 4
RET_CHUNK = 128
MOE_EXPERTS = 8

LANES = 128
VMEM_LIMIT_BYTES = 56 * 1024 * 1024
NEG_BIG = -1e30


def _cparams(*sem):
    return pltpu.CompilerParams(dimension_semantics=sem, vmem_limit_bytes=VMEM_LIMIT_BYTES)


def _resident(shape):
    nd = len(shape)
    return pl.BlockSpec(shape, lambda *_: (0,) * nd)


def _layer_norm(z, g, b):
    mu = jnp.mean(z, axis=-1, keepdims=True)
    zc = z - mu
    var = jnp.mean(zc * zc, axis=-1, keepdims=True)
    return zc * lax.rsqrt(var + LN_EPS) * g + b


def _pick_tile(candidates, *extents):
    for t in candidates:
        if all(e % t == 0 for e in extents):
            return t
    raise ValueError(f"no tile in {candidates} divides {extents}")


class _Rows:
    def __init__(self, batch, seq, ctx_len, dim):
        self.B, self.S, self.C, self.D = batch, seq, ctx_len, dim
        self.n_lat = batch * seq
        self.n_ctx = batch * ctx_len
        self.n = self.n_lat + self.n_ctx

    def tile(self, candidates):
        return _pick_tile(candidates, self.S, self.n_ctx)

    def mod_spec(self, part, tm):
        S, B = self.S, self.B
        return pl.BlockSpec((1, 1, self.D), lambda i, *_: (jnp.minimum(i * tm // S, B) * 6 + part, 0, 0))

    def pos_spec(self, tm, width):
        n_lat_tiles, per_seq = self.n_lat // tm, self.S // tm
        return pl.BlockSpec((tm, width), lambda i, *_: (jnp.where(i < n_lat_tiles, i % per_seq, per_seq), 0))


def _ada_kernel(c_ref, w_ref, b_ref, o_ref):
    c = c_ref[...]
    s = c * jax.nn.sigmoid(c)
    o_ref[...] = jnp.dot(s, w_ref[...], preferred_element_type=F32, precision=lax.Precision.HIGHEST) + b_ref[...]


def _ada(cond, w, b):
    r, d = cond.shape
    n = w.shape[1]
    tn = _pick_tile((1536, 1024, 512, 128), n)
    return pl.pallas_call(
        _ada_kernel,
        grid=(n // tn,),
        in_specs=[_resident((r, d)), pl.BlockSpec((d, tn), lambda j: (0, j)), pl.BlockSpec((1, tn), lambda j: (0, j))],
        out_specs=pl.BlockSpec((r, tn), lambda j: (0, j)),
        out_shape=jax.ShapeDtypeStruct((r, n), F32),
        compiler_params=_cparams("parallel"),
        name="ada",
    )(cond, w, b.reshape(1, n))


def _rope_tables(seq, head_dim, pad_rows):
    nf = head_dim // 4
    inv_freq = ROPE_THETA ** (-jnp.arange(nf, dtype=F32) / nf)
    t = jnp.arange(seq, dtype=jnp.int32)
    row = (t // GRID_W).astype(F32)[:, None] * inv_freq
    col = (t % GRID_W).astype(F32)[:, None] * inv_freq
    cos = jnp.concatenate([jnp.cos(row), jnp.cos(row), jnp.cos(col), jnp.cos(col)], axis=-1)
    sin = jnp.concatenate([-jnp.sin(row), jnp.sin(row), -jnp.sin(col), jnp.sin(col)], axis=-1)
    cos = jnp.concatenate([cos, jnp.ones((pad_rows, head_dim), F32)], axis=0)
    sin = jnp.concatenate([sin, jnp.zeros((pad_rows, head_dim), F32)], axis=0)
    return cos, sin


def _swap_pairs(y, half):
    if 2 * half == LANES:
        return pltpu.roll(y, half, 1)
    lane = lax.broadcasted_iota(jnp.int32, y.shape, 1)
    return jnp.where(lane % (2 * half) < half, pltpu.roll(y, LANES - half, 1), pltpu.roll(y, half, 1))


KEY_BLOCK = 256
LOG2_E = 1.4426950408889634


def _qkv_kernel(x_ref, sh_ref, sc_ref, w_ref, qg_ref, kg_ref, cos_ref, sin_ref, q_ref, kt_ref, v_ref, *, nq, nkv):
    dh = ATTN_HEAD_DIM
    h = (x_ref[...] * (1.0 + sc_ref[0]) + sh_ref[0]).astype(BF16)
    cos, sin = cos_ref[...], sin_ref[...]
    scale = dh ** -0.5 * LOG2_E

    def normed_rope(t, g):
        y = t * lax.rsqrt(jnp.mean(t * t, axis=-1, keepdims=True) + RMS_EPS) * g
        return y * cos + _swap_pairs(y, dh // 4) * sin

    t = jnp.dot(h, w_ref[...], preferred_element_type=F32)
    for hd in range(nq // dh):
        q_ref[:, hd * dh:(hd + 1) * dh] = (normed_rope(t[:, hd * dh:(hd + 1) * dh], qg_ref[...]) * scale).astype(BF16)
    k = jnp.concatenate([normed_rope(t[:, nq + hd * dh:nq + (hd + 1) * dh], kg_ref[...]) for hd in range(nkv // dh)],
                        axis=1)
    for j in range(kt_ref.shape[0]):
        kt_ref[j] = k[j * KEY_BLOCK:(j + 1) * KEY_BLOCK, :].T.astype(BF16)
    v_ref[...] = t[:, nq + nkv:].astype(BF16)


def _attn_project(rows, xa, mod, w, qg, kg, cos, sin):
    D = rows.D
    nkv = ATTN_KV_HEADS * ATTN_HEAD_DIM
    nq = w.shape[1] - 2 * nkv
    tm = rows.tile((512, 256))
    row_spec = lambda width: pl.BlockSpec((tm, width), lambda i: (i, 0))
    return pl.pallas_call(
        functools.partial(_qkv_kernel, nq=nq, nkv=nkv),
        grid=(rows.n // tm,),
        in_specs=[row_spec(D), rows.mod_spec(0, tm), rows.mod_spec(1, tm), _resident(w.shape),
                  _resident((1, ATTN_HEAD_DIM)), _resident((1, ATTN_HEAD_DIM)),
                  rows.pos_spec(tm, ATTN_HEAD_DIM), rows.pos_spec(tm, ATTN_HEAD_DIM)],
        out_specs=[row_spec(nq), pl.BlockSpec((tm // KEY_BLOCK, nkv, KEY_BLOCK), lambda i: (i, 0, 0)), row_spec(nkv)],
        out_shape=[jax.ShapeDtypeStruct((rows.n, nq), BF16),
                   jax.ShapeDtypeStruct((rows.n // KEY_BLOCK, nkv, KEY_BLOCK), BF16),
                   jax.ShapeDtypeStruct((rows.n, nkv), BF16)],
        compiler_params=_cparams("parallel"),
        name="attn_qkv",
    )(xa, mod, mod, w, qg.reshape(1, -1), kg.reshape(1, -1), cos, sin)


def _attn_kernel(q_ref, kc_ref, vc_ref, kl_ref, vl_ref, o_ref, m_sc, acc_sc, *, tq, blocks_per_chunk, n_lat_tiles):
    dh = ATTN_HEAD_DIM
    group = q_ref.shape[1] // dh // ATTN_KV_HEADS
    is_latent = pl.program_id(1) < n_lat_tiles
    n_lat_chunks = jnp.where(is_latent, kl_ref.shape[0] // blocks_per_chunk, 0)
    chunk = blocks_per_chunk * KEY_BLOCK

    heads = range(ATTN_KV_HEADS)
    head_rows = group * tq
    gs = [slice(g * dh, (g + 1) * dh) for g in heads]
    qs = [jnp.concatenate([q_ref[:, (g * group + j) * dh:(g * group + j + 1) * dh] for j in range(group)], axis=0)
          for g in heads]
    m_sc[...] = jnp.full(m_sc.shape, NEG_BIG, F32)
    acc_sc[...] = jnp.zeros(acc_sc.shape, F32)

    def online_softmax_step(kts, vs):
        width = kts[0].shape[1]
        s = jnp.concatenate([jnp.dot(qs[g], kts[g], preferred_element_type=F32) for g in heads], axis=0)
        m_old = m_sc[...]
        m_new = jnp.maximum(m_old, jnp.max(s, axis=-1, keepdims=True))
        p = jnp.exp2(s - jnp.concatenate([m_new] * (width // LANES), axis=1)).astype(BF16)
        alpha = jnp.exp2(m_old - m_new)
        ones = jnp.ones((width, dh), BF16)
        pv = jnp.concatenate([jnp.dot(p[g * head_rows:(g + 1) * head_rows], jnp.concatenate([vs[g], ones], axis=1),
                                      preferred_element_type=F32) for g in heads], axis=0)
        acc_sc[...] = jnp.concatenate([alpha, alpha], axis=1) * acc_sc[...] + pv
        m_sc[...] = m_new

    for c in range(kc_ref.shape[0]):
        online_softmax_step([kc_ref[c, gs[g], :] for g in heads],
                            [vc_ref[c * KEY_BLOCK:(c + 1) * KEY_BLOCK, gs[g]] for g in heads])

    def latent_chunk(c, carry):
        r0 = pl.multiple_of(c * chunk, chunk)
        online_softmax_step(
            [jnp.concatenate([kl_ref[c * blocks_per_chunk + j, gs[g], :] for j in range(blocks_per_chunk)], axis=1)
             for g in heads],
            [vl_ref[pl.ds(r0, chunk), gs[g]] for g in heads])
        return carry

    lax.fori_loop(0, n_lat_chunks, latent_chunk, 0)
    acc = acc_sc[...]
    o = acc[:, :dh] / acc[:, dh:]
    for hd in range(ATTN_KV_HEADS * group):
        o_ref[:, hd * dh:(hd + 1) * dh] = o[hd * tq:(hd + 1) * tq].astype(BF16)


def _attention(rows, q, kt, v, with_ctx):
    B, S, C = rows.B, rows.S, rows.C
    nq = q.shape[1]
    nkv = v.shape[1]
    assert S % KEY_BLOCK == 0 and C % KEY_BLOCK == 0 and rows.n_lat % C == 0
    tq = _pick_tile((256, 128), S, C)
    blocks_per_chunk = 2 if S % (2 * KEY_BLOCK) == 0 else 1
    lat_tiles, ctx_tiles = S // tq, C // tq
    tiles = lat_tiles + (ctx_tiles if with_ctx else 0)
    group = nq // nkv

    def q_map(b, i):
        return (jnp.where(i < lat_tiles, b * lat_tiles + i, rows.n_lat // tq + b * ctx_tiles + (i - lat_tiles)), 0)

    kt_ctx = pl.BlockSpec((C // KEY_BLOCK, nkv, KEY_BLOCK), lambda b, i: (rows.n_lat // C + b, 0, 0))
    kt_lat = pl.BlockSpec((S // KEY_BLOCK, nkv, KEY_BLOCK), lambda b, i: (b, 0, 0))
    v_ctx = pl.BlockSpec((C, nkv), lambda b, i: (rows.n_lat // C + b, 0))
    v_lat = pl.BlockSpec((S, nkv), lambda b, i: (b, 0))
    return pl.pallas_call(
        functools.partial(_attn_kernel, tq=tq, blocks_per_chunk=blocks_per_chunk, n_lat_tiles=lat_tiles),
        grid=(B, tiles),
        in_specs=[pl.BlockSpec((tq, nq), q_map), kt_ctx, v_ctx, kt_lat, v_lat],
        out_specs=pl.BlockSpec((tq, nq), q_map),
        out_shape=jax.ShapeDtypeStruct((rows.n if with_ctx else rows.n_lat, nq), BF16),
        scratch_shapes=[pltpu.VMEM((ATTN_KV_HEADS * group * tq, LANES), F32),
                        pltpu.VMEM((ATTN_KV_HEADS * group * tq, 2 * ATTN_HEAD_DIM), F32)],
        compiler_params=_cparams("parallel", "parallel"),
        name="attn_core",
    )(q, kt, v, kt, v)


def _proj_ln_kernel(o_ref, w_ref, bias_ref, x_ref, gate_ref, g_ref, b_ref, out_ref):
    y = jnp.dot(o_ref[...], w_ref[...], preferred_element_type=F32) + bias_ref[...]
    out_ref[...] = _layer_norm(DEEPNORM_ALPHA * x_ref[...] + gate_ref[0] * y, g_ref[...], b_ref[...])


def _proj_ln(rows, o, w, bias, xa, mod, ln_g, ln_b, n_rows):
    D = rows.D
    kdim = o.shape[1]
    tm = rows.tile((512, 256, 128))
    return pl.pallas_call(
        _proj_ln_kernel,
        grid=(n_rows // tm,),
        in_specs=[pl.BlockSpec((tm, kdim), lambda i: (i, 0)), _resident(w.shape), _resident((1, D)),
                  pl.BlockSpec((tm, D), lambda i: (i, 0)), rows.mod_spec(2, tm), _resident((1, D)), _resident((1, D))],
        out_specs=pl.BlockSpec((tm, D), lambda i: (i, 0)),
        out_shape=jax.ShapeDtypeStruct((n_rows, D), F32),
        compiler_params=_cparams("parallel"),
        name="proj_ln",
    )(o, w, bias.reshape(1, D), xa, mod, ln_g.reshape(1, D), ln_b.reshape(1, D))


def _ffn_kernel(x_ref, sh_ref, sc_ref, gate_ref, wg_ref, wu_ref, wd_ref, g_ref, b_ref, out_ref, h_sc, acc_sc):
    f = pl.program_id(1)

    @pl.when(f == 0)
    def _():
        h_sc[...] = (x_ref[...] * (1.0 + sc_ref[0]) + sh_ref[0]).astype(BF16)
        acc_sc[...] = jnp.zeros(acc_sc.shape, F32)

    h = h_sc[...]
    gt = jnp.dot(h, wg_ref[...], preferred_element_type=F32)
    up = jnp.dot(h, wu_ref[...], preferred_element_type=F32)
    act = (gt * jax.nn.sigmoid(gt) * up).astype(BF16)
    acc_sc[...] += jnp.dot(act, wd_ref[...], preferred_element_type=F32)

    @pl.when(f == pl.num_programs(1) - 1)
    def _():
        out_ref[...] = _layer_norm(DEEPNORM_ALPHA * x_ref[...] + gate_ref[0] * acc_sc[...], g_ref[...], b_ref[...])


def _ffn_ln(rows, xa, mod, w_gu, w_down, ln_g, ln_b, n_rows, row_tiles=(1024, 512, 256, 128),
            hidden_tiles=(512, 256, 128)):
    D = rows.D
    F = w_down.shape[0]
    tm = rows.tile(row_tiles)
    tf = _pick_tile(hidden_tiles, F)
    nf = F // tf
    return pl.pallas_call(
        _ffn_kernel,
        grid=(n_rows // tm, nf),
        in_specs=[pl.BlockSpec((tm, D), lambda i, f: (i, 0)),
                  rows.mod_spec(3, tm), rows.mod_spec(4, tm), rows.mod_spec(5, tm),
                  pl.BlockSpec((D, tf), lambda i, f: (0, f)), pl.BlockSpec((D, tf), lambda i, f: (0, nf + f)),
                  pl.BlockSpec((tf, D), lambda i, f: (f, 0)), _resident((1, D)), _resident((1, D))],
        out_specs=pl.BlockSpec((tm, D), lambda i, f: (i, 0)),
        out_shape=jax.ShapeDtypeStruct((n_rows, D), F32),
        scratch_shapes=[pltpu.VMEM((tm, D), BF16), pltpu.VMEM((tm, D), F32)],
        compiler_params=_cparams("parallel", "arbitrary"),
        name="ffn_ln",
    )(xa, mod, mod, mod, w_gu, w_gu, w_down, ln_g.reshape(1, D), ln_b.reshape(1, D))


MOE_ROW_BLOCKS = (512, 256, 128)


def _router_kernel(x_ref, sh_ref, sc_ref, r_ref, h_ref, gates_ref, *, n_experts):
    h = x_ref[...] * (1.0 + sc_ref[0]) + sh_ref[0]
    h_ref[...] = h.astype(BF16)
    logits = jnp.dot(h, r_ref[...], preferred_element_type=F32, precision=lax.Precision.HIGHEST)
    lane = lax.broadcasted_iota(jnp.int32, logits.shape, 1)
    lowest = jnp.finfo(F32).min
    lg = jnp.where(lane < n_experts, logits, lowest)
    m1 = jnp.max(lg, axis=-1, keepdims=True)
    i1 = jnp.min(jnp.where(lg == m1, lane, LANES), axis=-1, keepdims=True)
    lg2 = jnp.where(lane == i1, lowest, lg)
    m2 = jnp.max(lg2, axis=-1, keepdims=True)
    i2 = jnp.min(jnp.where(lg2 == m2, lane, LANES), axis=-1, keepdims=True)
    e2 = jnp.exp(m2 - m1)
    denom = 1.0 + e2
    gates = jnp.where(lane == i1, 1.0 / denom, 0.0) + jnp.where(lane == i2, e2 / denom, 0.0)
    gates_ref[...] = gates.T[:n_experts, :]


def _router(rows, xa, mod, router, n_rows):
    D = rows.D
    n_experts = router.shape[1]
    tm = rows.tile((512, 256, 128))
    router_pad = jnp.pad(router, ((0, 0), (0, LANES - n_experts)))
    return pl.pallas_call(
        functools.partial(_router_kernel, n_experts=n_experts),
        grid=(n_rows // tm,),
        in_specs=[pl.BlockSpec((tm, D), lambda i: (i, 0)), rows.mod_spec(3, tm), rows.mod_spec(4, tm),
                  _resident((D, LANES))],
        out_specs=[pl.BlockSpec((tm, D), lambda i: (i, 0)), pl.BlockSpec((n_experts, tm), lambda i: (0, i))],
        out_shape=[jax.ShapeDtypeStruct((n_rows, D), BF16), jax.ShapeDtypeStruct((n_experts, n_rows), F32)],
        compiler_params=_cparams("parallel"),
        name="moe_router",
    )(xa, mod, mod, router_pad)


def _plan_kernel(g_ref, rank_ref, cnt_ref):
    n_experts, tile = g_ref.shape
    routed = g_ref[...] > 0.0
    ones = jnp.where(routed, 1.0, 0.0)
    before = jnp.where(lax.broadcasted_iota(jnp.int32, (LANES, LANES), 0)
                       < lax.broadcasted_iota(jnp.int32, (LANES, LANES), 1), 1.0, 0.0)
    seen = jnp.zeros((n_experts, 1), F32)
    for c in range(tile // LANES):
        cs = slice(c * LANES, (c + 1) * LANES)
        rank = jnp.dot(ones[:, cs], before, preferred_element_type=F32) + seen
        rank_ref[:, cs] = jnp.where(routed[:, cs], rank, -1.0).astype(jnp.int32)
        seen = seen + jnp.sum(ones[:, cs], axis=1, keepdims=True)
    cnt_ref[0] = jnp.broadcast_to(seen, (n_experts, LANES)).astype(jnp.int32)


def _moe_plan(gates_t, tile):
    n_experts, n_rows = gates_t.shape
    n_tiles = n_rows // tile
    rank, counts = pl.pallas_call(
        _plan_kernel,
        grid=(n_tiles,),
        in_specs=[pl.BlockSpec((n_experts, tile), lambda s: (0, s))],
        out_specs=[pl.BlockSpec((n_experts, tile), lambda s: (0, s)),
                   pl.BlockSpec((1, n_experts, LANES), lambda s: (s, 0, 0))],
        out_shape=[jax.ShapeDtypeStruct((n_experts, n_rows), jnp.int32),
                   jax.ShapeDtypeStruct((n_tiles, n_experts, LANES), jnp.int32)],
        compiler_params=_cparams("parallel"),
        name="moe_plan",
    )(gates_t)
    count = counts[:, :, 0].reshape(-1)
    smallest = MOE_ROW_BLOCKS[-1]
    return rank, (count + smallest - 1) // smallest * smallest


def _moe_kernel(nrows_ref, h_ref, rank_ref, gate_ref, wg_ref, wu_ref, wd_ref, out_ref, xg_sc, y_sc):
    s, e, f = pl.program_id(0), pl.program_id(1), pl.program_id(2)
    tile = h_ref.shape[0]
    n_rows = nrows_ref[s * pl.num_programs(1) + e]
    big = MOE_ROW_BLOCKS[0]

    @pl.when((e == 0) & (f == 0))
    def _():
        out_ref[...] = jnp.zeros(out_ref.shape, F32)

    def for_each_block(fn):
        def body(r, carry):
            fn(r * big, big)
            return carry

        lax.fori_loop(0, n_rows // big, body, 0)
        for size in MOE_ROW_BLOCKS[1:]:
            @pl.when(n_rows // size % 2 == 1)
            def _():
                fn(n_rows // (2 * size) * (2 * size), size)

    def rows_at(start, size):
        return pl.ds(pl.multiple_of(start, MOE_ROW_BLOCKS[-1]), size)

    def slot_matches(start, size):
        slot = lax.broadcasted_iota(jnp.int32, (size, tile), 0) + start
        return slot == rank_ref[pl.ds(e, 1), :]

    def gather(start, size):
        onehot = jnp.where(slot_matches(start, size), 1.0, 0.0).astype(BF16)
        xg_sc[rows_at(start, size), :] = jnp.dot(onehot, h_ref[...], preferred_element_type=F32).astype(BF16)
        y_sc[rows_at(start, size), :] = jnp.zeros((size, y_sc.shape[1]), F32)

    def expert(start, size):
        xb = xg_sc[rows_at(start, size), :]
        gt = jnp.dot(xb, wg_ref[0], preferred_element_type=F32)
        up = jnp.dot(xb, wu_ref[0], preferred_element_type=F32)
        act = (gt * jax.nn.sigmoid(gt) * up).astype(BF16)
        y_sc[rows_at(start, size), :] += jnp.dot(act, wd_ref[0], preferred_element_type=F32)

    def scatter(start, size):
        weighted = jnp.where(slot_matches(start, size), gate_ref[pl.ds(e, 1), :], 0.0).astype(BF16)
        out_ref[...] += lax.dot_general(weighted, y_sc[rows_at(start, size), :].astype(BF16),
                                        (((0,), (0,)), ((), ())), preferred_element_type=F32)

    @pl.when(f == 0)
    def _():
        for_each_block(gather)

    for_each_block(expert)

    @pl.when(f == pl.num_programs(2) - 1)
    def _():
        for_each_block(scatter)


def _moe(h, gates_t, w_gu, w_down):
    n_rows, D = h.shape
    E, F = w_down.shape[0], w_down.shape[1]
    tile = _pick_tile((2176, 2048, 1024, 512, 256, 128), n_rows)
    tf = _pick_tile((512, 256, 128), F)
    nf = F // tf
    rank, n_block_rows = _moe_plan(gates_t, tile)
    max_rows = -(-tile // MOE_ROW_BLOCKS[-1]) * MOE_ROW_BLOCKS[-1]
    grid_spec = pltpu.PrefetchScalarGridSpec(
        num_scalar_prefetch=1,
        grid=(n_rows // tile, E, nf),
        in_specs=[pl.BlockSpec((tile, D), lambda s, e, f, *_: (s, 0)),
                  pl.BlockSpec((E, tile), lambda s, e, f, *_: (0, s)),
                  pl.BlockSpec((E, tile), lambda s, e, f, *_: (0, s)),
                  pl.BlockSpec((1, D, tf), lambda s, e, f, *_: (e, 0, f)),
                  pl.BlockSpec((1, D, tf), lambda s, e, f, *_: (e, 0, nf + f)),
                  pl.BlockSpec((1, tf, D), lambda s, e, f, *_: (e, f, 0))],
        out_specs=pl.BlockSpec((tile, D), lambda s, e, f, *_: (s, 0)),
        scratch_shapes=[pltpu.VMEM((max_rows, D), BF16), pltpu.VMEM((max_rows, D), F32)],
    )
    return pl.pallas_call(
        _moe_kernel,
        grid_spec=grid_spec,
        out_shape=jax.ShapeDtypeStruct((n_rows, D), F32),
        compiler_params=_cparams("parallel", "arbitrary", "arbitrary"),
        name="moe_experts",
    )(n_block_rows, h, rank, gates_t, w_gu, w_gu, w_down)


def _residual_ln_kernel(x_ref, f_ref, gate_ref, g_ref, b_ref, out_ref):
    out_ref[...] = _layer_norm(DEEPNORM_ALPHA * x_ref[...] + gate_ref[0] * f_ref[...], g_ref[...], b_ref[...])


def _residual_ln(rows, xa, f, mod, part, ln_g, ln_b, n_rows):
    D = rows.D
    tm = rows.tile((512, 256, 128))
    return pl.pallas_call(
        _residual_ln_kernel,
        grid=(n_rows // tm,),
        in_specs=[pl.BlockSpec((tm, D), lambda i: (i, 0)), pl.BlockSpec((tm, D), lambda i: (i, 0)),
                  rows.mod_spec(part, tm), _resident((1, D)), _resident((1, D))],
        out_specs=pl.BlockSpec((tm, D), lambda i: (i, 0)),
        out_shape=jax.ShapeDtypeStruct((n_rows, D), F32),
        compiler_params=_cparams("parallel"),
        name="residual_ln",
    )(xa, f, mod, ln_g.reshape(1, D), ln_b.reshape(1, D))


def _moe_ln(rows, xa, mod, router, w_gu, w_down, ln_g, ln_b, n_rows):
    h, gates_t = _router(rows, xa, mod, router, n_rows)
    return _residual_ln(rows, xa, _moe(h, gates_t, w_gu, w_down), mod, 5, ln_g, ln_b, n_rows)


def _gmlp_kernel(x_ref, sh_ref, sc_ref, gate_ref, win_ref, bin_ref, vg_ref, vb_ref, ws_ref, bs_ref, wout_ref,
                 bout_ref, g_ref, b_ref, out_ref, gated_sc):
    tm = x_ref.shape[0]
    inner = wout_ref.shape[0]
    gdim = inner // CMLP_GROUPS
    x = x_ref[...]
    h = (x * (1.0 + sc_ref[0]) + sh_ref[0]).astype(BF16)
    v = jax.nn.gelu(jnp.dot(h, win_ref[:, inner:], preferred_element_type=F32) + bin_ref[:, inner:])
    v = _layer_norm(v, vg_ref[...], vb_ref[...]).astype(BF16)
    for gi in range(CMLP_GROUPS):
        cs = slice(gi * gdim, (gi + 1) * gdim)
        u = jax.nn.gelu(jnp.dot(h, win_ref[:, cs], preferred_element_type=F32) + bin_ref[:, cs])
        w_s = ws_ref[gi]
        bias = bs_ref[:, gi:gi + 1]
        for c in range(tm // CMLP_CHUNK):
            rs = slice(c * CMLP_CHUNK, (c + 1) * CMLP_CHUNK)
            mixed = jnp.dot(w_s, v[rs, cs], preferred_element_type=F32) + bias
            gated_sc[rs, cs] = (u[rs] * mixed).astype(BF16)
    y = jnp.dot(gated_sc[...], wout_ref[...], preferred_element_type=F32) + bout_ref[...]
    out_ref[...] = _layer_norm(DEEPNORM_ALPHA * x + gate_ref[0] * y, g_ref[...], b_ref[...])


def _gmlp_ln(rows, xa, mod, w_in, b_in, vg, vb, w_s, b_s, w_out, b_out, ln_g, ln_b, n_rows):
    D = rows.D
    inner = w_out.shape[0]
    tm = rows.tile((512, 256, 128))
    return pl.pallas_call(
        _gmlp_kernel,
        grid=(n_rows // tm,),
        in_specs=[pl.BlockSpec((tm, D), lambda i: (i, 0)), rows.mod_spec(0, tm), rows.mod_spec(1, tm),
                  rows.mod_spec(2, tm), _resident(w_in.shape), _resident((1, 2 * inner)), _resident((1, inner)),
                  _resident((1, inner)), _resident(w_s.shape), _resident((CMLP_CHUNK, CMLP_GROUPS)),
                  _resident(w_out.shape), _resident((1, D)), _resident((1, D)), _resident((1, D))],
        out_specs=pl.BlockSpec((tm, D), lambda i: (i, 0)),
        out_shape=jax.ShapeDtypeStruct((n_rows, D), F32),
        scratch_shapes=[pltpu.VMEM((tm, inner), BF16)],
        compiler_params=_cparams("parallel"),
        name="gmlp_ln",
    )(xa, mod, mod, mod, w_in, b_in.reshape(1, -1), vg.reshape(1, -1), vb.reshape(1, -1), w_s, b_s.T, w_out,
      b_out.reshape(1, D), ln_g.reshape(1, D), ln_b.reshape(1, D))


def _ret_proj_kernel(x_ref, sh_ref, sc_ref, w_ref, cos_ref, sin_ref, q_ref, k_ref, v_ref, sg_ref, *, nk, nv, dk):
    h = (x_ref[...] * (1.0 + sc_ref[0]) + sh_ref[0]).astype(BF16)
    cos, sin = cos_ref[...], sin_ref[...]
    k_scale = dk ** -0.5

    def rope_store(dst_ref, col0, scale):
        full = jnp.dot(h, w_ref[:, col0:col0 + nk], preferred_element_type=F32) * scale
        for j in range(nk // LANES):
            t = full[:, j * LANES:(j + 1) * LANES]
            ts = slice((j % (dk // LANES)) * LANES, (j % (dk // LANES) + 1) * LANES)
            dst_ref[:, j * LANES:(j + 1) * LANES] = (t * cos[:, ts] + _swap_pairs(t, dk // 4) * sin[:, ts]).astype(BF16)

    rope_store(q_ref, 0, 1.0)
    rope_store(k_ref, nk, k_scale)
    blk = 512
    for j in range(nv // blk):
        v_ref[:, j * blk:(j + 1) * blk] = jnp.dot(
            h, w_ref[:, 2 * nk + j * blk:2 * nk + (j + 1) * blk], preferred_element_type=F32).astype(BF16)
        gt = jnp.dot(h, w_ref[:, 2 * nk + nv + j * blk:2 * nk + nv + (j + 1) * blk], preferred_element_type=F32)
        sg_ref[:, j * blk:(j + 1) * blk] = (gt * jax.nn.sigmoid(gt)).astype(BF16)


def _ret_project(rows, xa, mod, w, cos, sin):
    D = rows.D
    nk = D
    nv = (w.shape[1] - 2 * nk) // 2
    dk = nk // RET_HEADS
    tm = rows.tile((512, 256))
    row_spec = lambda width: pl.BlockSpec((tm, width), lambda i: (i, 0))
    return pl.pallas_call(
        functools.partial(_ret_proj_kernel, nk=nk, nv=nv, dk=dk),
        grid=(rows.n // tm,),
        in_specs=[row_spec(D), rows.mod_spec(0, tm), rows.mod_spec(1, tm), _resident(w.shape),
                  rows.pos_spec(tm, dk), rows.pos_spec(tm, dk)],
        out_specs=[row_spec(nk), row_spec(nk), row_spec(nv), row_spec(nv)],
        out_shape=[jax.ShapeDtypeStruct((rows.n, nk), BF16), jax.ShapeDtypeStruct((rows.n, nk), BF16),
                   jax.ShapeDtypeStruct((rows.n, nv), BF16), jax.ShapeDtypeStruct((rows.n, nv), BF16)],
        compiler_params=_cparams("parallel"),
        name="ret_proj",
    )(xa, mod, mod, w, cos, sin)


def _ret_scan_kernel(lg_ref, q_ref, k_ref, v_ref, sg_ref, sf_ref, sb_ref, *rest, aliased):
    if aliased:
        rest = rest[1:]
    o_ref, sf_out, sb_out, state_sc, part_sc, inner_sc, cross_sc, weight_sc = rest
    ch = RET_CHUNK
    n_chunks = q_ref.shape[0] // ch
    dk, dv = q_ref.shape[1], v_ref.shape[1]
    head = pl.program_id(1)

    chunk_decay = []
    for d in range(2):
        lg = lg_ref[d, head]
        ii = lax.broadcasted_iota(jnp.int32, (ch, ch), 0)
        jj = lax.broadcasted_iota(jnp.int32, (ch, ch), 1)
        dist = (ii - jj) if d == 0 else (jj - ii)
        inner_sc[d] = jnp.where(dist >= 0, jnp.exp(jnp.maximum(dist, 0).astype(F32) * lg), 0.0)
        row_v = lax.broadcasted_iota(jnp.int32, (ch, dv), 0)
        row_k = lax.broadcasted_iota(jnp.int32, (ch, dk), 0)
        cross_sc[d] = jnp.exp(((row_v if d == 0 else ch - 1 - row_v) + 1).astype(F32) * lg)
        weight_sc[d] = jnp.exp((ch - 1 - (row_k if d == 0 else ch - 1 - row_k)).astype(F32) * lg)
        chunk_decay.append(jnp.exp(jnp.full((1, 1), ch, F32) * lg))
    state_sc[0] = sf_ref[0, 0]
    state_sc[1] = sb_ref[0, 0]

    def visit(d, c):
        rows = pl.ds(pl.multiple_of(c * ch, ch), ch)
        q, k, v = q_ref[rows, :], k_ref[rows, :], v_ref[rows, :]
        scores = lax.dot_general(q, k, (((1,), (1,)), ((), ())), preferred_element_type=F32) * inner_sc[d]
        state = state_sc[d]
        out = (jnp.dot(scores.astype(BF16), v, preferred_element_type=F32)
               + jnp.dot(q, state.astype(BF16), preferred_element_type=F32) * cross_sc[d])
        kw = (k.astype(F32) * weight_sc[d]).astype(BF16)
        state_sc[d] = state * chunk_decay[d] + lax.dot_general(
            kw, v, (((0,), (0,)), ((), ())), preferred_element_type=F32)
        return rows, out

    def finish(rows, o):
        o = o * lax.rsqrt(jnp.mean(o * o, axis=-1, keepdims=True) + RMS_EPS)
        o_ref[rows, :] = (sg_ref[rows, :].astype(F32) * o).astype(BF16)

    def first_visits(t, carry):
        for d, c in ((0, t), (1, n_chunks - 1 - t)):
            rows, out = visit(d, c)
            part_sc[rows, :] = out
        return carry

    def second_visits(t, carry):
        for d, c in ((0, t), (1, n_chunks - 1 - t)):
            rows, out = visit(d, c)
            finish(rows, out + part_sc[rows, :])
        return carry

    lax.fori_loop(0, n_chunks // 2, first_visits, 0)
    lax.fori_loop(n_chunks // 2, n_chunks, second_visits, 0)
    sf_out[0, 0] = state_sc[0]
    sb_out[0, 0] = state_sc[1]


def _ret_scan_segment(rows, log_gamma, q, k, v, sg, state_f, state_b, o_prev, *, seg_len, first_block):
    B = rows.B
    dk = q.shape[1] // RET_HEADS
    dv = v.shape[1] // RET_HEADS
    assert seg_len % (2 * RET_CHUNK) == 0
    kspec = pl.BlockSpec((seg_len, dk), lambda b, h: (first_block + b, h))
    vspec = pl.BlockSpec((seg_len, dv), lambda b, h: (first_block + b, h))
    sspec = pl.BlockSpec((1, 1, dk, dv), lambda b, h: (b, h, 0, 0))
    in_specs = [pl.BlockSpec(memory_space=pltpu.SMEM), kspec, kspec, vspec, vspec, sspec, sspec]
    args = [log_gamma, q, k, v, sg, state_f, state_b]
    aliases = {}
    if o_prev is not None:
        in_specs.append(pl.BlockSpec(memory_space=pl.ANY))
        args.append(o_prev)
        aliases = {len(args) - 1: 0}
    state_shape = jax.ShapeDtypeStruct((B, RET_HEADS, dk, dv), F32)
    return pl.pallas_call(
        functools.partial(_ret_scan_kernel, aliased=o_prev is not None),
        grid=(B, RET_HEADS),
        in_specs=in_specs,
        out_specs=[vspec, sspec, sspec],
        out_shape=[jax.ShapeDtypeStruct((rows.n, v.shape[1]), BF16), state_shape, state_shape],
        scratch_shapes=[pltpu.VMEM((2, dk, dv), F32), pltpu.VMEM((seg_len, dv), F32),
                        pltpu.VMEM((2, RET_CHUNK, RET_CHUNK), F32), pltpu.VMEM((2, RET_CHUNK, dv), F32),
                        pltpu.VMEM((2, RET_CHUNK, dk), F32)],
        input_output_aliases=aliases,
        compiler_params=_cparams("parallel", "parallel"),
        name="ret_scan",
    )(*args)


def _retention(rows, log_gamma, q, k, v, sg):
    B, S, C = rows.B, rows.S, rows.C
    assert rows.n_lat % C == 0
    dk, dv = q.shape[1] // RET_HEADS, v.shape[1] // RET_HEADS
    zeros = jnp.zeros((B, RET_HEADS, dk, dv), F32)
    o, state_f, state_b = _ret_scan_segment(rows, log_gamma, q, k, v, sg, zeros, zeros, None,
                                            seg_len=C, first_block=rows.n_lat // C)
    o, _, _ = _ret_scan_segment(rows, log_gamma, q, k, v, sg, state_f, state_b, o, seg_len=S, first_block=0)
    return o


def kernel(x, c, ctx, c_ctx, l0_ada_w, l0_ada_b, l0_ln1_g, l0_ln1_b, l0_ln2_g, l0_ln2_b, l0_attn_wqkv, l0_attn_q_norm, l0_attn_k_norm, l0_attn_wo, l0_ffn_w_gu, l0_ffn_w_down, l1_ada_w, l1_ada_b, l1_ln1_g, l1_ln1_b, l1_ln2_g, l1_ln2_b, l1_cmlp_w_in, l1_cmlp_b_in, l1_cmlp_v_norm_g, l1_cmlp_v_norm_b, l1_cmlp_w_s, l1_cmlp_b_s, l1_cmlp_w_out, l1_cmlp_b_out, l1_moe_router, l1_moe_w_gu, l1_moe_w_down, l2_ada_w, l2_ada_b, l2_ln1_g, l2_ln1_b, l2_ln2_g, l2_ln2_b, l2_ret_wqkvg, l2_ret_decay, l2_ret_wo, l2_ffn_w_gu, l2_ffn_w_down, l3_ada_w, l3_ada_b, l3_ln1_g, l3_ln1_b, l3_ln2_g, l3_ln2_b, l3_attn_wqkv, l3_attn_q_norm, l3_attn_k_norm, l3_attn_wo, l3_moe_router, l3_moe_w_gu, l3_moe_w_down):
    B, S, D = x.shape
    C = ctx.shape[1]
    rows = _Rows(B, S, C, D)
    assert S % GRID_W == 0 and C % RET_CHUNK == 0 and S % RET_CHUNK == 0

    xa = jnp.concatenate([x.reshape(rows.n_lat, D), ctx.reshape(rows.n_ctx, D)], axis=0)
    n_cond = -(-(B + 1) // 8) * 8
    cond = jnp.concatenate([c, c_ctx[None, :], jnp.zeros((n_cond - B - 1, D), F32)], axis=0)

    def modulation(ada_w, ada_b):
        return _ada(cond, ada_w, ada_b).reshape(n_cond * 6, 1, D)

    pad = rows.tile((512, 256))
    attn_cos, attn_sin = _rope_tables(S, ATTN_HEAD_DIM, pad)
    ret_cos, ret_sin = _rope_tables(S, D // RET_HEADS, pad)
    zero_bias = jnp.zeros((D,), F32)
    bf = lambda w: w.astype(BF16)

    def attention_layer(xa, mod, wqkv, qg, kg, wo, ln_g, ln_b, with_ctx):
        n_rows = rows.n if with_ctx else rows.n_lat
        q, k, v = _attn_project(rows, xa, mod, bf(wqkv), qg, kg, attn_cos, attn_sin)
        o = _attention(rows, q, k, v, with_ctx)
        return _proj_ln(rows, o, bf(wo), zero_bias, xa, mod, ln_g, ln_b, n_rows)

    def retention_layer(xa, mod, wqkvg, decay, wo, ln_g, ln_b):
        log_gamma = -jnp.exp(decay.astype(F32))
        q, k, v, sg = _ret_project(rows, xa, mod, bf(wqkvg), ret_cos, ret_sin)
        o = _retention(rows, log_gamma, q, k, v, sg)
        return _proj_ln(rows, o, bf(wo), zero_bias, xa, mod, ln_g, ln_b, rows.n)

    mod = modulation(l0_ada_w, l0_ada_b)
    xa = attention_layer(xa, mod, l0_attn_wqkv, l0_attn_q_norm, l0_attn_k_norm, l0_attn_wo, l0_ln1_g, l0_ln1_b, True)
    xa = _ffn_ln(rows, xa, mod, bf(l0_ffn_w_gu), bf(l0_ffn_w_down), l0_ln2_g, l0_ln2_b, rows.n,
                 row_tiles=(512, 256, 128), hidden_tiles=(1792, 512, 256, 128))

    mod = modulation(l1_ada_w, l1_ada_b)
    xa = _gmlp_ln(rows, xa, mod, bf(l1_cmlp_w_in), l1_cmlp_b_in, l1_cmlp_v_norm_g, l1_cmlp_v_norm_b,
                  bf(l1_cmlp_w_s), l1_cmlp_b_s, bf(l1_cmlp_w_out), l1_cmlp_b_out, l1_ln1_g, l1_ln1_b, rows.n)
    xa = _moe_ln(rows, xa, mod, l1_moe_router, bf(l1_moe_w_gu), bf(l1_moe_w_down), l1_ln2_g, l1_ln2_b, rows.n)

    mod = modulation(l2_ada_w, l2_ada_b)
    xa = retention_layer(xa, mod, l2_ret_wqkvg, l2_ret_decay, l2_ret_wo, l2_ln1_g, l2_ln1_b)
    xa = _ffn_ln(rows, xa, mod, bf(l2_ffn_w_gu), bf(l2_ffn_w_down), l2_ln2_g, l2_ln2_b, rows.n)

    mod = modulation(l3_ada_w, l3_ada_b)
    xl = attention_layer(xa, mod, l3_attn_wqkv, l3_attn_q_norm, l3_attn_k_norm, l3_attn_wo, l3_ln1_g, l3_ln1_b, False)
    xl = _moe_ln(rows, xl, mod, l3_moe_router, bf(l3_moe_w_gu), bf(l3_moe_w_down), l3_ln2_g, l3_ln2_b, rows.n_lat)
    return xl.reshape(B, S, D)
```

```python
import functools

import jax
import jax.numpy as jnp
from jax import lax
from jax.experimental import pallas as pl
from jax.experimental.pallas import tpu as pltpu

F32 = jnp.float32
BF16 = jnp.bfloat16

DEPTH = 4
GRID_W = 64
ROPE_THETA = 10000.0
DEEPNORM_ALPHA = (2 * DEPTH) ** 0.25
LN_EPS = 1e-5
RMS_EPS = 1e-6
ATTN_HEAD_DIM = 128
ATTN_KV_HEADS = 2
CMLP_CHUNK = 128
CMLP_GROUPS = 8
RET_HEADS = 4
RET_CHUNK = 128
MOE_EXPERTS = 8

LANES = 128
VMEM_LIMIT_BYTES = 56 * 1024 * 1024
NEG_BIG = -1e30


def _cparams(*sem):
    return pltpu.CompilerParams(dimension_semantics=sem, vmem_limit_bytes=VMEM_LIMIT_BYTES)


def _resident(shape):
    nd = len(shape)
    return pl.BlockSpec(shape, lambda *_: (0,) * nd)


def _layer_norm(z, g, b):
    mu = jnp.mean(z, axis=-1, keepdims=True)
    zc = z - mu
    var = jnp.mean(zc * zc, axis=-1, keepdims=True)
    return zc * lax.rsqrt(var + LN_EPS) * g + b


def _pick_tile(candidates, *extents):
    for t in candidates:
        if all(e % t == 0 for e in extents):
            return t
    raise ValueError(f"no tile in {candidates} divides {extents}")


class _Rows:
    def __init__(self, batch, seq, ctx_len, dim):
        self.B, self.S, self.C, self.D = batch, seq, ctx_len, dim
        self.n_lat = batch * seq
        self.n_ctx = batch * ctx_len
        self.n = self.n_lat + self.n_ctx

    def tile(self, candidates):
        return _pick_tile(candidates, self.S, self.n_ctx)

    def mod_spec(self, part, tm):
        S, B = self.S, self.B
        return pl.BlockSpec((1, 1, self.D), lambda i, *_: (jnp.minimum(i * tm // S, B) * 6 + part, 0, 0))

    def pos_spec(self, tm, width):
        n_lat_tiles, per_seq = self.n_lat // tm, self.S // tm
        return pl.BlockSpec((tm, width), lambda i, *_: (jnp.where(i < n_lat_tiles, i % per_seq, per_seq), 0))


def _ada_kernel(c_ref, w_ref, b_ref, o_ref):
    c = c_ref[...]
    s = c * jax.nn.sigmoid(c)
    o_ref[...] = jnp.dot(s, w_ref[...], preferred_element_type=F32, precision=lax.Precision.HIGHEST) + b_ref[...]


def _ada(cond, w, b):
    r, d = cond.shape
    n = w.shape[1]
    tn = _pick_tile((1536, 1024, 512, 128), n)
    return pl.pallas_call(
        _ada_kernel,
        grid=(n // tn,),
        in_specs=[_resident((r, d)), pl.BlockSpec((d, tn), lambda j: (0, j)), pl.BlockSpec((1, tn), lambda j: (0, j))],
        out_specs=pl.BlockSpec((r, tn), lambda j: (0, j)),
        out_shape=jax.ShapeDtypeStruct((r, n), F32),
        compiler_params=_cparams("parallel"),
        name="ada",
    )(cond, w, b.reshape(1, n))


def _rope_tables(seq, head_dim, pad_rows):
    nf = head_dim // 4
    inv_freq = ROPE_THETA ** (-jnp.arange(nf, dtype=F32) / nf)
    t = jnp.arange(seq, dtype=jnp.int32)
    row = (t // GRID_W).astype(F32)[:, None] * inv_freq
    col = (t % GRID_W).astype(F32)[:, None] * inv_freq
    cos = jnp.concatenate([jnp.cos(row), jnp.cos(row), jnp.cos(col), jnp.cos(col)], axis=-1)
    sin = jnp.concatenate([-jnp.sin(row), jnp.sin(row), -jnp.sin(col), jnp.sin(col)], axis=-1)
    cos = jnp.concatenate([cos, jnp.ones((pad_rows, head_dim), F32)], axis=0)
    sin = jnp.concatenate([sin, jnp.zeros((pad_rows, head_dim), F32)], axis=0)
    return cos, sin


def _swap_pairs(y, half):
    if 2 * half == LANES:
        return pltpu.roll(y, half, 1)
    lane = lax.broadcasted_iota(jnp.int32, y.shape, 1)
    return jnp.where(lane % (2 * half) < half, pltpu.roll(y, LANES - half, 1), pltpu.roll(y, half, 1))


KEY_BLOCK = 256
LOG2_E = 1.4426950408889634


def _qkv_kernel(x_ref, sh_ref, sc_ref, w_ref, qg_ref, kg_ref, cos_ref, sin_ref, q_ref, kt_ref, v_ref, *, nq, nkv):
    dh = ATTN_HEAD_DIM
    h = (x_ref[...] * (1.0 + sc_ref[0]) + sh_ref[0]).astype(BF16)
    cos, sin = cos_ref[...], sin_ref[...]
    scale = dh ** -0.5 * LOG2_E

    def normed_rope(t, g):
        y = t * lax.rsqrt(jnp.mean(t * t, axis=-1, keepdims=True) + RMS_EPS) * g
        return y * cos + _swap_pairs(y, dh // 4) * sin

    t = jnp.dot(h, w_ref[...], preferred_element_type=F32)
    for hd in range(nq // dh):
        q_ref[:, hd * dh:(hd + 1) * dh] = (normed_rope(t[:, hd * dh:(hd + 1) * dh], qg_ref[...]) * scale).astype(BF16)
    k = jnp.concatenate([normed_rope(t[:, nq + hd * dh:nq + (hd + 1) * dh], kg_ref[...]) for hd in range(nkv // dh)],
                        axis=1)
    for j in range(kt_ref.shape[0]):
        kt_ref[j] = k[j * KEY_BLOCK:(j + 1) * KEY_BLOCK, :].T.astype(BF16)
    v_ref[...] = t[:, nq + nkv:].astype(BF16)


def _attn_project(rows, xa, mod, w, qg, kg, cos, sin):
    D = rows.D
    nkv = ATTN_KV_HEADS * ATTN_HEAD_DIM
    nq = w.shape[1] - 2 * nkv
    tm = rows.tile((512, 256))
    row_spec = lambda width: pl.BlockSpec((tm, width), lambda i: (i, 0))
    return pl.pallas_call(
        functools.partial(_qkv_kernel, nq=nq, nkv=nkv),
        grid=(rows.n // tm,),
        in_specs=[row_spec(D), rows.mod_spec(0, tm), rows.mod_spec(1, tm), _resident(w.shape),
                  _resident((1, ATTN_HEAD_DIM)), _resident((1, ATTN_HEAD_DIM)),
                  rows.pos_spec(tm, ATTN_HEAD_DIM), rows.pos_spec(tm, ATTN_HEAD_DIM)],
        out_specs=[row_spec(nq), pl.BlockSpec((tm // KEY_BLOCK, nkv, KEY_BLOCK), lambda i: (i, 0, 0)), row_spec(nkv)],
        out_shape=[jax.ShapeDtypeStruct((rows.n, nq), BF16),
                   jax.ShapeDtypeStruct((rows.n // KEY_BLOCK, nkv, KEY_BLOCK), BF16),
                   jax.ShapeDtypeStruct((rows.n, nkv), BF16)],
        compiler_params=_cparams("parallel"),
        name="attn_qkv",
    )(xa, mod, mod, w, qg.reshape(1, -1), kg.reshape(1, -1), cos, sin)


def _attn_kernel(q_ref, kc_ref, vc_ref, kl_ref, vl_ref, o_ref, m_sc, acc_sc, *, tq, blocks_per_chunk, n_lat_tiles):
    dh = ATTN_HEAD_DIM
    group = q_ref.shape[1] // dh // ATTN_KV_HEADS
    is_latent = pl.program_id(1) < n_lat_tiles
    n_lat_chunks = jnp.where(is_latent, kl_ref.shape[0] // blocks_per_chunk, 0)
    chunk = blocks_per_chunk * KEY_BLOCK

    heads = range(ATTN_KV_HEADS)
    head_rows = group * tq
    gs = [slice(g * dh, (g + 1) * dh) for g in heads]
    qs = [jnp.concatenate([q_ref[:, (g * group + j) * dh:(g * group + j + 1) * dh] for j in range(group)], axis=0)
          for g in heads]
    m_sc[...] = jnp.full(m_sc.shape, NEG_BIG, F32)
    acc_sc[...] = jnp.zeros(acc_sc.shape, F32)

    def online_softmax_step(kts, vs):
        width = kts[0].shape[1]
        s = jnp.concatenate([jnp.dot(qs[g], kts[g], preferred_element_type=F32) for g in heads], axis=0)
        m_old = m_sc[...]
        m_new = jnp.maximum(m_old, jnp.max(s, axis=-1, keepdims=True))
        p = jnp.exp2(s - jnp.concatenate([m_new] * (width // LANES), axis=1)).astype(BF16)
        alpha = jnp.exp2(m_old - m_new)
        ones = jnp.ones((width, dh), BF16)
        pv = jnp.concatenate([jnp.dot(p[g * head_rows:(g + 1) * head_rows], jnp.concatenate([vs[g], ones], axis=1),
                                      preferred_element_type=F32) for g in heads], axis=0)
        acc_sc[...] = jnp.concatenate([alpha, alpha], axis=1) * acc_sc[...] + pv
        m_sc[...] = m_new

    for c in range(kc_ref.shape[0]):
        online_softmax_step([kc_ref[c, gs[g], :] for g in heads],
                            [vc_ref[c * KEY_BLOCK:(c + 1) * KEY_BLOCK, gs[g]] for g in heads])

    def latent_chunk(c, carry):
        r0 = pl.multiple_of(c * chunk, chunk)
        online_softmax_step(
            [jnp.concatenate([kl_ref[c * blocks_per_chunk + j, gs[g], :] for j in range(blocks_per_chunk)], axis=1)
             for g in heads],
            [vl_ref[pl.ds(r0, chunk), gs[g]] for g in heads])
        return carry

    lax.fori_loop(0, n_lat_chunks, latent_chunk, 0)
    acc = acc_sc[...]
    o = acc[:, :dh] / acc[:, dh:]
    for hd in range(ATTN_KV_HEADS * group):
        o_ref[:, hd * dh:(hd + 1) * dh] = o[hd * tq:(hd + 1) * tq].astype(BF16)


def _attention(rows, q, kt, v, with_ctx):
    B, S, C = rows.B, rows.S, rows.C
    nq = q.shape[1]
    nkv = v.shape[1]
    assert S % KEY_BLOCK == 0 and C % KEY_BLOCK == 0 and rows.n_lat % C == 0
    tq = _pick_tile((256, 128), S, C)
    blocks_per_chunk = 2 if S % (2 * KEY_BLOCK) == 0 else 1
    lat_tiles, ctx_tiles = S // tq, C // tq
    tiles = lat_tiles + (ctx_tiles if with_ctx else 0)
    group = nq // nkv

    def q_map(b, i):
        return (jnp.where(i < lat_tiles, b * lat_tiles + i, rows.n_lat // tq + b * ctx_tiles + (i - lat_tiles)), 0)

    kt_ctx = pl.BlockSpec((C // KEY_BLOCK, nkv, KEY_BLOCK), lambda b, i: (rows.n_lat // C + b, 0, 0))
    kt_lat = pl.BlockSpec((S // KEY_BLOCK, nkv, KEY_BLOCK), lambda b, i: (b, 0, 0))
    v_ctx = pl.BlockSpec((C, nkv), lambda b, i: (rows.n_lat // C + b, 0))
    v_lat = pl.BlockSpec((S, nkv), lambda b, i: (b, 0))
    return pl.pallas_call(
        functools.partial(_attn_kernel, tq=tq, blocks_per_chunk=blocks_per_chunk, n_lat_tiles=lat_tiles),
        grid=(B, tiles),
        in_specs=[pl.BlockSpec((tq, nq), q_map), kt_ctx, v_ctx, kt_lat, v_lat],
        out_specs=pl.BlockSpec((tq, nq), q_map),
        out_shape=jax.ShapeDtypeStruct((rows.n if with_ctx else rows.n_lat, nq), BF16),
        scratch_shapes=[pltpu.VMEM((ATTN_KV_HEADS * group * tq, LANES), F32),
                        pltpu.VMEM((ATTN_KV_HEADS * group * tq, 2 * ATTN_HEAD_DIM), F32)],
        compiler_params=_cparams("parallel", "parallel"),
        name="attn_core",
    )(q, kt, v, kt, v)


def _proj_ln_kernel(*refs, first_tiles):
    n_seg = len(first_tiles) - 1
    o_refs = refs[:n_seg]
    w_ref, bias_ref, x_ref, gate_ref, g_ref, b_ref, out_ref = refs[n_seg:]
    i = pl.program_id(0)
    for seg, o_ref in enumerate(o_refs):
        @pl.when((i >= first_tiles[seg]) & (i < first_tiles[seg + 1]))
        def _():
            y = jnp.dot(o_ref[...], w_ref[...], preferred_element_type=F32) + bias_ref[...]
            out_ref[...] = _layer_norm(DEEPNORM_ALPHA * x_ref[...] + gate_ref[0] * y, g_ref[...], b_ref[...])


def _proj_ln(rows, o_segments, w, bias, xa, mod, ln_g, ln_b):
    D = rows.D
    kdim = w.shape[0]
    tm = rows.tile((512, 256, 128))
    first_tiles = [0]
    for o in o_segments:
        assert o.shape[0] % tm == 0
        first_tiles.append(first_tiles[-1] + o.shape[0] // tm)
    n_rows = first_tiles[-1] * tm

    def segment_spec(seg):
        first, count = first_tiles[seg], first_tiles[seg + 1] - first_tiles[seg]
        return pl.BlockSpec((tm, kdim), lambda i: (jnp.clip(i - first, 0, count - 1), 0))

    return pl.pallas_call(
        functools.partial(_proj_ln_kernel, first_tiles=tuple(first_tiles)),
        grid=(n_rows // tm,),
        in_specs=[segment_spec(seg) for seg in range(len(o_segments))] + [
            _resident(w.shape), _resident((1, D)), pl.BlockSpec((tm, D), lambda i: (i, 0)), rows.mod_spec(2, tm),
            _resident((1, D)), _resident((1, D))],
        out_specs=pl.BlockSpec((tm, D), lambda i: (i, 0)),
        out_shape=jax.ShapeDtypeStruct((n_rows, D), F32),
        compiler_params=_cparams("parallel"),
        name="proj_ln",
    )(*o_segments, w, bias.reshape(1, D), xa, mod, ln_g.reshape(1, D), ln_b.reshape(1, D))


def _ffn_kernel(x_ref, sh_ref, sc_ref, gate_ref, wg_ref, wu_ref, wd_ref, g_ref, b_ref, out_ref, h_sc, acc_sc):
    f = pl.program_id(1)

    @pl.when(f == 0)
    def _():
        h_sc[...] = (x_ref[...] * (1.0 + sc_ref[0]) + sh_ref[0]).astype(BF16)
        acc_sc[...] = jnp.zeros(acc_sc.shape, F32)

    h = h_sc[...]
    gt = jnp.dot(h, wg_ref[...], preferred_element_type=F32)
    up = jnp.dot(h, wu_ref[...], preferred_element_type=F32)
    act = (gt * jax.nn.sigmoid(gt) * up).astype(BF16)
    acc_sc[...] += jnp.dot(act, wd_ref[...], preferred_element_type=F32)

    @pl.when(f == pl.num_programs(1) - 1)
    def _():
        out_ref[...] = _layer_norm(DEEPNORM_ALPHA * x_ref[...] + gate_ref[0] * acc_sc[...], g_ref[...], b_ref[...])


def _ffn_ln(rows, xa, mod, w_gu, w_down, ln_g, ln_b, n_rows):
    D = rows.D
    F = w_down.shape[0]
    tm = rows.tile((512, 256, 128))
    tf = _pick_tile((1792, 512, 256, 128), F)
    nf = F // tf
    return pl.pallas_call(
        _ffn_kernel,
        grid=(n_rows // tm, nf),
        in_specs=[pl.BlockSpec((tm, D), lambda i, f: (i, 0)),
                  rows.mod_spec(3, tm), rows.mod_spec(4, tm), rows.mod_spec(5, tm),
                  pl.BlockSpec((D, tf), lambda i, f: (0, f)), pl.BlockSpec((D, tf), lambda i, f: (0, nf + f)),
                  pl.BlockSpec((tf, D), lambda i, f: (f, 0)), _resident((1, D)), _resident((1, D))],
        out_specs=pl.BlockSpec((tm, D), lambda i, f: (i, 0)),
        out_shape=jax.ShapeDtypeStruct((n_rows, D), F32),
        scratch_shapes=[pltpu.VMEM((tm, D), BF16), pltpu.VMEM((tm, D), F32)],
        compiler_params=_cparams("parallel", "arbitrary"),
        name="ffn_ln",
    )(xa, mod, mod, mod, w_gu, w_gu, w_down, ln_g.reshape(1, D), ln_b.reshape(1, D))


MOE_TOP_K = 2
MOE_SMALL_BLOCK = 128
MOE_MAX_SMALL_BLOCKS = 2
MOE_BLOCK_ALIGN = 64


def _moe_big_block(tile, n_experts):
    expected = tile * MOE_TOP_K // n_experts
    return -(-(expected + expected // 8) // MOE_BLOCK_ALIGN) * MOE_BLOCK_ALIGN


def _router_kernel(x_ref, sh_ref, sc_ref, r_ref, h_ref, gates_ref, *, n_experts):
    h = x_ref[...] * (1.0 + sc_ref[0]) + sh_ref[0]
    h_ref[...] = h.astype(BF16)
    logits = jnp.dot(h, r_ref[...], preferred_element_type=F32, precision=lax.Precision.HIGHEST)
    lane = lax.broadcasted_iota(jnp.int32, logits.shape, 1)
    lowest = jnp.finfo(F32).min
    lg = jnp.where(lane < n_experts, logits, lowest)
    m1 = jnp.max(lg, axis=-1, keepdims=True)
    i1 = jnp.min(jnp.where(lg == m1, lane, LANES), axis=-1, keepdims=True)
    lg2 = jnp.where(lane == i1, lowest, lg)
    m2 = jnp.max(lg2, axis=-1, keepdims=True)
    i2 = jnp.min(jnp.where(lg2 == m2, lane, LANES), axis=-1, keepdims=True)
    e2 = jnp.exp(m2 - m1)
    denom = 1.0 + e2
    gates = jnp.where(lane == i1, 1.0 / denom, 0.0) + jnp.where(lane == i2, e2 / denom, 0.0)
    gates_ref[...] = gates.T[:n_experts, :]


def _router(rows, xa, mod, router, n_rows):
    D = rows.D
    n_experts = router.shape[1]
    tm = rows.tile((512, 256, 128))
    router_pad = jnp.pad(router, ((0, 0), (0, LANES - n_experts)))
    return pl.pallas_call(
        functools.partial(_router_kernel, n_experts=n_experts),
        grid=(n_rows // tm,),
        in_specs=[pl.BlockSpec((tm, D), lambda i: (i, 0)), rows.mod_spec(3, tm), rows.mod_spec(4, tm),
                  _resident((D, LANES))],
        out_specs=[pl.BlockSpec((tm, D), lambda i: (i, 0)), pl.BlockSpec((n_experts, tm), lambda i: (0, i))],
        out_shape=[jax.ShapeDtypeStruct((n_rows, D), BF16), jax.ShapeDtypeStruct((n_experts, n_rows), F32)],
        compiler_params=_cparams("parallel"),
        name="moe_router",
    )(xa, mod, mod, router_pad)


def _plan_kernel(g_ref, rank_ref, cnt_ref):
    n_experts, tile = g_ref.shape
    routed = g_ref[...] > 0.0
    ones = jnp.where(routed, 1.0, 0.0)
    before = jnp.where(lax.broadcasted_iota(jnp.int32, (LANES, LANES), 0)
                       < lax.broadcasted_iota(jnp.int32, (LANES, LANES), 1), 1.0, 0.0)
    seen = jnp.zeros((n_experts, 1), F32)
    for c in range(tile // LANES):
        cs = slice(c * LANES, (c + 1) * LANES)
        rank = jnp.dot(ones[:, cs], before, preferred_element_type=F32) + seen
        rank_ref[:, cs] = jnp.where(routed[:, cs], rank, -1.0).astype(jnp.int32)
        seen = seen + jnp.sum(ones[:, cs], axis=1, keepdims=True)
    cnt_ref[0] = jnp.broadcast_to(seen, (n_experts, LANES)).astype(jnp.int32)


def _moe_plan(gates_t, tile, big):
    n_experts, n_rows = gates_t.shape
    n_tiles = n_rows // tile
    rank, counts = pl.pallas_call(
        _plan_kernel,
        grid=(n_tiles,),
        in_specs=[pl.BlockSpec((n_experts, tile), lambda s: (0, s))],
        out_specs=[pl.BlockSpec((n_experts, tile), lambda s: (0, s)),
                   pl.BlockSpec((1, n_experts, LANES), lambda s: (s, 0, 0))],
        out_shape=[jax.ShapeDtypeStruct((n_experts, n_rows), jnp.int32),
                   jax.ShapeDtypeStruct((n_tiles, n_experts, LANES), jnp.int32)],
        compiler_params=_cparams("parallel"),
        name="moe_plan",
    )(gates_t)
    count = counts[:, :, 0].reshape(-1)
    n_big = count // big
    rest = count - n_big * big
    round_up = rest > MOE_MAX_SMALL_BLOCKS * MOE_SMALL_BLOCK
    n_big = n_big + round_up.astype(jnp.int32)
    n_small = jnp.where(round_up, 0, (rest + MOE_SMALL_BLOCK - 1) // MOE_SMALL_BLOCK)
    return rank, n_big, n_small


def _moe_kernel(nbig_ref, nsmall_ref, h_ref, rank_ref, gate_ref, wg_ref, wu_ref, wd_ref, out_ref, xg_sc, y_sc, *,
                big):
    s, e, f = pl.program_id(0), pl.program_id(1), pl.program_id(2)
    tile = h_ref.shape[0]
    pair = s * pl.num_programs(1) + e
    n_big, n_small = nbig_ref[pair], nsmall_ref[pair]

    @pl.when((e == 0) & (f == 0))
    def _():
        out_ref[...] = jnp.zeros(out_ref.shape, F32)

    def for_each_block(fn):
        def big_block(r, carry):
            fn(r * big, big)
            return carry

        def small_block(r, carry):
            fn(n_big * big + r * MOE_SMALL_BLOCK, MOE_SMALL_BLOCK)
            return carry

        lax.fori_loop(0, n_big, big_block, 0)
        lax.fori_loop(0, n_small, small_block, 0)

    def rows_at(start, size):
        return pl.ds(pl.multiple_of(start, MOE_BLOCK_ALIGN), size)

    def slot_matches(start, size):
        slot = lax.broadcasted_iota(jnp.int32, (size, tile), 0) + start
        return slot == rank_ref[pl.ds(e, 1), :]

    def gather(start, size):
        onehot = jnp.where(slot_matches(start, size), 1.0, 0.0).astype(BF16)
        xg_sc[rows_at(start, size), :] = jnp.dot(onehot, h_ref[...], preferred_element_type=F32).astype(BF16)
        y_sc[rows_at(start, size), :] = jnp.zeros((size, y_sc.shape[1]), F32)

    def expert(start, size):
        xb = xg_sc[rows_at(start, size), :]
        gt = jnp.dot(xb, wg_ref[0], preferred_element_type=F32)
        up = jnp.dot(xb, wu_ref[0], preferred_element_type=F32)
        act = (gt * jax.nn.sigmoid(gt) * up).astype(BF16)
        y_sc[rows_at(start, size), :] += jnp.dot(act, wd_ref[0], preferred_element_type=F32)

    def scatter(start, size):
        weighted = jnp.where(slot_matches(start, size), gate_ref[pl.ds(e, 1), :], 0.0).astype(BF16)
        out_ref[...] += lax.dot_general(weighted, y_sc[rows_at(start, size), :].astype(BF16),
                                        (((0,), (0,)), ((), ())), preferred_element_type=F32)

    @pl.when(f == 0)
    def _():
        for_each_block(gather)

    for_each_block(expert)

    @pl.when(f == pl.num_programs(2) - 1)
    def _():
        for_each_block(scatter)


def _moe(h, gates_t, w_gu, w_down):
    n_rows, D = h.shape
    E, F = w_down.shape[0], w_down.shape[1]
    tile = _pick_tile((2176, 2048, 1024, 512, 256, 128), n_rows)
    tf = _pick_tile((512, 256, 128), F)
    nf = F // tf
    big = _moe_big_block(tile, E)
    rank, n_big, n_small = _moe_plan(gates_t, tile, big)
    max_rows = tile + max(big, MOE_SMALL_BLOCK)
    grid_spec = pltpu.PrefetchScalarGridSpec(
        num_scalar_prefetch=2,
        grid=(n_rows // tile, E, nf),
        in_specs=[pl.BlockSpec((tile, D), lambda s, e, f, *_: (s, 0)),
                  pl.BlockSpec((E, tile), lambda s, e, f, *_: (0, s)),
                  pl.BlockSpec((E, tile), lambda s, e, f, *_: (0, s)),
                  pl.BlockSpec((1, D, tf), lambda s, e, f, *_: (e, 0, f)),
                  pl.BlockSpec((1, D, tf), lambda s, e, f, *_: (e, 0, nf + f)),
                  pl.BlockSpec((1, tf, D), lambda s, e, f, *_: (e, f, 0))],
        out_specs=pl.BlockSpec((tile, D), lambda s, e, f, *_: (s, 0)),
        scratch_shapes=[pltpu.VMEM((max_rows, D), BF16), pltpu.VMEM((max_rows, D), F32)],
    )
    return pl.pallas_call(
        functools.partial(_moe_kernel, big=big),
        grid_spec=grid_spec,
        out_shape=jax.ShapeDtypeStruct((n_rows, D), F32),
        compiler_params=_cparams("parallel", "arbitrary", "arbitrary"),
        name="moe_experts",
    )(n_big, n_small, h, rank, gates_t, w_gu, w_gu, w_down)


def _residual_ln_kernel(x_ref, f_ref, gate_ref, g_ref, b_ref, out_ref):
    out_ref[...] = _layer_norm(DEEPNORM_ALPHA * x_ref[...] + gate_ref[0] * f_ref[...], g_ref[...], b_ref[...])


def _residual_ln(rows, xa, f, mod, part, ln_g, ln_b, n_rows):
    D = rows.D
    tm = rows.tile((512, 256, 128))
    return pl.pallas_call(
        _residual_ln_kernel,
        grid=(n_rows // tm,),
        in_specs=[pl.BlockSpec((tm, D), lambda i: (i, 0)), pl.BlockSpec((tm, D), lambda i: (i, 0)),
                  rows.mod_spec(part, tm), _resident((1, D)), _resident((1, D))],
        out_specs=pl.BlockSpec((tm, D), lambda i: (i, 0)),
        out_shape=jax.ShapeDtypeStruct((n_rows, D), F32),
        compiler_params=_cparams("parallel"),
        name="residual_ln",
    )(xa, f, mod, ln_g.reshape(1, D), ln_b.reshape(1, D))


def _moe_ln(rows, xa, mod, router, w_gu, w_down, ln_g, ln_b, n_rows):
    h, gates_t = _router(rows, xa, mod, router, n_rows)
    return _residual_ln(rows, xa, _moe(h, gates_t, w_gu, w_down), mod, 5, ln_g, ln_b, n_rows)


def _gmlp_kernel(x_ref, sh_ref, sc_ref, gate_ref, win_ref, bin_ref, vg_ref, vb_ref, ws_ref, bs_ref, wout_ref,
                 bout_ref, g_ref, b_ref, out_ref, gated_sc):
    tm = x_ref.shape[0]
    inner = wout_ref.shape[0]
    gdim = inner // CMLP_GROUPS
    x = x_ref[...]
    h = (x * (1.0 + sc_ref[0]) + sh_ref[0]).astype(BF16)
    v = jax.nn.gelu(jnp.dot(h, win_ref[:, inner:], preferred_element_type=F32) + bin_ref[:, inner:])
    v = _layer_norm(v, vg_ref[...], vb_ref[...]).astype(BF16)
    for gi in range(CMLP_GROUPS):
        cs = slice(gi * gdim, (gi + 1) * gdim)
        u = jax.nn.gelu(jnp.dot(h, win_ref[:, cs], preferred_element_type=F32) + bin_ref[:, cs])
        w_s = ws_ref[gi]
        bias = bs_ref[:, gi:gi + 1]
        for c in range(tm // CMLP_CHUNK):
            rs = slice(c * CMLP_CHUNK, (c + 1) * CMLP_CHUNK)
            mixed = jnp.dot(w_s, v[rs, cs], preferred_element_type=F32) + bias
            gated_sc[rs, cs] = (u[rs] * mixed).astype(BF16)
    y = jnp.dot(gated_sc[...], wout_ref[...], preferred_element_type=F32) + bout_ref[...]
    out_ref[...] = _layer_norm(DEEPNORM_ALPHA * x + gate_ref[0] * y, g_ref[...], b_ref[...])


def _gmlp_ln(rows, xa, mod, w_in, b_in, vg, vb, w_s, b_s, w_out, b_out, ln_g, ln_b, n_rows):
    D = rows.D
    inner = w_out.shape[0]
    tm = rows.tile((512, 256, 128))
    return pl.pallas_call(
        _gmlp_kernel,
        grid=(n_rows // tm,),
        in_specs=[pl.BlockSpec((tm, D), lambda i: (i, 0)), rows.mod_spec(0, tm), rows.mod_spec(1, tm),
                  rows.mod_spec(2, tm), _resident(w_in.shape), _resident((1, 2 * inner)), _resident((1, inner)),
                  _resident((1, inner)), _resident(w_s.shape), _resident((CMLP_CHUNK, CMLP_GROUPS)),
                  _resident(w_out.shape), _resident((1, D)), _resident((1, D)), _resident((1, D))],
        out_specs=pl.BlockSpec((tm, D), lambda i: (i, 0)),
        out_shape=jax.ShapeDtypeStruct((n_rows, D), F32),
        scratch_shapes=[pltpu.VMEM((tm, inner), BF16)],
        compiler_params=_cparams("parallel"),
        name="gmlp_ln",
    )(xa, mod, mod, mod, w_in, b_in.reshape(1, -1), vg.reshape(1, -1), vb.reshape(1, -1), w_s, b_s.T, w_out,
      b_out.reshape(1, D), ln_g.reshape(1, D), ln_b.reshape(1, D))


def _ret_proj_kernel(x_ref, sh_ref, sc_ref, w_ref, cos_ref, sin_ref, q_ref, k_ref, v_ref, sg_ref, *, nk, nv, dk):
    h = (x_ref[...] * (1.0 + sc_ref[0]) + sh_ref[0]).astype(BF16)
    cos, sin = cos_ref[...], sin_ref[...]
    k_scale = dk ** -0.5

    def rope_store(dst_ref, col0, scale):
        full = jnp.dot(h, w_ref[:, col0:col0 + nk], preferred_element_type=F32) * scale
        for j in range(nk // LANES):
            t = full[:, j * LANES:(j + 1) * LANES]
            ts = slice((j % (dk // LANES)) * LANES, (j % (dk // LANES) + 1) * LANES)
            dst_ref[:, j * LANES:(j + 1) * LANES] = (t * cos[:, ts] + _swap_pairs(t, dk // 4) * sin[:, ts]).astype(BF16)

    rope_store(q_ref, 0, 1.0)
    rope_store(k_ref, nk, k_scale)
    blk = 512
    for j in range(nv // blk):
        v_ref[:, j * blk:(j + 1) * blk] = jnp.dot(
            h, w_ref[:, 2 * nk + j * blk:2 * nk + (j + 1) * blk], preferred_element_type=F32).astype(BF16)
        gt = jnp.dot(h, w_ref[:, 2 * nk + nv + j * blk:2 * nk + nv + (j + 1) * blk], preferred_element_type=F32)
        sg_ref[:, j * blk:(j + 1) * blk] = (gt * jax.nn.sigmoid(gt)).astype(BF16)


def _ret_project(rows, xa, mod, w, cos, sin):
    D = rows.D
    nk = D
    nv = (w.shape[1] - 2 * nk) // 2
    dk = nk // RET_HEADS
    tm = rows.tile((512, 256))
    row_spec = lambda width: pl.BlockSpec((tm, width), lambda i: (i, 0))
    return pl.pallas_call(
        functools.partial(_ret_proj_kernel, nk=nk, nv=nv, dk=dk),
        grid=(rows.n // tm,),
        in_specs=[row_spec(D), rows.mod_spec(0, tm), rows.mod_spec(1, tm), _resident(w.shape),
                  rows.pos_spec(tm, dk), rows.pos_spec(tm, dk)],
        out_specs=[row_spec(nk), row_spec(nk), row_spec(nv), row_spec(nv)],
        out_shape=[jax.ShapeDtypeStruct((rows.n, nk), BF16), jax.ShapeDtypeStruct((rows.n, nk), BF16),
                   jax.ShapeDtypeStruct((rows.n, nv), BF16), jax.ShapeDtypeStruct((rows.n, nv), BF16)],
        compiler_params=_cparams("parallel"),
        name="ret_proj",
    )(xa, mod, mod, w, cos, sin)


RET_SCAN_CHUNK = 256


def _ret_scan_kernel(lg_ref, q_ref, k_ref, v_ref, sg_ref, sf_ref, sb_ref, o_ref, sf_out, sb_out,
                     state_sc, part_sc, inner_sc, cross_sc, weight_sc):
    ch = RET_SCAN_CHUNK
    n_chunks = q_ref.shape[0] // ch
    dk, dv = q_ref.shape[1], v_ref.shape[1]
    head = pl.program_id(1)

    chunk_decay = []
    for d in range(2):
        lg = lg_ref[d, head]
        ii = lax.broadcasted_iota(jnp.int32, (ch, ch), 0)
        jj = lax.broadcasted_iota(jnp.int32, (ch, ch), 1)
        dist = (ii - jj) if d == 0 else (jj - ii)
        inner_sc[d] = jnp.where(dist >= 0, jnp.exp(jnp.maximum(dist, 0).astype(F32) * lg), 0.0)
        row_v = lax.broadcasted_iota(jnp.int32, (ch, dv), 0)
        row_k = lax.broadcasted_iota(jnp.int32, (ch, dk), 0)
        cross_sc[d] = jnp.exp(((row_v if d == 0 else ch - 1 - row_v) + 1).astype(F32) * lg)
        weight_sc[d] = jnp.exp((ch - 1 - (row_k if d == 0 else ch - 1 - row_k)).astype(F32) * lg)
        chunk_decay.append(jnp.exp(jnp.full((1, 1), ch, F32) * lg))
    state_sc[0] = sf_ref[0, 0]
    state_sc[1] = sb_ref[0, 0]

    def visit(d, c):
        rows = pl.ds(pl.multiple_of(c * ch, ch), ch)
        q, k, v = q_ref[rows, :], k_ref[rows, :], v_ref[rows, :]
        scores = lax.dot_general(q, k, (((1,), (1,)), ((), ())), preferred_element_type=F32) * inner_sc[d]
        state = state_sc[d]
        out = (jnp.dot(scores.astype(BF16), v, preferred_element_type=F32)
               + jnp.dot(q, state.astype(BF16), preferred_element_type=F32) * cross_sc[d])
        kw = (k.astype(F32) * weight_sc[d]).astype(BF16)
        state_sc[d] = state * chunk_decay[d] + lax.dot_general(
            kw, v, (((0,), (0,)), ((), ())), preferred_element_type=F32)
        return rows, out

    def finish(rows, o):
        o = o * lax.rsqrt(jnp.mean(o * o, axis=-1, keepdims=True) + RMS_EPS)
        o_ref[rows, :] = (sg_ref[rows, :].astype(F32) * o).astype(BF16)

    def first_visits(t, carry):
        for d, c in ((0, t), (1, n_chunks - 1 - t)):
            rows, out = visit(d, c)
            part_sc[rows, :] = out
        return carry

    def second_visits(t, carry):
        for d, c in ((0, t), (1, n_chunks - 1 - t)):
            rows, out = visit(d, c)
            finish(rows, out + part_sc[rows, :])
        return carry

    lax.fori_loop(0, n_chunks // 2, first_visits, 0)
    if n_chunks % 2 == 1:
        rows, out_f = visit(0, n_chunks // 2)
        _, out_b = visit(1, n_chunks // 2)
        finish(rows, out_f + out_b)
    lax.fori_loop((n_chunks + 1) // 2, n_chunks, second_visits, 0)
    sf_out[0, 0] = state_sc[0]
    sb_out[0, 0] = state_sc[1]


def _ret_scan_segment(rows, log_gamma, q, k, v, sg, state_f, state_b, *, seg_len, first_block):
    B = rows.B
    dk = q.shape[1] // RET_HEADS
    dv = v.shape[1] // RET_HEADS
    ch = RET_SCAN_CHUNK
    assert seg_len % ch == 0
    kspec = pl.BlockSpec((seg_len, dk), lambda b, h: (first_block + b, h))
    vspec = pl.BlockSpec((seg_len, dv), lambda b, h: (first_block + b, h))
    sspec = pl.BlockSpec((1, 1, dk, dv), lambda b, h: (b, h, 0, 0))
    state_shape = jax.ShapeDtypeStruct((B, RET_HEADS, dk, dv), F32)
    return pl.pallas_call(
        _ret_scan_kernel,
        grid=(B, RET_HEADS),
        in_specs=[pl.BlockSpec(memory_space=pltpu.SMEM), kspec, kspec, vspec, vspec, sspec, sspec],
        out_specs=[pl.BlockSpec((seg_len, dv), lambda b, h: (b, h)), sspec, sspec],
        out_shape=[jax.ShapeDtypeStruct((B * seg_len, v.shape[1]), BF16), state_shape, state_shape],
        scratch_shapes=[pltpu.VMEM((2, dk, dv), F32), pltpu.VMEM((seg_len, dv), F32),
                        pltpu.VMEM((2, ch, ch), F32), pltpu.VMEM((2, ch, dv), F32), pltpu.VMEM((2, ch, dk), F32)],
        compiler_params=_cparams("parallel", "parallel"),
        name="ret_scan",
    )(log_gamma, q, k, v, sg, state_f, state_b)


def _retention(rows, log_gamma, q, k, v, sg):
    B, S, C = rows.B, rows.S, rows.C
    assert rows.n_lat % C == 0
    dk, dv = q.shape[1] // RET_HEADS, v.shape[1] // RET_HEADS
    zeros = jnp.zeros((B, RET_HEADS, dk, dv), F32)
    o_ctx, state_f, state_b = _ret_scan_segment(rows, log_gamma, q, k, v, sg, zeros, zeros,
                                                seg_len=C, first_block=rows.n_lat // C)
    o_lat, _, _ = _ret_scan_segment(rows, log_gamma, q, k, v, sg, state_f, state_b, seg_len=S, first_block=0)
    return [o_lat, o_ctx]


def kernel(x, c, ctx, c_ctx, l0_ada_w, l0_ada_b, l0_ln1_g, l0_ln1_b, l0_ln2_g, l0_ln2_b, l0_attn_wqkv, l0_attn_q_norm, l0_attn_k_norm, l0_attn_wo, l0_ffn_w_gu, l0_ffn_w_down, l1_ada_w, l1_ada_b, l1_ln1_g, l1_ln1_b, l1_ln2_g, l1_ln2_b, l1_cmlp_w_in, l1_cmlp_b_in, l1_cmlp_v_norm_g, l1_cmlp_v_norm_b, l1_cmlp_w_s, l1_cmlp_b_s, l1_cmlp_w_out, l1_cmlp_b_out, l1_moe_router, l1_moe_w_gu, l1_moe_w_down, l2_ada_w, l2_ada_b, l2_ln1_g, l2_ln1_b, l2_ln2_g, l2_ln2_b, l2_ret_wqkvg, l2_ret_decay, l2_ret_wo, l2_ffn_w_gu, l2_ffn_w_down, l3_ada_w, l3_ada_b, l3_ln1_g, l3_ln1_b, l3_ln2_g, l3_ln2_b, l3_attn_wqkv, l3_attn_q_norm, l3_attn_k_norm, l3_attn_wo, l3_moe_router, l3_moe_w_gu, l3_moe_w_down):
    B, S, D = x.shape
    C = ctx.shape[1]
    rows = _Rows(B, S, C, D)
    assert S % GRID_W == 0 and C % RET_CHUNK == 0 and S % RET_CHUNK == 0

    xa = jnp.concatenate([x.reshape(rows.n_lat, D), ctx.reshape(rows.n_ctx, D)], axis=0)
    n_cond = -(-(B + 1) // 8) * 8
    cond = jnp.concatenate([c, c_ctx[None, :], jnp.zeros((n_cond - B - 1, D), F32)], axis=0)

    def modulation(ada_w, ada_b):
        return _ada(cond, ada_w, ada_b).reshape(n_cond * 6, 1, D)

    pad = rows.tile((512, 256))
    attn_cos, attn_sin = _rope_tables(S, ATTN_HEAD_DIM, pad)
    ret_cos, ret_sin = _rope_tables(S, D // RET_HEADS, pad)
    zero_bias = jnp.zeros((D,), F32)
    bf = lambda w: w.astype(BF16)

    def attention_layer(xa, mod, wqkv, qg, kg, wo, ln_g, ln_b, with_ctx):
        q, k, v = _attn_project(rows, xa, mod, bf(wqkv), qg, kg, attn_cos, attn_sin)
        o = _attention(rows, q, k, v, with_ctx)
        return _proj_ln(rows, [o], bf(wo), zero_bias, xa, mod, ln_g, ln_b)

    def retention_layer(xa, mod, wqkvg, decay, wo, ln_g, ln_b):
        log_gamma = -jnp.exp(decay.astype(F32))
        q, k, v, sg = _ret_project(rows, xa, mod, bf(wqkvg), ret_cos, ret_sin)
        o_segments = _retention(rows, log_gamma, q, k, v, sg)
        return _proj_ln(rows, o_segments, bf(wo), zero_bias, xa, mod, ln_g, ln_b)

    mod = modulation(l0_ada_w, l0_ada_b)
    xa = attention_layer(xa, mod, l0_attn_wqkv, l0_attn_q_norm, l0_attn_k_norm, l0_attn_wo, l0_ln1_g, l0_ln1_b, True)
    xa = _ffn_ln(rows, xa, mod, bf(l0_ffn_w_gu), bf(l0_ffn_w_down), l0_ln2_g, l0_ln2_b, rows.n)

    mod = modulation(l1_ada_w, l1_ada_b)
    xa = _gmlp_ln(rows, xa, mod, bf(l1_cmlp_w_in), l1_cmlp_b_in, l1_cmlp_v_norm_g, l1_cmlp_v_norm_b,
                  bf(l1_cmlp_w_s), l1_cmlp_b_s, bf(l1_cmlp_w_out), l1_cmlp_b_out, l1_ln1_g, l1_ln1_b, rows.n)
    xa = _moe_ln(rows, xa, mod, l1_moe_router, bf(l1_moe_w_gu), bf(l1_moe_w_down), l1_ln2_g, l1_ln2_b, rows.n)

    mod = modulation(l2_ada_w, l2_ada_b)
    xa = retention_layer(xa, mod, l2_ret_wqkvg, l2_ret_decay, l2_ret_wo, l2_ln1_g, l2_ln1_b)
    xa = _ffn_ln(rows, xa, mod, bf(l2_ffn_w_gu), bf(l2_ffn_w_down), l2_ln2_g, l2_ln2_b, rows.n)

    mod = modulation(l3_ada_w, l3_ada_b)
    xl = attention_layer(xa, mod, l3_attn_wqkv, l3_attn_q_norm, l3_attn_k_norm, l3_attn_wo, l3_ln1_g, l3_ln1_b, False)
    xl = _moe_ln(rows, xl, mod, l3_moe_router, bf(l3_moe_w_gu), bf(l3_moe_w_down), l3_ln2_g, l3_ln2_b, rows.n_lat)
    return xl.reshape(B, S, D)
```

```python
import functools

import jax
import jax.numpy as jnp
from jax import lax
from jax.experimental import pallas as pl
from jax.experimental.pallas import tpu as pltpu

F32 = jnp.float32
BF16 = jnp.bfloat16

DEPTH = 4
GRID_W = 64
ROPE_THETA = 10000.0
DEEPNORM_ALPHA = (2 * DEPTH) ** 0.25
LN_EPS = 1e-5
RMS_EPS = 1e-6
ATTN_HEAD_DIM = 128
ATTN_KV_HEADS = 2
CMLP_CHUNK = 128
CMLP_GROUPS = 8
RET_HEADS = 4
RET_CHUNK = 128
MOE_EXPERTS = 8

LANES = 128
VMEM_LIMIT_BYTES = 56 * 1024 * 1024
NEG_BIG = -1e30


def _cparams(*sem):
    return pltpu.CompilerParams(dimension_semantics=sem, vmem_limit_bytes=VMEM_LIMIT_BYTES)


def _resident(shape):
    nd = len(shape)
    return pl.BlockSpec(shape, lambda *_: (0,) * nd)


def _layer_norm(z, g, b):
    mu = jnp.mean(z, axis=-1, keepdims=True)
    zc = z - mu
    var = jnp.mean(zc * zc, axis=-1, keepdims=True)
    return zc * lax.rsqrt(var + LN_EPS) * g + b


def _pick_tile(candidates, *extents):
    for t in candidates:
        if all(e % t == 0 for e in extents):
            return t
    raise ValueError(f"no tile in {candidates} divides {extents}")


class _Rows:
    def __init__(self, batch, seq, ctx_len, dim):
        self.B, self.S, self.C, self.D = batch, seq, ctx_len, dim
        self.n_lat = batch * seq
        self.n_ctx = batch * ctx_len
        self.n = self.n_lat + self.n_ctx

    def tile(self, candidates):
        return _pick_tile(candidates, self.S, self.n_ctx)

    def mod_spec(self, part, tm):
        S, B = self.S, self.B
        return pl.BlockSpec((1, 1, self.D), lambda i, *_: (jnp.minimum(i * tm // S, B) * 6 + part, 0, 0))

    def pos_spec(self, tm, width):
        n_lat_tiles, per_seq = self.n_lat // tm, self.S // tm
        return pl.BlockSpec((tm, width), lambda i, *_: (jnp.where(i < n_lat_tiles, i % per_seq, per_seq), 0))


def _ada_kernel(c_ref, w_ref, b_ref, o_ref):
    c = c_ref[...]
    s = c * jax.nn.sigmoid(c)
    o_ref[...] = jnp.dot(s, w_ref[...], preferred_element_type=F32, precision=lax.Precision.HIGHEST) + b_ref[...]


def _ada(cond, w, b):
    r, d = cond.shape
    n = w.shape[1]
    tn = _pick_tile((1536, 1024, 512, 128), n)
    return pl.pallas_call(
        _ada_kernel,
        grid=(n // tn,),
        in_specs=[_resident((r, d)), pl.BlockSpec((d, tn), lambda j: (0, j)), pl.BlockSpec((1, tn), lambda j: (0, j))],
        out_specs=pl.BlockSpec((r, tn), lambda j: (0, j)),
        out_shape=jax.ShapeDtypeStruct((r, n), F32),
        compiler_params=_cparams("parallel"),
        name="ada",
    )(cond, w, b.reshape(1, n))


def _rope_tables(seq, head_dim, pad_rows):
    nf = head_dim // 4
    inv_freq = ROPE_THETA ** (-jnp.arange(nf, dtype=F32) / nf)
    t = jnp.arange(seq, dtype=jnp.int32)
    row = (t // GRID_W).astype(F32)[:, None] * inv_freq
    col = (t % GRID_W).astype(F32)[:, None] * inv_freq
    cos = jnp.concatenate([jnp.cos(row), jnp.cos(row), jnp.cos(col), jnp.cos(col)], axis=-1)
    sin = jnp.concatenate([-jnp.sin(row), jnp.sin(row), -jnp.sin(col), jnp.sin(col)], axis=-1)
    cos = jnp.concatenate([cos, jnp.ones((pad_rows, head_dim), F32)], axis=0)
    sin = jnp.concatenate([sin, jnp.zeros((pad_rows, head_dim), F32)], axis=0)
    return cos, sin


def _swap_pairs(y, half):
    if 2 * half == LANES:
        return pltpu.roll(y, half, 1)
    lane = lax.broadcasted_iota(jnp.int32, y.shape, 1)
    return jnp.where(lane % (2 * half) < half, pltpu.roll(y, LANES - half, 1), pltpu.roll(y, half, 1))


KEY_BLOCK = 256
LOG2_E = 1.4426950408889634


def _qkv_kernel(x_ref, sh_ref, sc_ref, w_ref, qg_ref, kg_ref, cos_ref, sin_ref, q_ref, kt_ref, v_ref, *, nq, nkv):
    dh = ATTN_HEAD_DIM
    h = (x_ref[...] * (1.0 + sc_ref[0]) + sh_ref[0]).astype(BF16)
    cos, sin = cos_ref[...], sin_ref[...]
    scale = dh ** -0.5 * LOG2_E

    def normed_rope(t, g):
        y = t * lax.rsqrt(jnp.mean(t * t, axis=-1, keepdims=True) + RMS_EPS) * g
        return y * cos + _swap_pairs(y, dh // 4) * sin

    t = jnp.dot(h, w_ref[...], preferred_element_type=F32)
    for hd in range(nq // dh):
        q_ref[:, hd * dh:(hd + 1) * dh] = (normed_rope(t[:, hd * dh:(hd + 1) * dh], qg_ref[...]) * scale).astype(BF16)
    k = jnp.concatenate([normed_rope(t[:, nq + hd * dh:nq + (hd + 1) * dh], kg_ref[...]) for hd in range(nkv // dh)],
                        axis=1)
    for j in range(kt_ref.shape[0]):
        kt_ref[j] = k[j * KEY_BLOCK:(j + 1) * KEY_BLOCK, :].T.astype(BF16)
    v_ref[...] = t[:, nq + nkv:].astype(BF16)


def _attn_project(rows, xa, mod, w, qg, kg, cos, sin):
    D = rows.D
    nkv = ATTN_KV_HEADS * ATTN_HEAD_DIM
    nq = w.shape[1] - 2 * nkv
    tm = rows.tile((512, 256))
    row_spec = lambda width: pl.BlockSpec((tm, width), lambda i: (i, 0))
    return pl.pallas_call(
        functools.partial(_qkv_kernel, nq=nq, nkv=nkv),
        grid=(rows.n // tm,),
        in_specs=[row_spec(D), rows.mod_spec(0, tm), rows.mod_spec(1, tm), _resident(w.shape),
                  _resident((1, ATTN_HEAD_DIM)), _resident((1, ATTN_HEAD_DIM)),
                  rows.pos_spec(tm, ATTN_HEAD_DIM), rows.pos_spec(tm, ATTN_HEAD_DIM)],
        out_specs=[row_spec(nq), pl.BlockSpec((tm // KEY_BLOCK, nkv, KEY_BLOCK), lambda i: (i, 0, 0)), row_spec(nkv)],
        out_shape=[jax.ShapeDtypeStruct((rows.n, nq), BF16),
                   jax.ShapeDtypeStruct((rows.n // KEY_BLOCK, nkv, KEY_BLOCK), BF16),
                   jax.ShapeDtypeStruct((rows.n, nkv), BF16)],
        compiler_params=_cparams("parallel"),
        name="attn_qkv",
    )(xa, mod, mod, w, qg.reshape(1, -1), kg.reshape(1, -1), cos, sin)


def _attn_kernel(q_ref, kc_ref, vc_ref, kl_ref, vl_ref, o_ref, m_sc, acc_sc, *, tq, blocks_per_chunk, n_lat_tiles):
    dh = ATTN_HEAD_DIM
    group = q_ref.shape[1] // dh // ATTN_KV_HEADS
    is_latent = pl.program_id(1) < n_lat_tiles
    n_lat_chunks = jnp.where(is_latent, kl_ref.shape[0] // blocks_per_chunk, 0)
    chunk = blocks_per_chunk * KEY_BLOCK

    heads = range(ATTN_KV_HEADS)
    head_rows = group * tq
    gs = [slice(g * dh, (g + 1) * dh) for g in heads]
    qs = [jnp.concatenate([q_ref[:, (g * group + j) * dh:(g * group + j + 1) * dh] for j in range(group)], axis=0)
          for g in heads]
    m_sc[...] = jnp.full(m_sc.shape, NEG_BIG, F32)
    acc_sc[...] = jnp.zeros(acc_sc.shape, F32)

    def online_softmax_step(kts, vs):
        width = kts[0].shape[1]
        s = jnp.concatenate([jnp.dot(qs[g], kts[g], preferred_element_type=F32) for g in heads], axis=0)
        m_old = m_sc[...]
        m_new = jnp.maximum(m_old, jnp.max(s, axis=-1, keepdims=True))
        p = jnp.exp2(s - jnp.concatenate([m_new] * (width // LANES), axis=1)).astype(BF16)
        alpha = jnp.exp2(m_old - m_new)
        ones = jnp.ones((width, dh), BF16)
        pv = jnp.concatenate([jnp.dot(p[g * head_rows:(g + 1) * head_rows], jnp.concatenate([vs[g], ones], axis=1),
                                      preferred_element_type=F32) for g in heads], axis=0)
        acc_sc[...] = jnp.concatenate([alpha, alpha], axis=1) * acc_sc[...] + pv
        m_sc[...] = m_new

    for c in range(kc_ref.shape[0]):
        online_softmax_step([kc_ref[c, gs[g], :] for g in heads],
                            [vc_ref[c * KEY_BLOCK:(c + 1) * KEY_BLOCK, gs[g]] for g in heads])

    def latent_chunk(c, carry):
        r0 = pl.multiple_of(c * chunk, chunk)
        online_softmax_step(
            [jnp.concatenate([kl_ref[c * blocks_per_chunk + j, gs[g], :] for j in range(blocks_per_chunk)], axis=1)
             for g in heads],
            [vl_ref[pl.ds(r0, chunk), gs[g]] for g in heads])
        return carry

    lax.fori_loop(0, n_lat_chunks, latent_chunk, 0)
    acc = acc_sc[...]
    o = acc[:, :dh] / acc[:, dh:]
    for hd in range(ATTN_KV_HEADS * group):
        o_ref[:, hd * dh:(hd + 1) * dh] = o[hd * tq:(hd + 1) * tq].astype(BF16)


def _attention(rows, q, kt, v, with_ctx):
    B, S, C = rows.B, rows.S, rows.C
    nq = q.shape[1]
    nkv = v.shape[1]
    assert S % KEY_BLOCK == 0 and C % KEY_BLOCK == 0 and rows.n_lat % C == 0
    tq = _pick_tile((256, 128), S, C)
    blocks_per_chunk = 2 if S % (2 * KEY_BLOCK) == 0 else 1
    lat_tiles, ctx_tiles = S // tq, C // tq
    tiles = lat_tiles + (ctx_tiles if with_ctx else 0)
    group = nq // nkv

    def q_map(b, i):
        return (jnp.where(i < lat_tiles, b * lat_tiles + i, rows.n_lat // tq + b * ctx_tiles + (i - lat_tiles)), 0)

    kt_ctx = pl.BlockSpec((C // KEY_BLOCK, nkv, KEY_BLOCK), lambda b, i: (rows.n_lat // C + b, 0, 0))
    kt_lat = pl.BlockSpec((S // KEY_BLOCK, nkv, KEY_BLOCK), lambda b, i: (b, 0, 0))
    v_ctx = pl.BlockSpec((C, nkv), lambda b, i: (rows.n_lat // C + b, 0))
    v_lat = pl.BlockSpec((S, nkv), lambda b, i: (b, 0))
    return pl.pallas_call(
        functools.partial(_attn_kernel, tq=tq, blocks_per_chunk=blocks_per_chunk, n_lat_tiles=lat_tiles),
        grid=(B, tiles),
        in_specs=[pl.BlockSpec((tq, nq), q_map), kt_ctx, v_ctx, kt_lat, v_lat],
        out_specs=pl.BlockSpec((tq, nq), q_map),
        out_shape=jax.ShapeDtypeStruct((rows.n if with_ctx else rows.n_lat, nq), BF16),
        scratch_shapes=[pltpu.VMEM((ATTN_KV_HEADS * group * tq, LANES), F32),
                        pltpu.VMEM((ATTN_KV_HEADS * group * tq, 2 * ATTN_HEAD_DIM), F32)],
        compiler_params=_cparams("parallel", "parallel"),
        name="attn_core",
    )(q, kt, v, kt, v)


def _proj_ln_kernel(*refs, first_tiles):
    n_seg = len(first_tiles) - 1
    o_refs = refs[:n_seg]
    w_ref, bias_ref, x_ref, gate_ref, g_ref, b_ref, out_ref = refs[n_seg:]
    i = pl.program_id(0)
    for seg, o_ref in enumerate(o_refs):
        @pl.when((i >= first_tiles[seg]) & (i < first_tiles[seg + 1]))
        def _():
            y = jnp.dot(o_ref[...], w_ref[...], preferred_element_type=F32) + bias_ref[...]
            out_ref[...] = _layer_norm(DEEPNORM_ALPHA * x_ref[...] + gate_ref[0] * y, g_ref[...], b_ref[...])


def _proj_ln(rows, o_segments, w, bias, xa, mod, ln_g, ln_b):
    D = rows.D
    kdim = w.shape[0]
    tm = rows.tile((512, 256, 128))
    first_tiles = [0]
    for o in o_segments:
        assert o.shape[0] % tm == 0
        first_tiles.append(first_tiles[-1] + o.shape[0] // tm)
    n_rows = first_tiles[-1] * tm

    def segment_spec(seg):
        first, count = first_tiles[seg], first_tiles[seg + 1] - first_tiles[seg]
        return pl.BlockSpec((tm, kdim), lambda i: (jnp.clip(i - first, 0, count - 1), 0))

    return pl.pallas_call(
        functools.partial(_proj_ln_kernel, first_tiles=tuple(first_tiles)),
        grid=(n_rows // tm,),
        in_specs=[segment_spec(seg) for seg in range(len(o_segments))] + [
            _resident(w.shape), _resident((1, D)), pl.BlockSpec((tm, D), lambda i: (i, 0)), rows.mod_spec(2, tm),
            _resident((1, D)), _resident((1, D))],
        out_specs=pl.BlockSpec((tm, D), lambda i: (i, 0)),
        out_shape=jax.ShapeDtypeStruct((n_rows, D), F32),
        compiler_params=_cparams("parallel"),
        name="proj_ln",
    )(*o_segments, w, bias.reshape(1, D), xa, mod, ln_g.reshape(1, D), ln_b.reshape(1, D))


def _ffn_kernel(x_ref, sh_ref, sc_ref, gate_ref, wg_ref, wu_ref, wd_ref, g_ref, b_ref, out_ref, h_sc, acc_sc):
    f = pl.program_id(1)

    @pl.when(f == 0)
    def _():
        h_sc[...] = (x_ref[...] * (1.0 + sc_ref[0]) + sh_ref[0]).astype(BF16)
        acc_sc[...] = jnp.zeros(acc_sc.shape, F32)

    h = h_sc[...]
    gt = jnp.dot(h, wg_ref[...], preferred_element_type=F32)
    up = jnp.dot(h, wu_ref[...], preferred_element_type=F32)
    act = (gt * jax.nn.sigmoid(gt) * up).astype(BF16)
    acc_sc[...] += jnp.dot(act, wd_ref[...], preferred_element_type=F32)

    @pl.when(f == pl.num_programs(1) - 1)
    def _():
        out_ref[...] = _layer_norm(DEEPNORM_ALPHA * x_ref[...] + gate_ref[0] * acc_sc[...], g_ref[...], b_ref[...])


def _ffn_ln(rows, xa, mod, w_gu, w_down, ln_g, ln_b, n_rows):
    D = rows.D
    F = w_down.shape[0]
    tm = rows.tile((512, 256, 128))
    tf = _pick_tile((1792, 512, 256, 128), F)
    nf = F // tf
    return pl.pallas_call(
        _ffn_kernel,
        grid=(n_rows // tm, nf),
        in_specs=[pl.BlockSpec((tm, D), lambda i, f: (i, 0)),
                  rows.mod_spec(3, tm), rows.mod_spec(4, tm), rows.mod_spec(5, tm),
                  pl.BlockSpec((D, tf), lambda i, f: (0, f)), pl.BlockSpec((D, tf), lambda i, f: (0, nf + f)),
                  pl.BlockSpec((tf, D), lambda i, f: (f, 0)), _resident((1, D)), _resident((1, D))],
        out_specs=pl.BlockSpec((tm, D), lambda i, f: (i, 0)),
        out_shape=jax.ShapeDtypeStruct((n_rows, D), F32),
        scratch_shapes=[pltpu.VMEM((tm, D), BF16), pltpu.VMEM((tm, D), F32)],
        compiler_params=_cparams("parallel", "arbitrary"),
        name="ffn_ln",
    )(xa, mod, mod, mod, w_gu, w_gu, w_down, ln_g.reshape(1, D), ln_b.reshape(1, D))


MOE_TOKEN_TILES = (2048, 1024, 512)
MOE_BIG_BLOCK = 512
MOE_SMALL_BLOCK = 128
MOE_MAX_SMALL_BLOCKS = 2


def _router_kernel(x_ref, sh_ref, sc_ref, r_ref, h_ref, gates_ref, *, n_experts, n_real_tiles):
    @pl.when(pl.program_id(0) >= n_real_tiles)
    def _():
        h_ref[...] = jnp.zeros(h_ref.shape, BF16)
        gates_ref[...] = jnp.zeros(gates_ref.shape, F32)

    @pl.when(pl.program_id(0) < n_real_tiles)
    def _():
        _route_rows(x_ref, sh_ref, sc_ref, r_ref, h_ref, gates_ref, n_experts)


def _route_rows(x_ref, sh_ref, sc_ref, r_ref, h_ref, gates_ref, n_experts):
    h = x_ref[...] * (1.0 + sc_ref[0]) + sh_ref[0]
    h_ref[...] = h.astype(BF16)
    logits = jnp.dot(h, r_ref[...], preferred_element_type=F32, precision=lax.Precision.HIGHEST)
    lane = lax.broadcasted_iota(jnp.int32, logits.shape, 1)
    lowest = jnp.finfo(F32).min
    lg = jnp.where(lane < n_experts, logits, lowest)
    m1 = jnp.max(lg, axis=-1, keepdims=True)
    i1 = jnp.min(jnp.where(lg == m1, lane, LANES), axis=-1, keepdims=True)
    lg2 = jnp.where(lane == i1, lowest, lg)
    m2 = jnp.max(lg2, axis=-1, keepdims=True)
    i2 = jnp.min(jnp.where(lg2 == m2, lane, LANES), axis=-1, keepdims=True)
    e2 = jnp.exp(m2 - m1)
    denom = 1.0 + e2
    gates = jnp.where(lane == i1, 1.0 / denom, 0.0) + jnp.where(lane == i2, e2 / denom, 0.0)
    gates_ref[...] = gates.T[:n_experts, :]


def _router(rows, xa, mod, router, n_rows, n_padded):
    D = rows.D
    n_experts = router.shape[1]
    tm = rows.tile((512, 256, 128))
    assert n_padded % tm == 0
    n_real_tiles = n_rows // tm
    router_pad = jnp.pad(router, ((0, 0), (0, LANES - n_experts)))
    return pl.pallas_call(
        functools.partial(_router_kernel, n_experts=n_experts, n_real_tiles=n_real_tiles),
        grid=(n_padded // tm,),
        in_specs=[pl.BlockSpec((tm, D), lambda i: (jnp.minimum(i, n_real_tiles - 1), 0)),
                  rows.mod_spec(3, tm), rows.mod_spec(4, tm), _resident((D, LANES))],
        out_specs=[pl.BlockSpec((tm, D), lambda i: (i, 0)), pl.BlockSpec((n_experts, tm), lambda i: (0, i))],
        out_shape=[jax.ShapeDtypeStruct((n_padded, D), BF16), jax.ShapeDtypeStruct((n_experts, n_padded), F32)],
        compiler_params=_cparams("parallel"),
        name="moe_router",
    )(xa, mod, mod, router_pad)


def _plan_kernel(g_ref, rank_ref, cnt_ref):
    n_experts, tile = g_ref.shape
    routed = g_ref[...] > 0.0
    ones = jnp.where(routed, 1.0, 0.0)
    before = jnp.where(lax.broadcasted_iota(jnp.int32, (LANES, LANES), 0)
                       < lax.broadcasted_iota(jnp.int32, (LANES, LANES), 1), 1.0, 0.0)
    seen = jnp.zeros((n_experts, 1), F32)
    for c in range(tile // LANES):
        cs = slice(c * LANES, (c + 1) * LANES)
        rank = jnp.dot(ones[:, cs], before, preferred_element_type=F32) + seen
        rank_ref[:, cs] = jnp.where(routed[:, cs], rank, -1.0).astype(jnp.int32)
        seen = seen + jnp.sum(ones[:, cs], axis=1, keepdims=True)
    cnt_ref[0] = jnp.broadcast_to(seen, (n_experts, LANES)).astype(jnp.int32)


def _moe_plan(gates_t, tile):
    n_experts, n_rows = gates_t.shape
    n_tiles = n_rows // tile
    rank, counts = pl.pallas_call(
        _plan_kernel,
        grid=(n_tiles,),
        in_specs=[pl.BlockSpec((n_experts, tile), lambda s: (0, s))],
        out_specs=[pl.BlockSpec((n_experts, tile), lambda s: (0, s)),
                   pl.BlockSpec((1, n_experts, LANES), lambda s: (s, 0, 0))],
        out_shape=[jax.ShapeDtypeStruct((n_experts, n_rows), jnp.int32),
                   jax.ShapeDtypeStruct((n_tiles, n_experts, LANES), jnp.int32)],
        compiler_params=_cparams("parallel"),
        name="moe_plan",
    )(gates_t)
    count = counts[:, :, 0].reshape(-1)
    n_big = count // MOE_BIG_BLOCK
    rest = count - n_big * MOE_BIG_BLOCK
    round_up = rest > MOE_MAX_SMALL_BLOCKS * MOE_SMALL_BLOCK
    n_big = n_big + round_up.astype(jnp.int32)
    n_small = jnp.where(round_up, 0, (rest + MOE_SMALL_BLOCK - 1) // MOE_SMALL_BLOCK)
    return rank, n_big, n_small


def _moe_kernel(nbig_ref, nsmall_ref, h_ref, rank_ref, gate_ref, wg_ref, wu_ref, wd_ref, out_ref, xg_sc, y_sc):
    s, e, f = pl.program_id(0), pl.program_id(1), pl.program_id(2)
    tile = h_ref.shape[0]
    pair = s * pl.num_programs(1) + e
    n_big, n_small = nbig_ref[pair], nsmall_ref[pair]
    big = MOE_BIG_BLOCK

    @pl.when((e == 0) & (f == 0))
    def _():
        out_ref[...] = jnp.zeros(out_ref.shape, F32)

    def for_each_block(fn):
        def big_block(r, carry):
            fn(r * big, big)
            return carry

        def small_block(r, carry):
            fn(n_big * big + r * MOE_SMALL_BLOCK, MOE_SMALL_BLOCK)
            return carry

        lax.fori_loop(0, n_big, big_block, 0)
        lax.fori_loop(0, n_small, small_block, 0)

    def rows_at(start, size):
        return pl.ds(pl.multiple_of(start, MOE_SMALL_BLOCK), size)

    def slot_matches(start, size):
        slot = lax.broadcasted_iota(jnp.int32, (size, tile), 0) + start
        return slot == rank_ref[pl.ds(e, 1), :]

    def gather(start, size):
        onehot = jnp.where(slot_matches(start, size), 1.0, 0.0).astype(BF16)
        xg_sc[rows_at(start, size), :] = jnp.dot(onehot, h_ref[...], preferred_element_type=F32).astype(BF16)
        y_sc[rows_at(start, size), :] = jnp.zeros((size, y_sc.shape[1]), F32)

    def expert(start, size):
        xb = xg_sc[rows_at(start, size), :]
        gt = jnp.dot(xb, wg_ref[0], preferred_element_type=F32)
        up = jnp.dot(xb, wu_ref[0], preferred_element_type=F32)
        act = (gt * jax.nn.sigmoid(gt) * up).astype(BF16)
        y_sc[rows_at(start, size), :] += jnp.dot(act, wd_ref[0], preferred_element_type=F32)

    def scatter(start, size):
        weighted = jnp.where(slot_matches(start, size), gate_ref[pl.ds(e, 1), :], 0.0).astype(BF16)
        out_ref[...] += lax.dot_general(weighted, y_sc[rows_at(start, size), :].astype(BF16),
                                        (((0,), (0,)), ((), ())), preferred_element_type=F32)

    @pl.when(f == 0)
    def _():
        for_each_block(gather)

    for_each_block(expert)

    @pl.when(f == pl.num_programs(2) - 1)
    def _():
        for_each_block(scatter)


def _moe(h, gates_t, w_gu, w_down, tile):
    n_rows, D = h.shape
    E, F = w_down.shape[0], w_down.shape[1]
    assert n_rows % tile == 0
    tf = _pick_tile((512, 256, 128), F)
    nf = F // tf
    rank, n_big, n_small = _moe_plan(gates_t, tile)
    max_rows = tile + max(MOE_BIG_BLOCK - MOE_MAX_SMALL_BLOCKS * MOE_SMALL_BLOCK, MOE_SMALL_BLOCK)
    grid_spec = pltpu.PrefetchScalarGridSpec(
        num_scalar_prefetch=2,
        grid=(n_rows // tile, E, nf),
        in_specs=[pl.BlockSpec((tile, D), lambda s, e, f, *_: (s, 0)),
                  pl.BlockSpec((E, tile), lambda s, e, f, *_: (0, s)),
                  pl.BlockSpec((E, tile), lambda s, e, f, *_: (0, s)),
                  pl.BlockSpec((1, D, tf), lambda s, e, f, *_: (e, 0, f)),
                  pl.BlockSpec((1, D, tf), lambda s, e, f, *_: (e, 0, nf + f)),
                  pl.BlockSpec((1, tf, D), lambda s, e, f, *_: (e, f, 0))],
        out_specs=pl.BlockSpec((tile, D), lambda s, e, f, *_: (s, 0)),
        scratch_shapes=[pltpu.VMEM((max_rows, D), BF16), pltpu.VMEM((max_rows, D), F32)],
    )
    return pl.pallas_call(
        _moe_kernel,
        grid_spec=grid_spec,
        out_shape=jax.ShapeDtypeStruct((n_rows, D), F32),
        compiler_params=_cparams("parallel", "arbitrary", "arbitrary"),
        name="moe_experts",
    )(n_big, n_small, h, rank, gates_t, w_gu, w_gu, w_down)


def _residual_ln_kernel(x_ref, f_ref, gate_ref, g_ref, b_ref, out_ref):
    out_ref[...] = _layer_norm(DEEPNORM_ALPHA * x_ref[...] + gate_ref[0] * f_ref[...], g_ref[...], b_ref[...])


def _residual_ln(rows, xa, f, mod, part, ln_g, ln_b, n_rows):
    D = rows.D
    tm = rows.tile((512, 256, 128))
    return pl.pallas_call(
        _residual_ln_kernel,
        grid=(n_rows // tm,),
        in_specs=[pl.BlockSpec((tm, D), lambda i: (i, 0)), pl.BlockSpec((tm, D), lambda i: (i, 0)),
                  rows.mod_spec(part, tm), _resident((1, D)), _resident((1, D))],
        out_specs=pl.BlockSpec((tm, D), lambda i: (i, 0)),
        out_shape=jax.ShapeDtypeStruct((n_rows, D), F32),
        compiler_params=_cparams("parallel"),
        name="residual_ln",
    )(xa, f, mod, ln_g.reshape(1, D), ln_b.reshape(1, D))


def _moe_ln(rows, xa, mod, router, w_gu, w_down, ln_g, ln_b, n_rows):
    tile = next(t for t in MOE_TOKEN_TILES if t <= n_rows)
    n_padded = -(-n_rows // tile) * tile
    h, gates_t = _router(rows, xa, mod, router, n_rows, n_padded)
    return _residual_ln(rows, xa, _moe(h, gates_t, w_gu, w_down, tile), mod, 5, ln_g, ln_b, n_rows)


def _gmlp_kernel(x_ref, sh_ref, sc_ref, gate_ref, win_ref, bin_ref, vg_ref, vb_ref, ws_ref, bs_ref, wout_ref,
                 bout_ref, g_ref, b_ref, out_ref, gated_sc):
    tm = x_ref.shape[0]
    inner = wout_ref.shape[0]
    gdim = inner // CMLP_GROUPS
    x = x_ref[...]
    h = (x * (1.0 + sc_ref[0]) + sh_ref[0]).astype(BF16)
    v = jax.nn.gelu(jnp.dot(h, win_ref[:, inner:], preferred_element_type=F32) + bin_ref[:, inner:])
    v = _layer_norm(v, vg_ref[...], vb_ref[...]).astype(BF16)
    for gi in range(CMLP_GROUPS):
        cs = slice(gi * gdim, (gi + 1) * gdim)
        u = jax.nn.gelu(jnp.dot(h, win_ref[:, cs], preferred_element_type=F32) + bin_ref[:, cs])
        w_s = ws_ref[gi]
        bias = bs_ref[:, gi:gi + 1]
        for c in range(tm // CMLP_CHUNK):
            rs = slice(c * CMLP_CHUNK, (c + 1) * CMLP_CHUNK)
            mixed = jnp.dot(w_s, v[rs, cs], preferred_element_type=F32) + bias
            gated_sc[rs, cs] = (u[rs] * mixed).astype(BF16)
    y = jnp.dot(gated_sc[...], wout_ref[...], preferred_element_type=F32) + bout_ref[...]
    out_ref[...] = _layer_norm(DEEPNORM_ALPHA * x + gate_ref[0] * y, g_ref[...], b_ref[...])


def _gmlp_ln(rows, xa, mod, w_in, b_in, vg, vb, w_s, b_s, w_out, b_out, ln_g, ln_b, n_rows):
    D = rows.D
    inner = w_out.shape[0]
    tm = rows.tile((512, 256, 128))
    return pl.pallas_call(
        _gmlp_kernel,
        grid=(n_rows // tm,),
        in_specs=[pl.BlockSpec((tm, D), lambda i: (i, 0)), rows.mod_spec(0, tm), rows.mod_spec(1, tm),
                  rows.mod_spec(2, tm), _resident(w_in.shape), _resident((1, 2 * inner)), _resident((1, inner)),
                  _resident((1, inner)), _resident(w_s.shape), _resident((CMLP_CHUNK, CMLP_GROUPS)),
                  _resident(w_out.shape), _resident((1, D)), _resident((1, D)), _resident((1, D))],
        out_specs=pl.BlockSpec((tm, D), lambda i: (i, 0)),
        out_shape=jax.ShapeDtypeStruct((n_rows, D), F32),
        scratch_shapes=[pltpu.VMEM((tm, inner), BF16)],
        compiler_params=_cparams("parallel"),
        name="gmlp_ln",
    )(xa, mod, mod, mod, w_in, b_in.reshape(1, -1), vg.reshape(1, -1), vb.reshape(1, -1), w_s, b_s.T, w_out,
      b_out.reshape(1, D), ln_g.reshape(1, D), ln_b.reshape(1, D))


def _ret_proj_kernel(x_ref, sh_ref, sc_ref, w_ref, cos_ref, sin_ref, q_ref, k_ref, v_ref, sg_ref, *, nk, nv, dk):
    h = (x_ref[...] * (1.0 + sc_ref[0]) + sh_ref[0]).astype(BF16)
    cos, sin = cos_ref[...], sin_ref[...]
    k_scale = dk ** -0.5

    def rope_store(dst_ref, col0, scale):
        full = jnp.dot(h, w_ref[:, col0:col0 + nk], preferred_element_type=F32) * scale
        for j in range(nk // LANES):
            t = full[:, j * LANES:(j + 1) * LANES]
            ts = slice((j % (dk // LANES)) * LANES, (j % (dk // LANES) + 1) * LANES)
            dst_ref[:, j * LANES:(j + 1) * LANES] = (t * cos[:, ts] + _swap_pairs(t, dk // 4) * sin[:, ts]).astype(BF16)

    rope_store(q_ref, 0, 1.0)
    rope_store(k_ref, nk, k_scale)
    blk = 512
    for j in range(nv // blk):
        v_ref[:, j * blk:(j + 1) * blk] = jnp.dot(
            h, w_ref[:, 2 * nk + j * blk:2 * nk + (j + 1) * blk], preferred_element_type=F32).astype(BF16)
        gt = jnp.dot(h, w_ref[:, 2 * nk + nv + j * blk:2 * nk + nv + (j + 1) * blk], preferred_element_type=F32)
        sg_ref[:, j * blk:(j + 1) * blk] = (gt * jax.nn.sigmoid(gt)).astype(BF16)


def _ret_project(rows, xa, mod, w, cos, sin):
    D = rows.D
    nk = D
    nv = (w.shape[1] - 2 * nk) // 2
    dk = nk // RET_HEADS
    tm = rows.tile((512, 256))
    row_spec = lambda width: pl.BlockSpec((tm, width), lambda i: (i, 0))
    return pl.pallas_call(
        functools.partial(_ret_proj_kernel, nk=nk, nv=nv, dk=dk),
        grid=(rows.n // tm,),
        in_specs=[row_spec(D), rows.mod_spec(0, tm), rows.mod_spec(1, tm), _resident(w.shape),
                  rows.pos_spec(tm, dk), rows.pos_spec(tm, dk)],
        out_specs=[row_spec(nk), row_spec(nk), row_spec(nv), row_spec(nv)],
        out_shape=[jax.ShapeDtypeStruct((rows.n, nk), BF16), jax.ShapeDtypeStruct((rows.n, nk), BF16),
                   jax.ShapeDtypeStruct((rows.n, nv), BF16), jax.ShapeDtypeStruct((rows.n, nv), BF16)],
        compiler_params=_cparams("parallel"),
        name="ret_proj",
    )(xa, mod, mod, w, cos, sin)


RET_SCAN_CHUNK = 256


def _ret_scan_kernel(lg_ref, q_ref, k_ref, v_ref, sg_ref, sf_ref, sb_ref, o_ref, sf_out, sb_out,
                     state_sc, part_sc, inner_sc, cross_sc, weight_sc):
    ch = RET_SCAN_CHUNK
    n_chunks = q_ref.shape[0] // ch
    dk, dv = q_ref.shape[1], v_ref.shape[1]
    head = pl.program_id(1)

    chunk_decay = []
    for d in range(2):
        lg = lg_ref[d, head]
        ii = lax.broadcasted_iota(jnp.int32, (ch, ch), 0)
        jj = lax.broadcasted_iota(jnp.int32, (ch, ch), 1)
        dist = (ii - jj) if d == 0 else (jj - ii)
        inner_sc[d] = jnp.where(dist >= 0, jnp.exp(jnp.maximum(dist, 0).astype(F32) * lg), 0.0)
        row_v = lax.broadcasted_iota(jnp.int32, (ch, dv), 0)
        row_k = lax.broadcasted_iota(jnp.int32, (ch, dk), 0)
        cross_sc[d] = jnp.exp(((row_v if d == 0 else ch - 1 - row_v) + 1).astype(F32) * lg)
        weight_sc[d] = jnp.exp((ch - 1 - (row_k if d == 0 else ch - 1 - row_k)).astype(F32) * lg)
        chunk_decay.append(jnp.exp(jnp.full((1, 1), ch, F32) * lg))
    state_sc[0] = sf_ref[0, 0]
    state_sc[1] = sb_ref[0, 0]

    def visit(d, c):
        rows = pl.ds(pl.multiple_of(c * ch, ch), ch)
        q, k, v = q_ref[rows, :], k_ref[rows, :], v_ref[rows, :]
        scores = lax.dot_general(q, k, (((1,), (1,)), ((), ())), preferred_element_type=F32) * inner_sc[d]
        state = state_sc[d]
        out = (jnp.dot(scores.astype(BF16), v, preferred_element_type=F32)
               + jnp.dot(q, state.astype(BF16), preferred_element_type=F32) * cross_sc[d])
        kw = (k.astype(F32) * weight_sc[d]).astype(BF16)
        state_sc[d] = state * chunk_decay[d] + lax.dot_general(
            kw, v, (((0,), (0,)), ((), ())), preferred_element_type=F32)
        return rows, out

    def finish(rows, o):
        o = o * lax.rsqrt(jnp.mean(o * o, axis=-1, keepdims=True) + RMS_EPS)
        o_ref[rows, :] = (sg_ref[rows, :].astype(F32) * o).astype(BF16)

    def first_visits(t, carry):
        for d, c in ((0, t), (1, n_chunks - 1 - t)):
            rows, out = visit(d, c)
            part_sc[rows, :] = out
        return carry

    def second_visits(t, carry):
        for d, c in ((0, t), (1, n_chunks - 1 - t)):
            rows, out = visit(d, c)
            finish(rows, out + part_sc[rows, :])
        return carry

    lax.fori_loop(0, n_chunks // 2, first_visits, 0)
    if n_chunks % 2 == 1:
        rows, out_f = visit(0, n_chunks // 2)
        _, out_b = visit(1, n_chunks // 2)
        finish(rows, out_f + out_b)
    lax.fori_loop((n_chunks + 1) // 2, n_chunks, second_visits, 0)
    sf_out[0, 0] = state_sc[0]
    sb_out[0, 0] = state_sc[1]


def _ret_scan_segment(rows, log_gamma, q, k, v, sg, state_f, state_b, *, seg_len, first_block):
    B = rows.B
    dk = q.shape[1] // RET_HEADS
    dv = v.shape[1] // RET_HEADS
    ch = RET_SCAN_CHUNK
    assert seg_len % ch == 0
    kspec = pl.BlockSpec((seg_len, dk), lambda b, h: (first_block + b, h))
    vspec = pl.BlockSpec((seg_len, dv), lambda b, h: (first_block + b, h))
    sspec = pl.BlockSpec((1, 1, dk, dv), lambda b, h: (b, h, 0, 0))
    state_shape = jax.ShapeDtypeStruct((B, RET_HEADS, dk, dv), F32)
    return pl.pallas_call(
        _ret_scan_kernel,
        grid=(B, RET_HEADS),
        in_specs=[pl.BlockSpec(memory_space=pltpu.SMEM), kspec, kspec, vspec, vspec, sspec, sspec],
        out_specs=[pl.BlockSpec((seg_len, dv), lambda b, h: (b, h)), sspec, sspec],
        out_shape=[jax.ShapeDtypeStruct((B * seg_len, v.shape[1]), BF16), state_shape, state_shape],
        scratch_shapes=[pltpu.VMEM((2, dk, dv), F32), pltpu.VMEM((seg_len, dv), F32),
                        pltpu.VMEM((2, ch, ch), F32), pltpu.VMEM((2, ch, dv), F32), pltpu.VMEM((2, ch, dk), F32)],
        compiler_params=_cparams("parallel", "parallel"),
        name="ret_scan",
    )(log_gamma, q, k, v, sg, state_f, state_b)


def _retention(rows, log_gamma, q, k, v, sg):
    B, S, C = rows.B, rows.S, rows.C
    assert rows.n_lat % C == 0
    dk, dv = q.shape[1] // RET_HEADS, v.shape[1] // RET_HEADS
    zeros = jnp.zeros((B, RET_HEADS, dk, dv), F32)
    o_ctx, state_f, state_b = _ret_scan_segment(rows, log_gamma, q, k, v, sg, zeros, zeros,
                                                seg_len=C, first_block=rows.n_lat // C)
    o_lat, _, _ = _ret_scan_segment(rows, log_gamma, q, k, v, sg, state_f, state_b, seg_len=S, first_block=0)
    return [o_lat, o_ctx]


def kernel(x, c, ctx, c_ctx, l0_ada_w, l0_ada_b, l0_ln1_g, l0_ln1_b, l0_ln2_g, l0_ln2_b, l0_attn_wqkv, l0_attn_q_norm, l0_attn_k_norm, l0_attn_wo, l0_ffn_w_gu, l0_ffn_w_down, l1_ada_w, l1_ada_b, l1_ln1_g, l1_ln1_b, l1_ln2_g, l1_ln2_b, l1_cmlp_w_in, l1_cmlp_b_in, l1_cmlp_v_norm_g, l1_cmlp_v_norm_b, l1_cmlp_w_s, l1_cmlp_b_s, l1_cmlp_w_out, l1_cmlp_b_out, l1_moe_router, l1_moe_w_gu, l1_moe_w_down, l2_ada_w, l2_ada_b, l2_ln1_g, l2_ln1_b, l2_ln2_g, l2_ln2_b, l2_ret_wqkvg, l2_ret_decay, l2_ret_wo, l2_ffn_w_gu, l2_ffn_w_down, l3_ada_w, l3_ada_b, l3_ln1_g, l3_ln1_b, l3_ln2_g, l3_ln2_b, l3_attn_wqkv, l3_attn_q_norm, l3_attn_k_norm, l3_attn_wo, l3_moe_router, l3_moe_w_gu, l3_moe_w_down):
    B, S, D = x.shape
    C = ctx.shape[1]
    rows = _Rows(B, S, C, D)
    assert S % GRID_W == 0 and C % RET_CHUNK == 0 and S % RET_CHUNK == 0

    xa = jnp.concatenate([x.reshape(rows.n_lat, D), ctx.reshape(rows.n_ctx, D)], axis=0)
    n_cond = -(-(B + 1) // 8) * 8
    cond = jnp.concatenate([c, c_ctx[None, :], jnp.zeros((n_cond - B - 1, D), F32)], axis=0)

    def modulation(ada_w, ada_b):
        return _ada(cond, ada_w, ada_b).reshape(n_cond * 6, 1, D)

    pad = rows.tile((512, 256))
    attn_cos, attn_sin = _rope_tables(S, ATTN_HEAD_DIM, pad)
    ret_cos, ret_sin = _rope_tables(S, D // RET_HEADS, pad)
    zero_bias = jnp.zeros((D,), F32)
    bf = lambda w: w.astype(BF16)

    def attention_layer(xa, mod, wqkv, qg, kg, wo, ln_g, ln_b, with_ctx):
        q, k, v = _attn_project(rows, xa, mod, bf(wqkv), qg, kg, attn_cos, attn_sin)
        o = _attention(rows, q, k, v, with_ctx)
        return _proj_ln(rows, [o], bf(wo), zero_bias, xa, mod, ln_g, ln_b)

    def retention_layer(xa, mod, wqkvg, decay, wo, ln_g, ln_b):
        log_gamma = -jnp.exp(decay.astype(F32))
        q, k, v, sg = _ret_project(rows, xa, mod, bf(wqkvg), ret_cos, ret_sin)
        o_segments = _retention(rows, log_gamma, q, k, v, sg)
        return _proj_ln(rows, o_segments, bf(wo), zero_bias, xa, mod, ln_g, ln_b)

    mod = modulation(l0_ada_w, l0_ada_b)
    xa = attention_layer(xa, mod, l0_attn_wqkv, l0_attn_q_norm, l0_attn_k_norm, l0_attn_wo, l0_ln1_g, l0_ln1_b, True)
    xa = _ffn_ln(rows, xa, mod, bf(l0_ffn_w_gu), bf(l0_ffn_w_down), l0_ln2_g, l0_ln2_b, rows.n)

    mod = modulation(l1_ada_w, l1_ada_b)
    xa = _gmlp_ln(rows, xa, mod, bf(l1_cmlp_w_in), l1_cmlp_b_in, l1_cmlp_v_norm_g, l1_cmlp_v_norm_b,
                  bf(l1_cmlp_w_s), l1_cmlp_b_s, bf(l1_cmlp_w_out), l1_cmlp_b_out, l1_ln1_g, l1_ln1_b, rows.n)
    xa = _moe_ln(rows, xa, mod, l1_moe_router, bf(l1_moe_w_gu), bf(l1_moe_w_down), l1_ln2_g, l1_ln2_b, rows.n)

    mod = modulation(l2_ada_w, l2_ada_b)
    xa = retention_layer(xa, mod, l2_ret_wqkvg, l2_ret_decay, l2_ret_wo, l2_ln1_g, l2_ln1_b)
    xa = _ffn_ln(rows, xa, mod, bf(l2_ffn_w_gu), bf(l2_ffn_w_down), l2_ln2_g, l2_ln2_b, rows.n)

    mod = modulation(l3_ada_w, l3_ada_b)
    xl = attention_layer(xa, mod, l3_attn_wqkv, l3_attn_q_norm, l3_attn_k_norm, l3_attn_wo, l3_ln1_g, l3_ln1_b, False)
    xl = _moe_ln(rows, xl, mod, l3_moe_router, bf(l3_moe_w_gu), bf(l3_moe_w_down), l3_ln2_g, l3_ln2_b, rows.n_lat)
    return xl.reshape(B, S, D)
```

```python
import functools

import jax
import jax.numpy as jnp
from jax import lax
from jax.experimental import pallas as pl
from jax.experimental.pallas import tpu as pltpu

F32 = jnp.float32
BF16 = jnp.bfloat16

DEPTH = 4
GRID_W = 64
ROPE_THETA = 10000.0
DEEPNORM_ALPHA = (2 * DEPTH) ** 0.25
LN_EPS = 1e-5
RMS_EPS = 1e-6
ATTN_HEAD_DIM = 128
ATTN_KV_HEADS = 2
CMLP_CHUNK = 128
CMLP_GROUPS = 8
RET_HEADS = 4
RET_CHUNK = 128
MOE_EXPERTS = 8

LANES = 128
VMEM_LIMIT_BYTES = 56 * 1024 * 1024
NEG_BIG = -1e30


def _cparams(*sem):
    return pltpu.CompilerParams(dimension_semantics=sem, vmem_limit_bytes=VMEM_LIMIT_BYTES)


def _resident(shape):
    nd = len(shape)
    return pl.BlockSpec(shape, lambda *_: (0,) * nd)


def _layer_norm(z, g, b):
    mu = jnp.mean(z, axis=-1, keepdims=True)
    zc = z - mu
    var = jnp.mean(zc * zc, axis=-1, keepdims=True)
    return zc * lax.rsqrt(var + LN_EPS) * g + b


def _pick_tile(candidates, *extents):
    for t in candidates:
        if all(e % t == 0 for e in extents):
            return t
    raise ValueError(f"no tile in {candidates} divides {extents}")


class _Rows:
    def __init__(self, batch, seq, ctx_len, dim):
        self.B, self.S, self.C, self.D = batch, seq, ctx_len, dim
        self.n_lat = batch * seq
        self.n_ctx = batch * ctx_len
        self.n = self.n_lat + self.n_ctx

    def tile(self, candidates):
        return _pick_tile(candidates, self.S, self.n_ctx)

    def mod_spec(self, part, tm):
        S, B = self.S, self.B
        return pl.BlockSpec((1, 1, self.D), lambda i, *_: (jnp.minimum(i * tm // S, B) * 6 + part, 0, 0))

    def pos_spec(self, tm, width):
        n_lat_tiles, per_seq = self.n_lat // tm, self.S // tm
        return pl.BlockSpec((tm, width), lambda i, *_: (jnp.where(i < n_lat_tiles, i % per_seq, per_seq), 0))


def _ada_kernel(c_ref, w_ref, b_ref, o_ref):
    c = c_ref[...]
    s = c * jax.nn.sigmoid(c)
    o_ref[...] = jnp.dot(s, w_ref[...], preferred_element_type=F32, precision=lax.Precision.HIGHEST) + b_ref[...]


def _ada(cond, w, b):
    r, d = cond.shape
    n = w.shape[1]
    tn = _pick_tile((1536, 1024, 512, 128), n)
    return pl.pallas_call(
        _ada_kernel,
        grid=(n // tn,),
        in_specs=[_resident((r, d)), pl.BlockSpec((d, tn), lambda j: (0, j)), pl.BlockSpec((1, tn), lambda j: (0, j))],
        out_specs=pl.BlockSpec((r, tn), lambda j: (0, j)),
        out_shape=jax.ShapeDtypeStruct((r, n), F32),
        compiler_params=_cparams("parallel"),
        name="ada",
    )(cond, w, b.reshape(1, n))


def _rope_tables(seq, head_dim, pad_rows):
    nf = head_dim // 4
    inv_freq = ROPE_THETA ** (-jnp.arange(nf, dtype=F32) / nf)
    t = jnp.arange(seq, dtype=jnp.int32)
    row = (t // GRID_W).astype(F32)[:, None] * inv_freq
    col = (t % GRID_W).astype(F32)[:, None] * inv_freq
    cos = jnp.concatenate([jnp.cos(row), jnp.cos(row), jnp.cos(col), jnp.cos(col)], axis=-1)
    sin = jnp.concatenate([-jnp.sin(row), jnp.sin(row), -jnp.sin(col), jnp.sin(col)], axis=-1)
    cos = jnp.concatenate([cos, jnp.ones((pad_rows, head_dim), F32)], axis=0)
    sin = jnp.concatenate([sin, jnp.zeros((pad_rows, head_dim), F32)], axis=0)
    return cos, sin


def _swap_pairs(y, half):
    if 2 * half == LANES:
        return pltpu.roll(y, half, 1)
    lane = lax.broadcasted_iota(jnp.int32, y.shape, 1)
    return jnp.where(lane % (2 * half) < half, pltpu.roll(y, LANES - half, 1), pltpu.roll(y, half, 1))


KEY_BLOCK = 256
LOG2_E = 1.4426950408889634


def _qkv_kernel(x_ref, sh_ref, sc_ref, w_ref, qg_ref, kg_ref, cos_ref, sin_ref, q_ref, kt_ref, v_ref, *, nq, nkv):
    dh = ATTN_HEAD_DIM
    h = (x_ref[...] * (1.0 + sc_ref[0]) + sh_ref[0]).astype(BF16)
    cos, sin = cos_ref[...], sin_ref[...]
    scale = dh ** -0.5 * LOG2_E

    def normed_rope(t, g):
        y = t * lax.rsqrt(jnp.mean(t * t, axis=-1, keepdims=True) + RMS_EPS) * g
        return y * cos + _swap_pairs(y, dh // 4) * sin

    t = jnp.dot(h, w_ref[...], preferred_element_type=F32)
    for hd in range(nq // dh):
        q_ref[:, hd * dh:(hd + 1) * dh] = (normed_rope(t[:, hd * dh:(hd + 1) * dh], qg_ref[...]) * scale).astype(BF16)
    k = jnp.concatenate([normed_rope(t[:, nq + hd * dh:nq + (hd + 1) * dh], kg_ref[...]) for hd in range(nkv // dh)],
                        axis=1)
    for j in range(kt_ref.shape[0]):
        kt_ref[j] = k[j * KEY_BLOCK:(j + 1) * KEY_BLOCK, :].T.astype(BF16)
    v_ref[...] = t[:, nq + nkv:].astype(BF16)


def _attn_project(rows, xa, mod, w, qg, kg, cos, sin):
    D = rows.D
    nkv = ATTN_KV_HEADS * ATTN_HEAD_DIM
    nq = w.shape[1] - 2 * nkv
    tm = rows.tile((512, 256))
    row_spec = lambda width: pl.BlockSpec((tm, width), lambda i: (i, 0))
    return pl.pallas_call(
        functools.partial(_qkv_kernel, nq=nq, nkv=nkv),
        grid=(rows.n // tm,),
        in_specs=[row_spec(D), rows.mod_spec(0, tm), rows.mod_spec(1, tm), _resident(w.shape),
                  _resident((1, ATTN_HEAD_DIM)), _resident((1, ATTN_HEAD_DIM)),
                  rows.pos_spec(tm, ATTN_HEAD_DIM), rows.pos_spec(tm, ATTN_HEAD_DIM)],
        out_specs=[row_spec(nq), pl.BlockSpec((tm // KEY_BLOCK, nkv, KEY_BLOCK), lambda i: (i, 0, 0)), row_spec(nkv)],
        out_shape=[jax.ShapeDtypeStruct((rows.n, nq), BF16),
                   jax.ShapeDtypeStruct((rows.n // KEY_BLOCK, nkv, KEY_BLOCK), BF16),
                   jax.ShapeDtypeStruct((rows.n, nkv), BF16)],
        compiler_params=_cparams("parallel"),
        name="attn_qkv",
    )(xa, mod, mod, w, qg.reshape(1, -1), kg.reshape(1, -1), cos, sin)


def _attn_kernel(q_ref, kc_ref, vc_ref, kl_ref, vl_ref, o_ref, m_sc, acc_sc, *, tq, blocks_per_chunk, n_lat_tiles):
    dh = ATTN_HEAD_DIM
    group = q_ref.shape[1] // dh // ATTN_KV_HEADS
    is_latent = pl.program_id(1) < n_lat_tiles
    n_lat_chunks = jnp.where(is_latent, kl_ref.shape[0] // blocks_per_chunk, 0)
    chunk = blocks_per_chunk * KEY_BLOCK

    heads = range(ATTN_KV_HEADS)
    head_rows = group * tq
    gs = [slice(g * dh, (g + 1) * dh) for g in heads]
    qs = [jnp.concatenate([q_ref[:, (g * group + j) * dh:(g * group + j + 1) * dh] for j in range(group)], axis=0)
          for g in heads]
    m_sc[...] = jnp.full(m_sc.shape, NEG_BIG, F32)
    acc_sc[...] = jnp.zeros(acc_sc.shape, F32)

    def online_softmax_step(kts, vs):
        width = kts[0].shape[1]
        s = jnp.concatenate([jnp.dot(qs[g], kts[g], preferred_element_type=F32) for g in heads], axis=0)
        m_old = m_sc[...]
        m_new = jnp.maximum(m_old, jnp.max(s, axis=-1, keepdims=True))
        p = jnp.exp2(s - jnp.concatenate([m_new] * (width // LANES), axis=1)).astype(BF16)
        alpha = jnp.exp2(m_old - m_new)
        ones = jnp.ones((width, dh), BF16)
        pv = jnp.concatenate([jnp.dot(p[g * head_rows:(g + 1) * head_rows], jnp.concatenate([vs[g], ones], axis=1),
                                      preferred_element_type=F32) for g in heads], axis=0)
        acc_sc[...] = jnp.concatenate([alpha, alpha], axis=1) * acc_sc[...] + pv
        m_sc[...] = m_new

    for c in range(kc_ref.shape[0]):
        online_softmax_step([kc_ref[c, gs[g], :] for g in heads],
                            [vc_ref[c * KEY_BLOCK:(c + 1) * KEY_BLOCK, gs[g]] for g in heads])

    def latent_chunk(c, carry):
        r0 = pl.multiple_of(c * chunk, chunk)
        online_softmax_step(
            [jnp.concatenate([kl_ref[c * blocks_per_chunk + j, gs[g], :] for j in range(blocks_per_chunk)], axis=1)
             for g in heads],
            [vl_ref[pl.ds(r0, chunk), gs[g]] for g in heads])
        return carry

    lax.fori_loop(0, n_lat_chunks, latent_chunk, 0)
    acc = acc_sc[...]
    o = acc[:, :dh] / acc[:, dh:]
    for hd in range(ATTN_KV_HEADS * group):
        o_ref[:, hd * dh:(hd + 1) * dh] = o[hd * tq:(hd + 1) * tq].astype(BF16)


def _attention(rows, q, kt, v, with_ctx):
    B, S, C = rows.B, rows.S, rows.C
    nq = q.shape[1]
    nkv = v.shape[1]
    assert S % KEY_BLOCK == 0 and C % KEY_BLOCK == 0 and rows.n_lat % C == 0
    tq = _pick_tile((256, 128), S, C)
    blocks_per_chunk = 2 if S % (2 * KEY_BLOCK) == 0 else 1
    lat_tiles, ctx_tiles = S // tq, C // tq
    tiles = lat_tiles + (ctx_tiles if with_ctx else 0)
    group = nq // nkv

    def q_map(b, i):
        return (jnp.where(i < lat_tiles, b * lat_tiles + i, rows.n_lat // tq + b * ctx_tiles + (i - lat_tiles)), 0)

    kt_ctx = pl.BlockSpec((C // KEY_BLOCK, nkv, KEY_BLOCK), lambda b, i: (rows.n_lat // C + b, 0, 0))
    kt_lat = pl.BlockSpec((S // KEY_BLOCK, nkv, KEY_BLOCK), lambda b, i: (b, 0, 0))
    v_ctx = pl.BlockSpec((C, nkv), lambda b, i: (rows.n_lat // C + b, 0))
    v_lat = pl.BlockSpec((S, nkv), lambda b, i: (b, 0))
    return pl.pallas_call(
        functools.partial(_attn_kernel, tq=tq, blocks_per_chunk=blocks_per_chunk, n_lat_tiles=lat_tiles),
        grid=(B, tiles),
        in_specs=[pl.BlockSpec((tq, nq), q_map), kt_ctx, v_ctx, kt_lat, v_lat],
        out_specs=pl.BlockSpec((tq, nq), q_map),
        out_shape=jax.ShapeDtypeStruct((rows.n if with_ctx else rows.n_lat, nq), BF16),
        scratch_shapes=[pltpu.VMEM((ATTN_KV_HEADS * group * tq, LANES), F32),
                        pltpu.VMEM((ATTN_KV_HEADS * group * tq, 2 * ATTN_HEAD_DIM), F32)],
        compiler_params=_cparams("parallel", "parallel"),
        name="attn_core",
    )(q, kt, v, kt, v)


def _proj_ln_kernel(*refs, first_tiles):
    n_seg = len(first_tiles) - 1
    o_refs = refs[:n_seg]
    w_ref, bias_ref, x_ref, gate_ref, g_ref, b_ref, out_ref = refs[n_seg:]
    i = pl.program_id(0)
    for seg, o_ref in enumerate(o_refs):
        @pl.when((i >= first_tiles[seg]) & (i < first_tiles[seg + 1]))
        def _():
            y = jnp.dot(o_ref[...], w_ref[...], preferred_element_type=F32) + bias_ref[...]
            out_ref[...] = _layer_norm(DEEPNORM_ALPHA * x_ref[...] + gate_ref[0] * y, g_ref[...], b_ref[...])


def _proj_ln(rows, o_segments, w, bias, xa, mod, ln_g, ln_b):
    D = rows.D
    kdim = w.shape[0]
    tm = rows.tile((512, 256, 128))
    first_tiles = [0]
    for o in o_segments:
        assert o.shape[0] % tm == 0
        first_tiles.append(first_tiles[-1] + o.shape[0] // tm)
    n_rows = first_tiles[-1] * tm

    def segment_spec(seg):
        first, count = first_tiles[seg], first_tiles[seg + 1] - first_tiles[seg]
        return pl.BlockSpec((tm, kdim), lambda i: (jnp.clip(i - first, 0, count - 1), 0))

    return pl.pallas_call(
        functools.partial(_proj_ln_kernel, first_tiles=tuple(first_tiles)),
        grid=(n_rows // tm,),
        in_specs=[segment_spec(seg) for seg in range(len(o_segments))] + [
            _resident(w.shape), _resident((1, D)), pl.BlockSpec((tm, D), lambda i: (i, 0)), rows.mod_spec(2, tm),
            _resident((1, D)), _resident((1, D))],
        out_specs=pl.BlockSpec((tm, D), lambda i: (i, 0)),
        out_shape=jax.ShapeDtypeStruct((n_rows, D), F32),
        compiler_params=_cparams("parallel"),
        name="proj_ln",
    )(*o_segments, w, bias.reshape(1, D), xa, mod, ln_g.reshape(1, D), ln_b.reshape(1, D))


def _ffn_kernel(x_ref, sh_ref, sc_ref, gate_ref, wg_ref, wu_ref, wd_ref, g_ref, b_ref, out_ref, h_sc, acc_sc):
    f = pl.program_id(1)

    @pl.when(f == 0)
    def _():
        h_sc[...] = (x_ref[...] * (1.0 + sc_ref[0]) + sh_ref[0]).astype(BF16)
        acc_sc[...] = jnp.zeros(acc_sc.shape, F32)

    h = h_sc[...]
    gt = jnp.dot(h, wg_ref[...], preferred_element_type=F32)
    up = jnp.dot(h, wu_ref[...], preferred_element_type=F32)
    act = (gt * jax.nn.sigmoid(gt) * up).astype(BF16)
    acc_sc[...] += jnp.dot(act, wd_ref[...], preferred_element_type=F32)

    @pl.when(f == pl.num_programs(1) - 1)
    def _():
        out_ref[...] = _layer_norm(DEEPNORM_ALPHA * x_ref[...] + gate_ref[0] * acc_sc[...], g_ref[...], b_ref[...])


def _ffn_ln(rows, xa, mod, w_gu, w_down, ln_g, ln_b, n_rows):
    D = rows.D
    F = w_down.shape[0]
    tm = rows.tile((512, 256, 128))
    tf = _pick_tile((1792, 512, 256, 128), F)
    nf = F // tf
    return pl.pallas_call(
        _ffn_kernel,
        grid=(n_rows // tm, nf),
        in_specs=[pl.BlockSpec((tm, D), lambda i, f: (i, 0)),
                  rows.mod_spec(3, tm), rows.mod_spec(4, tm), rows.mod_spec(5, tm),
                  pl.BlockSpec((D, tf), lambda i, f: (0, f)), pl.BlockSpec((D, tf), lambda i, f: (0, nf + f)),
                  pl.BlockSpec((tf, D), lambda i, f: (f, 0)), _resident((1, D)), _resident((1, D))],
        out_specs=pl.BlockSpec((tm, D), lambda i, f: (i, 0)),
        out_shape=jax.ShapeDtypeStruct((n_rows, D), F32),
        scratch_shapes=[pltpu.VMEM((tm, D), BF16), pltpu.VMEM((tm, D), F32)],
        compiler_params=_cparams("parallel", "arbitrary"),
        name="ffn_ln",
    )(xa, mod, mod, mod, w_gu, w_gu, w_down, ln_g.reshape(1, D), ln_b.reshape(1, D))


MOE_ROW_BLOCKS = (512, 256, 128)


def _router_kernel(x_ref, sh_ref, sc_ref, r_ref, h_ref, gates_ref, *, n_experts):
    h = x_ref[...] * (1.0 + sc_ref[0]) + sh_ref[0]
    h_ref[...] = h.astype(BF16)
    logits = jnp.dot(h, r_ref[...], preferred_element_type=F32, precision=lax.Precision.HIGHEST)
    lane = lax.broadcasted_iota(jnp.int32, logits.shape, 1)
    lowest = jnp.finfo(F32).min
    lg = jnp.where(lane < n_experts, logits, lowest)
    m1 = jnp.max(lg, axis=-1, keepdims=True)
    i1 = jnp.min(jnp.where(lg == m1, lane, LANES), axis=-1, keepdims=True)
    lg2 = jnp.where(lane == i1, lowest, lg)
    m2 = jnp.max(lg2, axis=-1, keepdims=True)
    i2 = jnp.min(jnp.where(lg2 == m2, lane, LANES), axis=-1, keepdims=True)
    e2 = jnp.exp(m2 - m1)
    denom = 1.0 + e2
    gates = jnp.where(lane == i1, 1.0 / denom, 0.0) + jnp.where(lane == i2, e2 / denom, 0.0)
    gates_ref[...] = gates.T[:n_experts, :]


def _router(rows, xa, mod, router, n_rows):
    D = rows.D
    n_experts = router.shape[1]
    tm = rows.tile((512, 256, 128))
    router_pad = jnp.pad(router, ((0, 0), (0, LANES - n_experts)))
    return pl.pallas_call(
        functools.partial(_router_kernel, n_experts=n_experts),
        grid=(n_rows // tm,),
        in_specs=[pl.BlockSpec((tm, D), lambda i: (i, 0)), rows.mod_spec(3, tm), rows.mod_spec(4, tm),
                  _resident((D, LANES))],
        out_specs=[pl.BlockSpec((tm, D), lambda i: (i, 0)), pl.BlockSpec((n_experts, tm), lambda i: (0, i))],
        out_shape=[jax.ShapeDtypeStruct((n_rows, D), BF16), jax.ShapeDtypeStruct((n_experts, n_rows), F32)],
        compiler_params=_cparams("parallel"),
        name="moe_router",
    )(xa, mod, mod, router_pad)


def _plan_kernel(g_ref, rank_ref, cnt_ref):
    n_experts, tile = g_ref.shape
    routed = g_ref[...] > 0.0
    ones = jnp.where(routed, 1.0, 0.0)
    before = jnp.where(lax.broadcasted_iota(jnp.int32, (LANES, LANES), 0)
                       < lax.broadcasted_iota(jnp.int32, (LANES, LANES), 1), 1.0, 0.0)
    seen = jnp.zeros((n_experts, 1), F32)
    for c in range(tile // LANES):
        cs = slice(c * LANES, (c + 1) * LANES)
        rank = jnp.dot(ones[:, cs], before, preferred_element_type=F32) + seen
        rank_ref[:, cs] = jnp.where(routed[:, cs], rank, -1.0).astype(jnp.int32)
        seen = seen + jnp.sum(ones[:, cs], axis=1, keepdims=True)
    cnt_ref[0] = jnp.broadcast_to(seen, (n_experts, LANES)).astype(jnp.int32)


def _moe_plan(gates_t, tile):
    n_experts, n_rows = gates_t.shape
    n_tiles = n_rows // tile
    rank, counts = pl.pallas_call(
        _plan_kernel,
        grid=(n_tiles,),
        in_specs=[pl.BlockSpec((n_experts, tile), lambda s: (0, s))],
        out_specs=[pl.BlockSpec((n_experts, tile), lambda s: (0, s)),
                   pl.BlockSpec((1, n_experts, LANES), lambda s: (s, 0, 0))],
        out_shape=[jax.ShapeDtypeStruct((n_experts, n_rows), jnp.int32),
                   jax.ShapeDtypeStruct((n_tiles, n_experts, LANES), jnp.int32)],
        compiler_params=_cparams("parallel"),
        name="moe_plan",
    )(gates_t)
    count = counts[:, :, 0].reshape(-1)
    smallest = MOE_ROW_BLOCKS[-1]
    return rank, (count + smallest - 1) // smallest * smallest


def _moe_kernel(nrows_ref, h_ref, rank_ref, gate_ref, wg_ref, wu_ref, wd_ref, out_ref, xg_sc, y_sc):
    s, e, f = pl.program_id(0), pl.program_id(1), pl.program_id(2)
    tile = h_ref.shape[0]
    n_rows = nrows_ref[s * pl.num_programs(1) + e]
    big = MOE_ROW_BLOCKS[0]

    @pl.when((e == 0) & (f == 0))
    def _():
        out_ref[...] = jnp.zeros(out_ref.shape, F32)

    def for_each_block(fn):
        def body(r, carry):
            fn(r * big, big)
            return carry

        lax.fori_loop(0, n_rows // big, body, 0)
        for size in MOE_ROW_BLOCKS[1:]:
            @pl.when(n_rows // size % 2 == 1)
            def _():
                fn(n_rows // (2 * size) * (2 * size), size)

    def rows_at(start, size):
        return pl.ds(pl.multiple_of(start, MOE_ROW_BLOCKS[-1]), size)

    def slot_matches(start, size):
        slot = lax.broadcasted_iota(jnp.int32, (size, tile), 0) + start
        return slot == rank_ref[pl.ds(e, 1), :]

    def gather(start, size):
        onehot = jnp.where(slot_matches(start, size), 1.0, 0.0).astype(BF16)
        xg_sc[rows_at(start, size), :] = jnp.dot(onehot, h_ref[...], preferred_element_type=F32).astype(BF16)
        y_sc[rows_at(start, size), :] = jnp.zeros((size, y_sc.shape[1]), F32)

    def expert(start, size):
        xb = xg_sc[rows_at(start, size), :]
        gt = jnp.dot(xb, wg_ref[0], preferred_element_type=F32)
        up = jnp.dot(xb, wu_ref[0], preferred_element_type=F32)
        act = (gt * jax.nn.sigmoid(gt) * up).astype(BF16)
        y_sc[rows_at(start, size), :] += jnp.dot(act, wd_ref[0], preferred_element_type=F32)

    def scatter(start, size):
        weighted = jnp.where(slot_matches(start, size), gate_ref[pl.ds(e, 1), :], 0.0).astype(BF16)
        out_ref[...] += lax.dot_general(weighted, y_sc[rows_at(start, size), :].astype(BF16),
                                        (((0,), (0,)), ((), ())), preferred_element_type=F32)

    @pl.when(f == 0)
    def _():
        for_each_block(gather)

    for_each_block(expert)

    @pl.when(f == pl.num_programs(2) - 1)
    def _():
        for_each_block(scatter)


def _moe(h, gates_t, w_gu, w_down):
    n_rows, D = h.shape
    E, F = w_down.shape[0], w_down.shape[1]
    tile = _pick_tile((2176, 2048, 1024, 512, 256, 128), n_rows)
    tf = _pick_tile((512, 256, 128), F)
    nf = F // tf
    rank, n_block_rows = _moe_plan(gates_t, tile)
    max_rows = -(-tile // MOE_ROW_BLOCKS[-1]) * MOE_ROW_BLOCKS[-1]
    grid_spec = pltpu.PrefetchScalarGridSpec(
        num_scalar_prefetch=1,
        grid=(n_rows // tile, E, nf),
        in_specs=[pl.BlockSpec((tile, D), lambda s, e, f, *_: (s, 0)),
                  pl.BlockSpec((E, tile), lambda s, e, f, *_: (0, s)),
                  pl.BlockSpec((E, tile), lambda s, e, f, *_: (0, s)),
                  pl.BlockSpec((1, D, tf), lambda s, e, f, *_: (e, 0, f)),
                  pl.BlockSpec((1, D, tf), lambda s, e, f, *_: (e, 0, nf + f)),
                  pl.BlockSpec((1, tf, D), lambda s, e, f, *_: (e, f, 0))],
        out_specs=pl.BlockSpec((tile, D), lambda s, e, f, *_: (s, 0)),
        scratch_shapes=[pltpu.VMEM((max_rows, D), BF16), pltpu.VMEM((max_rows, D), F32)],
    )
    return pl.pallas_call(
        _moe_kernel,
        grid_spec=grid_spec,
        out_shape=jax.ShapeDtypeStruct((n_rows, D), F32),
        compiler_params=_cparams("parallel", "arbitrary", "arbitrary"),
        name="moe_experts",
    )(n_block_rows, h, rank, gates_t, w_gu, w_gu, w_down)


def _residual_ln_kernel(x_ref, f_ref, gate_ref, g_ref, b_ref, out_ref):
    out_ref[...] = _layer_norm(DEEPNORM_ALPHA * x_ref[...] + gate_ref[0] * f_ref[...], g_ref[...], b_ref[...])


def _residual_ln(rows, xa, f, mod, part, ln_g, ln_b, n_rows):
    D = rows.D
    tm = rows.tile((512, 256, 128))
    return pl.pallas_call(
        _residual_ln_kernel,
        grid=(n_rows // tm,),
        in_specs=[pl.BlockSpec((tm, D), lambda i: (i, 0)), pl.BlockSpec((tm, D), lambda i: (i, 0)),
                  rows.mod_spec(part, tm), _resident((1, D)), _resident((1, D))],
        out_specs=pl.BlockSpec((tm, D), lambda i: (i, 0)),
        out_shape=jax.ShapeDtypeStruct((n_rows, D), F32),
        compiler_params=_cparams("parallel"),
        name="residual_ln",
    )(xa, f, mod, ln_g.reshape(1, D), ln_b.reshape(1, D))


def _moe_ln(rows, xa, mod, router, w_gu, w_down, ln_g, ln_b, n_rows):
    h, gates_t = _router(rows, xa, mod, router, n_rows)
    return _residual_ln(rows, xa, _moe(h, gates_t, w_gu, w_down), mod, 5, ln_g, ln_b, n_rows)


def _gmlp_kernel(x_ref, sh_ref, sc_ref, gate_ref, win_ref, bin_ref, vg_ref, vb_ref, ws_ref, bs_ref, wout_ref,
                 bout_ref, g_ref, b_ref, out_ref, gated_sc):
    tm = x_ref.shape[0]
    inner = wout_ref.shape[0]
    gdim = inner // CMLP_GROUPS
    x = x_ref[...]
    h = (x * (1.0 + sc_ref[0]) + sh_ref[0]).astype(BF16)
    v = jax.nn.gelu(jnp.dot(h, win_ref[:, inner:], preferred_element_type=F32) + bin_ref[:, inner:])
    v = _layer_norm(v, vg_ref[...], vb_ref[...]).astype(BF16)
    for gi in range(CMLP_GROUPS):
        cs = slice(gi * gdim, (gi + 1) * gdim)
        u = jax.nn.gelu(jnp.dot(h, win_ref[:, cs], preferred_element_type=F32) + bin_ref[:, cs])
        w_s = ws_ref[gi]
        bias = bs_ref[:, gi:gi + 1]
        for c in range(tm // CMLP_CHUNK):
            rs = slice(c * CMLP_CHUNK, (c + 1) * CMLP_CHUNK)
            mixed = jnp.dot(w_s, v[rs, cs], preferred_element_type=F32) + bias
            gated_sc[rs, cs] = (u[rs] * mixed).astype(BF16)
    y = jnp.dot(gated_sc[...], wout_ref[...], preferred_element_type=F32) + bout_ref[...]
    out_ref[...] = _layer_norm(DEEPNORM_ALPHA * x + gate_ref[0] * y, g_ref[...], b_ref[...])


def _gmlp_ln(rows, xa, mod, w_in, b_in, vg, vb, w_s, b_s, w_out, b_out, ln_g, ln_b, n_rows):
    D = rows.D
    inner = w_out.shape[0]
    tm = rows.tile((512, 256, 128))
    return pl.pallas_call(
        _gmlp_kernel,
        grid=(n_rows // tm,),
        in_specs=[pl.BlockSpec((tm, D), lambda i: (i, 0)), rows.mod_spec(0, tm), rows.mod_spec(1, tm),
                  rows.mod_spec(2, tm), _resident(w_in.shape), _resident((1, 2 * inner)), _resident((1, inner)),
                  _resident((1, inner)), _resident(w_s.shape), _resident((CMLP_CHUNK, CMLP_GROUPS)),
                  _resident(w_out.shape), _resident((1, D)), _resident((1, D)), _resident((1, D))],
        out_specs=pl.BlockSpec((tm, D), lambda i: (i, 0)),
        out_shape=jax.ShapeDtypeStruct((n_rows, D), F32),
        scratch_shapes=[pltpu.VMEM((tm, inner), BF16)],
        compiler_params=_cparams("parallel"),
        name="gmlp_ln",
    )(xa, mod, mod, mod, w_in, b_in.reshape(1, -1), vg.reshape(1, -1), vb.reshape(1, -1), w_s, b_s.T, w_out,
      b_out.reshape(1, D), ln_g.reshape(1, D), ln_b.reshape(1, D))


def _ret_proj_kernel(x_ref, sh_ref, sc_ref, w_ref, cos_ref, sin_ref, q_ref, k_ref, v_ref, sg_ref, *, nk, nv, dk):
    h = (x_ref[...] * (1.0 + sc_ref[0]) + sh_ref[0]).astype(BF16)
    cos, sin = cos_ref[...], sin_ref[...]
    k_scale = dk ** -0.5

    def rope_store(dst_ref, col0, scale):
        full = jnp.dot(h, w_ref[:, col0:col0 + nk], preferred_element_type=F32) * scale
        for j in range(nk // LANES):
            t = full[:, j * LANES:(j + 1) * LANES]
            ts = slice((j % (dk // LANES)) * LANES, (j % (dk // LANES) + 1) * LANES)
            dst_ref[:, j * LANES:(j + 1) * LANES] = (t * cos[:, ts] + _swap_pairs(t, dk // 4) * sin[:, ts]).astype(BF16)

    rope_store(q_ref, 0, 1.0)
    rope_store(k_ref, nk, k_scale)
    blk = 512
    for j in range(nv // blk):
        v_ref[:, j * blk:(j + 1) * blk] = jnp.dot(
            h, w_ref[:, 2 * nk + j * blk:2 * nk + (j + 1) * blk], preferred_element_type=F32).astype(BF16)
        gt = jnp.dot(h, w_ref[:, 2 * nk + nv + j * blk:2 * nk + nv + (j + 1) * blk], preferred_element_type=F32)
        sg_ref[:, j * blk:(j + 1) * blk] = (gt * jax.nn.sigmoid(gt)).astype(BF16)


def _ret_project(rows, xa, mod, w, cos, sin):
    D = rows.D
    nk = D
    nv = (w.shape[1] - 2 * nk) // 2
    dk = nk // RET_HEADS
    tm = rows.tile((512, 256))
    row_spec = lambda width: pl.BlockSpec((tm, width), lambda i: (i, 0))
    return pl.pallas_call(
        functools.partial(_ret_proj_kernel, nk=nk, nv=nv, dk=dk),
        grid=(rows.n // tm,),
        in_specs=[row_spec(D), rows.mod_spec(0, tm), rows.mod_spec(1, tm), _resident(w.shape),
                  rows.pos_spec(tm, dk), rows.pos_spec(tm, dk)],
        out_specs=[row_spec(nk), row_spec(nk), row_spec(nv), row_spec(nv)],
        out_shape=[jax.ShapeDtypeStruct((rows.n, nk), BF16), jax.ShapeDtypeStruct((rows.n, nk), BF16),
                   jax.ShapeDtypeStruct((rows.n, nv), BF16), jax.ShapeDtypeStruct((rows.n, nv), BF16)],
        compiler_params=_cparams("parallel"),
        name="ret_proj",
    )(xa, mod, mod, w, cos, sin)


RET_SCAN_CHUNK = 256


def _ret_scan_kernel(lg_ref, q_ref, k_ref, v_ref, sg_ref, sf_ref, sb_ref, o_ref, sf_out, sb_out,
                     state_sc, part_sc, inner_sc, cross_sc, weight_sc):
    ch = RET_SCAN_CHUNK
    n_chunks = q_ref.shape[0] // ch
    dk, dv = q_ref.shape[1], v_ref.shape[1]
    head = pl.program_id(1)

    chunk_decay = []
    for d in range(2):
        lg = lg_ref[d, head]
        ii = lax.broadcasted_iota(jnp.int32, (ch, ch), 0)
        jj = lax.broadcasted_iota(jnp.int32, (ch, ch), 1)
        dist = (ii - jj) if d == 0 else (jj - ii)
        inner_sc[d] = jnp.where(dist >= 0, jnp.exp(jnp.maximum(dist, 0).astype(F32) * lg), 0.0)
        row_v = lax.broadcasted_iota(jnp.int32, (ch, dv), 0)
        row_k = lax.broadcasted_iota(jnp.int32, (ch, dk), 0)
        cross_sc[d] = jnp.exp(((row_v if d == 0 else ch - 1 - row_v) + 1).astype(F32) * lg)
        weight_sc[d] = jnp.exp((ch - 1 - (row_k if d == 0 else ch - 1 - row_k)).astype(F32) * lg)
        chunk_decay.append(jnp.exp(jnp.full((1, 1), ch, F32) * lg))
    state_sc[0] = sf_ref[0, 0]
    state_sc[1] = sb_ref[0, 0]

    def visit(d, c):
        rows = pl.ds(pl.multiple_of(c * ch, ch), ch)
        q, k, v = q_ref[rows, :], k_ref[rows, :], v_ref[rows, :]
        scores = lax.dot_general(q, k, (((1,), (1,)), ((), ())), preferred_element_type=F32) * inner_sc[d]
        state = state_sc[d]
        out = (jnp.dot(scores.astype(BF16), v, preferred_element_type=F32)
               + jnp.dot(q, state.astype(BF16), preferred_element_type=F32) * cross_sc[d])
        kw = (k.astype(F32) * weight_sc[d]).astype(BF16)
        state_sc[d] = state * chunk_decay[d] + lax.dot_general(
            kw, v, (((0,), (0,)), ((), ())), preferred_element_type=F32)
        return rows, out

    def finish(rows, o):
        o = o * lax.rsqrt(jnp.mean(o * o, axis=-1, keepdims=True) + RMS_EPS)
        o_ref[rows, :] = (sg_ref[rows, :].astype(F32) * o).astype(BF16)

    def first_visits(t, carry):
        for d, c in ((0, t), (1, n_chunks - 1 - t)):
            rows, out = visit(d, c)
            part_sc[rows, :] = out
        return carry

    def second_visits(t, carry):
        for d, c in ((0, t), (1, n_chunks - 1 - t)):
            rows, out = visit(d, c)
            finish(rows, out + part_sc[rows, :])
        return carry

    lax.fori_loop(0, n_chunks // 2, first_visits, 0)
    if n_chunks % 2 == 1:
        rows, out_f = visit(0, n_chunks // 2)
        _, out_b = visit(1, n_chunks // 2)
        finish(rows, out_f + out_b)
    lax.fori_loop((n_chunks + 1) // 2, n_chunks, second_visits, 0)
    sf_out[0, 0] = state_sc[0]
    sb_out[0, 0] = state_sc[1]


def _ret_scan_segment(rows, log_gamma, q, k, v, sg, state_f, state_b, *, seg_len, first_block):
    B = rows.B
    dk = q.shape[1] // RET_HEADS
    dv = v.shape[1] // RET_HEADS
    ch = RET_SCAN_CHUNK
    assert seg_len % ch == 0
    kspec = pl.BlockSpec((seg_len, dk), lambda b, h: (first_block + b, h))
    vspec = pl.BlockSpec((seg_len, dv), lambda b, h: (first_block + b, h))
    sspec = pl.BlockSpec((1, 1, dk, dv), lambda b, h: (b, h, 0, 0))
    state_shape = jax.ShapeDtypeStruct((B, RET_HEADS, dk, dv), F32)
    return pl.pallas_call(
        _ret_scan_kernel,
        grid=(B, RET_HEADS),
        in_specs=[pl.BlockSpec(memory_space=pltpu.SMEM), kspec, kspec, vspec, vspec, sspec, sspec],
        out_specs=[pl.BlockSpec((seg_len, dv), lambda b, h: (b, h)), sspec, sspec],
        out_shape=[jax.ShapeDtypeStruct((B * seg_len, v.shape[1]), BF16), state_shape, state_shape],
        scratch_shapes=[pltpu.VMEM((2, dk, dv), F32), pltpu.VMEM((seg_len, dv), F32),
                        pltpu.VMEM((2, ch, ch), F32), pltpu.VMEM((2, ch, dv), F32), pltpu.VMEM((2, ch, dk), F32)],
        compiler_params=_cparams("parallel", "parallel"),
        name="ret_scan",
    )(log_gamma, q, k, v, sg, state_f, state_b)


def _retention(rows, log_gamma, q, k, v, sg):
    B, S, C = rows.B, rows.S, rows.C
    assert rows.n_lat % C == 0
    dk, dv = q.shape[1] // RET_HEADS, v.shape[1] // RET_HEADS
    zeros = jnp.zeros((B, RET_HEADS, dk, dv), F32)
    o_ctx, state_f, state_b = _ret_scan_segment(rows, log_gamma, q, k, v, sg, zeros, zeros,
                                                seg_len=C, first_block=rows.n_lat // C)
    o_lat, _, _ = _ret_scan_segment(rows, log_gamma, q, k, v, sg, state_f, state_b, seg_len=S, first_block=0)
    return [o_lat, o_ctx]


def kernel(x, c, ctx, c_ctx, l0_ada_w, l0_ada_b, l0_ln1_g, l0_ln1_b, l0_ln2_g, l0_ln2_b, l0_attn_wqkv, l0_attn_q_norm, l0_attn_k_norm, l0_attn_wo, l0_ffn_w_gu, l0_ffn_w_down, l1_ada_w, l1_ada_b, l1_ln1_g, l1_ln1_b, l1_ln2_g, l1_ln2_b, l1_cmlp_w_in, l1_cmlp_b_in, l1_cmlp_v_norm_g, l1_cmlp_v_norm_b, l1_cmlp_w_s, l1_cmlp_b_s, l1_cmlp_w_out, l1_cmlp_b_out, l1_moe_router, l1_moe_w_gu, l1_moe_w_down, l2_ada_w, l2_ada_b, l2_ln1_g, l2_ln1_b, l2_ln2_g, l2_ln2_b, l2_ret_wqkvg, l2_ret_decay, l2_ret_wo, l2_ffn_w_gu, l2_ffn_w_down, l3_ada_w, l3_ada_b, l3_ln1_g, l3_ln1_b, l3_ln2_g, l3_ln2_b, l3_attn_wqkv, l3_attn_q_norm, l3_attn_k_norm, l3_attn_wo, l3_moe_router, l3_moe_w_gu, l3_moe_w_down):
    B, S, D = x.shape
    C = ctx.shape[1]
    rows = _Rows(B, S, C, D)
    assert S % GRID_W == 0 and C % RET_CHUNK == 0 and S % RET_CHUNK == 0

    xa = jnp.concatenate([x.reshape(rows.n_lat, D), ctx.reshape(rows.n_ctx, D)], axis=0)
    n_cond = -(-(B + 1) // 8) * 8
    cond = jnp.concatenate([c, c_ctx[None, :], jnp.zeros((n_cond - B - 1, D), F32)], axis=0)

    def modulation(ada_w, ada_b):
        return _ada(cond, ada_w, ada_b).reshape(n_cond * 6, 1, D)

    pad = rows.tile((512, 256))
    attn_cos, attn_sin = _rope_tables(S, ATTN_HEAD_DIM, pad)
    ret_cos, ret_sin = _rope_tables(S, D // RET_HEADS, pad)
    zero_bias = jnp.zeros((D,), F32)
    bf = lambda w: w.astype(BF16)

    def attention_layer(xa, mod, wqkv, qg, kg, wo, ln_g, ln_b, with_ctx):
        q, k, v = _attn_project(rows, xa, mod, bf(wqkv), qg, kg, attn_cos, attn_sin)
        o = _attention(rows, q, k, v, with_ctx)
        return _proj_ln(rows, [o], bf(wo), zero_bias, xa, mod, ln_g, ln_b)

    def retention_layer(xa, mod, wqkvg, decay, wo, ln_g, ln_b):
        log_gamma = -jnp.exp(decay.astype(F32))
        q, k, v, sg = _ret_project(rows, xa, mod, bf(wqkvg), ret_cos, ret_sin)
        o_segments = _retention(rows, log_gamma, q, k, v, sg)
        return _proj_ln(rows, o_segments, bf(wo), zero_bias, xa, mod, ln_g, ln_b)

    mod = modulation(l0_ada_w, l0_ada_b)
    xa = attention_layer(xa, mod, l0_attn_wqkv, l0_attn_q_norm, l0_attn_k_norm, l0_attn_wo, l0_ln1_g, l0_ln1_b, True)
    xa = _ffn_ln(rows, xa, mod, bf(l0_ffn_w_gu), bf(l0_ffn_w_down), l0_ln2_g, l0_ln2_b, rows.n)

    mod = modulation(l1_ada_w, l1_ada_b)
    xa = _gmlp_ln(rows, xa, mod, bf(l1_cmlp_w_in), l1_cmlp_b_in, l1_cmlp_v_norm_g, l1_cmlp_v_norm_b,
                  bf(l1_cmlp_w_s), l1_cmlp_b_s, bf(l1_cmlp_w_out), l1_cmlp_b_out, l1_ln1_g, l1_ln1_b, rows.n)
    xa = _moe_ln(rows, xa, mod, l1_moe_router, bf(l1_moe_w_gu), bf(l1_moe_w_down), l1_ln2_g, l1_ln2_b, rows.n)

    mod = modulation(l2_ada_w, l2_ada_b)
    xa = retention_layer(xa, mod, l2_ret_wqkvg, l2_ret_decay, l2_ret_wo, l2_ln1_g, l2_ln1_b)
    xa = _ffn_ln(rows, xa, mod, bf(l2_ffn_w_gu), bf(l2_ffn_w_down), l2_ln2_g, l2_ln2_b, rows.n)

    mod = modulation(l3_ada_w, l3_ada_b)
    xl = attention_layer(xa, mod, l3_attn_wqkv, l3_attn_q_norm, l3_attn_k_norm, l3_attn_wo, l3_ln1_g, l3_ln1_b, False)
    xl = _moe_ln(rows, xl, mod, l3_moe_router, bf(l3_moe_w_gu), bf(l3_moe_w_down), l3_ln2_g, l3_ln2_b, rows.n_lat)
    return xl.reshape(B, S, D)
```

```python
import functools

import jax
import jax.numpy as jnp
from jax import lax
from jax.experimental import pallas as pl
from jax.experimental.pallas import tpu as pltpu

F32 = jnp.float32
BF16 = jnp.bfloat16

DEPTH = 4
GRID_W = 64
ROPE_THETA = 10000.0
DEEPNORM_ALPHA = (2 * DEPTH) ** 0.25
LN_EPS = 1e-5
RMS_EPS = 1e-6
ATTN_HEAD_DIM = 128
ATTN_KV_HEADS = 2
CMLP_CHUNK = 128
CMLP_GROUPS = 8
RET_HEADS = 4
RET_CHUNK = 128
MOE_EXPERTS = 8

LANES = 128
VMEM_LIMIT_BYTES = 56 * 1024 * 1024
NEG_BIG = -1e30


def _cparams(*sem):
    return pltpu.CompilerParams(dimension_semantics=sem, vmem_limit_bytes=VMEM_LIMIT_BYTES)


def _resident(shape):
    nd = len(shape)
    return pl.BlockSpec(shape, lambda *_: (0,) * nd)


def _layer_norm(z, g, b):
    mu = jnp.mean(z, axis=-1, keepdims=True)
    zc = z - mu
    var = jnp.mean(zc * zc, axis=-1, keepdims=True)
    return zc * lax.rsqrt(var + LN_EPS) * g + b


def _pick_tile(candidates, *extents):
    for t in candidates:
        if all(e % t == 0 for e in extents):
            return t
    raise ValueError(f"no tile in {candidates} divides {extents}")


class _Rows:
    def __init__(self, batch, seq, ctx_len, dim):
        self.B, self.S, self.C, self.D = batch, seq, ctx_len, dim
        self.n_lat = batch * seq
        self.n_ctx = batch * ctx_len
        self.n = self.n_lat + self.n_ctx

    def tile(self, candidates):
        return _pick_tile(candidates, self.S, self.n_ctx)

    def mod_spec(self, part, tm):
        S, B = self.S, self.B
        return pl.BlockSpec((1, 1, self.D), lambda i, *_: (jnp.minimum(i * tm // S, B) * 6 + part, 0, 0))

    def pos_spec(self, tm, width):
        n_lat_tiles, per_seq = self.n_lat // tm, self.S // tm
        return pl.BlockSpec((tm, width), lambda i, *_: (jnp.where(i < n_lat_tiles, i % per_seq, per_seq), 0))


def _ada_kernel(c_ref, w_ref, b_ref, o_ref):
    c = c_ref[...]
    s = c * jax.nn.sigmoid(c)
    o_ref[...] = jnp.dot(s, w_ref[...], preferred_element_type=F32, precision=lax.Precision.HIGHEST) + b_ref[...]


def _ada(cond, w, b):
    r, d = cond.shape
    n = w.shape[1]
    tn = _pick_tile((1536, 1024, 512, 128), n)
    return pl.pallas_call(
        _ada_kernel,
        grid=(n // tn,),
        in_specs=[_resident((r, d)), pl.BlockSpec((d, tn), lambda j: (0, j)), pl.BlockSpec((1, tn), lambda j: (0, j))],
        out_specs=pl.BlockSpec((r, tn), lambda j: (0, j)),
        out_shape=jax.ShapeDtypeStruct((r, n), F32),
        compiler_params=_cparams("parallel"),
        name="ada",
    )(cond, w, b.reshape(1, n))


def _rope_tables(seq, head_dim, pad_rows):
    nf = head_dim // 4
    inv_freq = ROPE_THETA ** (-jnp.arange(nf, dtype=F32) / nf)
    t = jnp.arange(seq, dtype=jnp.int32)
    row = (t // GRID_W).astype(F32)[:, None] * inv_freq
    col = (t % GRID_W).astype(F32)[:, None] * inv_freq
    cos = jnp.concatenate([jnp.cos(row), jnp.cos(row), jnp.cos(col), jnp.cos(col)], axis=-1)
    sin = jnp.concatenate([-jnp.sin(row), jnp.sin(row), -jnp.sin(col), jnp.sin(col)], axis=-1)
    cos = jnp.concatenate([cos, jnp.ones((pad_rows, head_dim), F32)], axis=0)
    sin = jnp.concatenate([sin, jnp.zeros((pad_rows, head_dim), F32)], axis=0)
    return cos, sin


def _swap_pairs(y, half):
    if 2 * half == LANES:
        return pltpu.roll(y, half, 1)
    lane = lax.broadcasted_iota(jnp.int32, y.shape, 1)
    return jnp.where(lane % (2 * half) < half, pltpu.roll(y, LANES - half, 1), pltpu.roll(y, half, 1))


KEY_BLOCK = 256
LOG2_E = 1.4426950408889634


def _qkv_kernel(x_ref, sh_ref, sc_ref, w_ref, qg_ref, kg_ref, cos_ref, sin_ref, q_ref, kt_ref, v_ref, *, nq, nkv):
    dh = ATTN_HEAD_DIM
    h = (x_ref[...] * (1.0 + sc_ref[0]) + sh_ref[0]).astype(BF16)
    cos, sin = cos_ref[...], sin_ref[...]
    scale = dh ** -0.5 * LOG2_E

    def normed_rope(t, g):
        y = t * lax.rsqrt(jnp.mean(t * t, axis=-1, keepdims=True) + RMS_EPS) * g
        return y * cos + _swap_pairs(y, dh // 4) * sin

    t = jnp.dot(h, w_ref[...], preferred_element_type=F32)
    for hd in range(nq // dh):
        q_ref[:, hd * dh:(hd + 1) * dh] = (normed_rope(t[:, hd * dh:(hd + 1) * dh], qg_ref[...]) * scale).astype(BF16)
    k = jnp.concatenate([normed_rope(t[:, nq + hd * dh:nq + (hd + 1) * dh], kg_ref[...]) for hd in range(nkv // dh)],
                        axis=1)
    for j in range(kt_ref.shape[0]):
        kt_ref[j] = k[j * KEY_BLOCK:(j + 1) * KEY_BLOCK, :].T.astype(BF16)
    v_ref[...] = t[:, nq + nkv:].astype(BF16)


def _attn_project(rows, xa, mod, w, qg, kg, cos, sin):
    D = rows.D
    nkv = ATTN_KV_HEADS * ATTN_HEAD_DIM
    nq = w.shape[1] - 2 * nkv
    tm = rows.tile((512, 256))
    row_spec = lambda width: pl.BlockSpec((tm, width), lambda i: (i, 0))
    return pl.pallas_call(
        functools.partial(_qkv_kernel, nq=nq, nkv=nkv),
        grid=(rows.n // tm,),
        in_specs=[row_spec(D), rows.mod_spec(0, tm), rows.mod_spec(1, tm), _resident(w.shape),
                  _resident((1, ATTN_HEAD_DIM)), _resident((1, ATTN_HEAD_DIM)),
                  rows.pos_spec(tm, ATTN_HEAD_DIM), rows.pos_spec(tm, ATTN_HEAD_DIM)],
        out_specs=[row_spec(nq), pl.BlockSpec((tm // KEY_BLOCK, nkv, KEY_BLOCK), lambda i: (i, 0, 0)), row_spec(nkv)],
        out_shape=[jax.ShapeDtypeStruct((rows.n, nq), BF16),
                   jax.ShapeDtypeStruct((rows.n // KEY_BLOCK, nkv, KEY_BLOCK), BF16),
                   jax.ShapeDtypeStruct((rows.n, nkv), BF16)],
        compiler_params=_cparams("parallel"),
        name="attn_qkv",
    )(xa, mod, mod, w, qg.reshape(1, -1), kg.reshape(1, -1), cos, sin)


def _attn_kernel(q_ref, kc_ref, vc_ref, kl_ref, vl_ref, o_ref, m_sc, acc_sc, *, tq, blocks_per_chunk, n_lat_tiles):
    dh = ATTN_HEAD_DIM
    group = q_ref.shape[1] // dh // ATTN_KV_HEADS
    is_latent = pl.program_id(1) < n_lat_tiles
    n_lat_chunks = jnp.where(is_latent, kl_ref.shape[0] // blocks_per_chunk, 0)
    chunk = blocks_per_chunk * KEY_BLOCK

    heads = range(ATTN_KV_HEADS)
    head_rows = group * tq
    gs = [slice(g * dh, (g + 1) * dh) for g in heads]
    qs = [jnp.concatenate([q_ref[:, (g * group + j) * dh:(g * group + j + 1) * dh] for j in range(group)], axis=0)
          for g in heads]
    m_sc[...] = jnp.full(m_sc.shape, NEG_BIG, F32)
    acc_sc[...] = jnp.zeros(acc_sc.shape, F32)

    def online_softmax_step(kts, vs):
        width = kts[0].shape[1]
        s = jnp.concatenate([jnp.dot(qs[g], kts[g], preferred_element_type=F32) for g in heads], axis=0)
        m_old = m_sc[...]
        m_new = jnp.maximum(m_old, jnp.max(s, axis=-1, keepdims=True))
        p = jnp.exp2((s - jnp.concatenate([m_new] * (width // LANES), axis=1)).astype(BF16))
        alpha = jnp.exp2(m_old - m_new)
        ones = jnp.ones((width, dh), BF16)
        pv = jnp.concatenate([jnp.dot(p[g * head_rows:(g + 1) * head_rows], jnp.concatenate([vs[g], ones], axis=1),
                                      preferred_element_type=F32) for g in heads], axis=0)
        acc_sc[...] = jnp.concatenate([alpha, alpha], axis=1) * acc_sc[...] + pv
        m_sc[...] = m_new

    for c in range(kc_ref.shape[0]):
        online_softmax_step([kc_ref[c, gs[g], :] for g in heads],
                            [vc_ref[c * KEY_BLOCK:(c + 1) * KEY_BLOCK, gs[g]] for g in heads])

    def latent_chunk(c, carry):
        r0 = pl.multiple_of(c * chunk, chunk)
        online_softmax_step(
            [jnp.concatenate([kl_ref[c * blocks_per_chunk + j, gs[g], :] for j in range(blocks_per_chunk)], axis=1)
             for g in heads],
            [vl_ref[pl.ds(r0, chunk), gs[g]] for g in heads])
        return carry

    lax.fori_loop(0, n_lat_chunks, latent_chunk, 0)
    acc = acc_sc[...]
    o = acc[:, :dh] / acc[:, dh:]
    for hd in range(ATTN_KV_HEADS * group):
        o_ref[:, hd * dh:(hd + 1) * dh] = o[hd * tq:(hd + 1) * tq].astype(BF16)


def _attention(rows, q, kt, v, with_ctx):
    B, S, C = rows.B, rows.S, rows.C
    nq = q.shape[1]
    nkv = v.shape[1]
    assert S % KEY_BLOCK == 0 and C % KEY_BLOCK == 0 and rows.n_lat % C == 0
    tq = _pick_tile((256, 128), S, C)
    blocks_per_chunk = 2 if S % (2 * KEY_BLOCK) == 0 else 1
    lat_tiles, ctx_tiles = S // tq, C // tq
    tiles = lat_tiles + (ctx_tiles if with_ctx else 0)
    group = nq // nkv

    def q_map(b, i):
        return (jnp.where(i < lat_tiles, b * lat_tiles + i, rows.n_lat // tq + b * ctx_tiles + (i - lat_tiles)), 0)

    kt_ctx = pl.BlockSpec((C // KEY_BLOCK, nkv, KEY_BLOCK), lambda b, i: (rows.n_lat // C + b, 0, 0))
    kt_lat = pl.BlockSpec((S // KEY_BLOCK, nkv, KEY_BLOCK), lambda b, i: (b, 0, 0))
    v_ctx = pl.BlockSpec((C, nkv), lambda b, i: (rows.n_lat // C + b, 0))
    v_lat = pl.BlockSpec((S, nkv), lambda b, i: (b, 0))
    return pl.pallas_call(
        functools.partial(_attn_kernel, tq=tq, blocks_per_chunk=blocks_per_chunk, n_lat_tiles=lat_tiles),
        grid=(B, tiles),
        in_specs=[pl.BlockSpec((tq, nq), q_map), kt_ctx, v_ctx, kt_lat, v_lat],
        out_specs=pl.BlockSpec((tq, nq), q_map),
        out_shape=jax.ShapeDtypeStruct((rows.n if with_ctx else rows.n_lat, nq), BF16),
        scratch_shapes=[pltpu.VMEM((ATTN_KV_HEADS * group * tq, LANES), F32),
                        pltpu.VMEM((ATTN_KV_HEADS * group * tq, 2 * ATTN_HEAD_DIM), F32)],
        compiler_params=_cparams("parallel", "parallel"),
        name="attn_core",
    )(q, kt, v, kt, v)


def _proj_ln_kernel(*refs, first_tiles):
    n_seg = len(first_tiles) - 1
    o_refs = refs[:n_seg]
    w_ref, bias_ref, x_ref, gate_ref, g_ref, b_ref, out_ref = refs[n_seg:]
    i = pl.program_id(0)
    for seg, o_ref in enumerate(o_refs):
        @pl.when((i >= first_tiles[seg]) & (i < first_tiles[seg + 1]))
        def _():
            y = jnp.dot(o_ref[...], w_ref[...], preferred_element_type=F32) + bias_ref[...]
            out_ref[...] = _layer_norm(DEEPNORM_ALPHA * x_ref[...] + gate_ref[0] * y, g_ref[...], b_ref[...])


def _proj_ln(rows, o_segments, w, bias, xa, mod, ln_g, ln_b):
    D = rows.D
    kdim = w.shape[0]
    tm = rows.tile((512, 256, 128))
    first_tiles = [0]
    for o in o_segments:
        assert o.shape[0] % tm == 0
        first_tiles.append(first_tiles[-1] + o.shape[0] // tm)
    n_rows = first_tiles[-1] * tm

    def segment_spec(seg):
        first, count = first_tiles[seg], first_tiles[seg + 1] - first_tiles[seg]
        return pl.BlockSpec((tm, kdim), lambda i: (jnp.clip(i - first, 0, count - 1), 0))

    return pl.pallas_call(
        functools.partial(_proj_ln_kernel, first_tiles=tuple(first_tiles)),
        grid=(n_rows // tm,),
        in_specs=[segment_spec(seg) for seg in range(len(o_segments))] + [
            _resident(w.shape), _resident((1, D)), pl.BlockSpec((tm, D), lambda i: (i, 0)), rows.mod_spec(2, tm),
            _resident((1, D)), _resident((1, D))],
        out_specs=pl.BlockSpec((tm, D), lambda i: (i, 0)),
        out_shape=jax.ShapeDtypeStruct((n_rows, D), F32),
        compiler_params=_cparams("parallel"),
        name="proj_ln",
    )(*o_segments, w, bias.reshape(1, D), xa, mod, ln_g.reshape(1, D), ln_b.reshape(1, D))


def _ffn_kernel(x_ref, sh_ref, sc_ref, gate_ref, wg_ref, wu_ref, wd_ref, g_ref, b_ref, out_ref, h_sc, acc_sc):
    f = pl.program_id(1)

    @pl.when(f == 0)
    def _():
        h_sc[...] = (x_ref[...] * (1.0 + sc_ref[0]) + sh_ref[0]).astype(BF16)
        acc_sc[...] = jnp.zeros(acc_sc.shape, F32)

    h = h_sc[...]
    gt = jnp.dot(h, wg_ref[...], preferred_element_type=F32)
    up = jnp.dot(h, wu_ref[...], preferred_element_type=F32)
    act = (gt * jax.nn.sigmoid(gt) * up).astype(BF16)
    acc_sc[...] += jnp.dot(act, wd_ref[...], preferred_element_type=F32)

    @pl.when(f == pl.num_programs(1) - 1)
    def _():
        out_ref[...] = _layer_norm(DEEPNORM_ALPHA * x_ref[...] + gate_ref[0] * acc_sc[...], g_ref[...], b_ref[...])


def _ffn_ln(rows, xa, mod, w_gu, w_down, ln_g, ln_b, n_rows):
    D = rows.D
    F = w_down.shape[0]
    tm = rows.tile((512, 256, 128))
    tf = _pick_tile((1792, 512, 256, 128), F)
    nf = F // tf
    return pl.pallas_call(
        _ffn_kernel,
        grid=(n_rows // tm, nf),
        in_specs=[pl.BlockSpec((tm, D), lambda i, f: (i, 0)),
                  rows.mod_spec(3, tm), rows.mod_spec(4, tm), rows.mod_spec(5, tm),
                  pl.BlockSpec((D, tf), lambda i, f: (0, f)), pl.BlockSpec((D, tf), lambda i, f: (0, nf + f)),
                  pl.BlockSpec((tf, D), lambda i, f: (f, 0)), _resident((1, D)), _resident((1, D))],
        out_specs=pl.BlockSpec((tm, D), lambda i, f: (i, 0)),
        out_shape=jax.ShapeDtypeStruct((n_rows, D), F32),
        scratch_shapes=[pltpu.VMEM((tm, D), BF16), pltpu.VMEM((tm, D), F32)],
        compiler_params=_cparams("parallel", "arbitrary"),
        name="ffn_ln",
    )(xa, mod, mod, mod, w_gu, w_gu, w_down, ln_g.reshape(1, D), ln_b.reshape(1, D))


MOE_ROW_BLOCKS = (512, 256, 128)


def _router_kernel(x_ref, sh_ref, sc_ref, r_ref, h_ref, gates_ref, *, n_experts):
    h = x_ref[...] * (1.0 + sc_ref[0]) + sh_ref[0]
    h_ref[...] = h.astype(BF16)
    logits = jnp.dot(h, r_ref[...], preferred_element_type=F32, precision=lax.Precision.HIGHEST)
    lane = lax.broadcasted_iota(jnp.int32, logits.shape, 1)
    lowest = jnp.finfo(F32).min
    lg = jnp.where(lane < n_experts, logits, lowest)
    m1 = jnp.max(lg, axis=-1, keepdims=True)
    i1 = jnp.min(jnp.where(lg == m1, lane, LANES), axis=-1, keepdims=True)
    lg2 = jnp.where(lane == i1, lowest, lg)
    m2 = jnp.max(lg2, axis=-1, keepdims=True)
    i2 = jnp.min(jnp.where(lg2 == m2, lane, LANES), axis=-1, keepdims=True)
    e2 = jnp.exp(m2 - m1)
    denom = 1.0 + e2
    gates = jnp.where(lane == i1, 1.0 / denom, 0.0) + jnp.where(lane == i2, e2 / denom, 0.0)
    gates_ref[...] = gates.T[:n_experts, :]


def _router(rows, xa, mod, router, n_rows):
    D = rows.D
    n_experts = router.shape[1]
    tm = rows.tile((512, 256, 128))
    router_pad = jnp.pad(router, ((0, 0), (0, LANES - n_experts)))
    return pl.pallas_call(
        functools.partial(_router_kernel, n_experts=n_experts),
        grid=(n_rows // tm,),
        in_specs=[pl.BlockSpec((tm, D), lambda i: (i, 0)), rows.mod_spec(3, tm), rows.mod_spec(4, tm),
                  _resident((D, LANES))],
        out_specs=[pl.BlockSpec((tm, D), lambda i: (i, 0)), pl.BlockSpec((n_experts, tm), lambda i: (0, i))],
        out_shape=[jax.ShapeDtypeStruct((n_rows, D), BF16), jax.ShapeDtypeStruct((n_experts, n_rows), F32)],
        compiler_params=_cparams("parallel"),
        name="moe_router",
    )(xa, mod, mod, router_pad)


def _plan_kernel(g_ref, rank_ref, cnt_ref):
    n_experts, tile = g_ref.shape
    routed = g_ref[...] > 0.0
    ones = jnp.where(routed, 1.0, 0.0)
    before = jnp.where(lax.broadcasted_iota(jnp.int32, (LANES, LANES), 0)
                       < lax.broadcasted_iota(jnp.int32, (LANES, LANES), 1), 1.0, 0.0)
    seen = jnp.zeros((n_experts, 1), F32)
    for c in range(tile // LANES):
        cs = slice(c * LANES, (c + 1) * LANES)
        rank = jnp.dot(ones[:, cs], before, preferred_element_type=F32) + seen
        rank_ref[:, cs] = jnp.where(routed[:, cs], rank, -1.0).astype(jnp.int32)
        seen = seen + jnp.sum(ones[:, cs], axis=1, keepdims=True)
    cnt_ref[0] = jnp.broadcast_to(seen, (n_experts, LANES)).astype(jnp.int32)


def _moe_plan(gates_t, tile):
    n_experts, n_rows = gates_t.shape
    n_tiles = n_rows // tile
    rank, counts = pl.pallas_call(
        _plan_kernel,
        grid=(n_tiles,),
        in_specs=[pl.BlockSpec((n_experts, tile), lambda s: (0, s))],
        out_specs=[pl.BlockSpec((n_experts, tile), lambda s: (0, s)),
                   pl.BlockSpec((1, n_experts, LANES), lambda s: (s, 0, 0))],
        out_shape=[jax.ShapeDtypeStruct((n_experts, n_rows), jnp.int32),
                   jax.ShapeDtypeStruct((n_tiles, n_experts, LANES), jnp.int32)],
        compiler_params=_cparams("parallel"),
        name="moe_plan",
    )(gates_t)
    count = counts[:, :, 0].reshape(-1)
    smallest = MOE_ROW_BLOCKS[-1]
    return rank, (count + smallest - 1) // smallest * smallest


def _moe_kernel(nrows_ref, h_ref, rank_ref, gate_ref, wg_ref, wu_ref, wd_ref, out_ref, xg_sc, y_sc):
    s, e, f = pl.program_id(0), pl.program_id(1), pl.program_id(2)
    tile = h_ref.shape[0]
    n_rows = nrows_ref[s * pl.num_programs(1) + e]
    big = MOE_ROW_BLOCKS[0]

    @pl.when((e == 0) & (f == 0))
    def _():
        out_ref[...] = jnp.zeros(out_ref.shape, F32)

    def for_each_block(fn):
        def body(r, carry):
            fn(r * big, big)
            return carry

        lax.fori_loop(0, n_rows // big, body, 0)
        for size in MOE_ROW_BLOCKS[1:]:
            @pl.when(n_rows // size % 2 == 1)
            def _():
                fn(n_rows // (2 * size) * (2 * size), size)

    def rows_at(start, size):
        return pl.ds(pl.multiple_of(start, MOE_ROW_BLOCKS[-1]), size)

    def slot_matches(start, size):
        slot = lax.broadcasted_iota(jnp.int32, (size, tile), 0) + start
        return slot == rank_ref[pl.ds(e, 1), :]

    def gather(start, size):
        onehot = jnp.where(slot_matches(start, size), 1.0, 0.0).astype(BF16)
        xg_sc[rows_at(start, size), :] = jnp.dot(onehot, h_ref[...], preferred_element_type=F32).astype(BF16)
        y_sc[rows_at(start, size), :] = jnp.zeros((size, y_sc.shape[1]), F32)

    def expert(start, size):
        xb = xg_sc[rows_at(start, size), :]
        gt = jnp.dot(xb, wg_ref[0], preferred_element_type=F32)
        up = jnp.dot(xb, wu_ref[0], preferred_element_type=F32)
        act = (gt * jax.nn.sigmoid(gt) * up).astype(BF16)
        y_sc[rows_at(start, size), :] += jnp.dot(act, wd_ref[0], preferred_element_type=F32)

    def scatter(start, size):
        weighted = jnp.where(slot_matches(start, size), gate_ref[pl.ds(e, 1), :], 0.0).astype(BF16)
        out_ref[...] += lax.dot_general(weighted, y_sc[rows_at(start, size), :].astype(BF16),
                                        (((0,), (0,)), ((), ())), preferred_element_type=F32)

    @pl.when(f == 0)
    def _():
        for_each_block(gather)

    for_each_block(expert)

    @pl.when(f == pl.num_programs(2) - 1)
    def _():
        for_each_block(scatter)


def _moe(h, gates_t, w_gu, w_down):
    n_rows, D = h.shape
    E, F = w_down.shape[0], w_down.shape[1]
    tile = _pick_tile((2176, 2048, 1024, 512, 256, 128), n_rows)
    tf = _pick_tile((512, 256, 128), F)
    nf = F // tf
    rank, n_block_rows = _moe_plan(gates_t, tile)
    max_rows = -(-tile // MOE_ROW_BLOCKS[-1]) * MOE_ROW_BLOCKS[-1]
    grid_spec = pltpu.PrefetchScalarGridSpec(
        num_scalar_prefetch=1,
        grid=(n_rows // tile, E, nf),
        in_specs=[pl.BlockSpec((tile, D), lambda s, e, f, *_: (s, 0)),
                  pl.BlockSpec((E, tile), lambda s, e, f, *_: (0, s)),
                  pl.BlockSpec((E, tile), lambda s, e, f, *_: (0, s)),
                  pl.BlockSpec((1, D, tf), lambda s, e, f, *_: (e, 0, f)),
                  pl.BlockSpec((1, D, tf), lambda s, e, f, *_: (e, 0, nf + f)),
                  pl.BlockSpec((1, tf, D), lambda s, e, f, *_: (e, f, 0))],
        out_specs=pl.BlockSpec((tile, D), lambda s, e, f, *_: (s, 0)),
        scratch_shapes=[pltpu.VMEM((max_rows, D), BF16), pltpu.VMEM((max_rows, D), F32)],
    )
    return pl.pallas_call(
        _moe_kernel,
        grid_spec=grid_spec,
        out_shape=jax.ShapeDtypeStruct((n_rows, D), F32),
        compiler_params=_cparams("parallel", "arbitrary", "arbitrary"),
        name="moe_experts",
    )(n_block_rows, h, rank, gates_t, w_gu, w_gu, w_down)


def _residual_ln_kernel(x_ref, f_ref, gate_ref, g_ref, b_ref, out_ref):
    out_ref[...] = _layer_norm(DEEPNORM_ALPHA * x_ref[...] + gate_ref[0] * f_ref[...], g_ref[...], b_ref[...])


def _residual_ln(rows, xa, f, mod, part, ln_g, ln_b, n_rows):
    D = rows.D
    tm = rows.tile((512, 256, 128))
    return pl.pallas_call(
        _residual_ln_kernel,
        grid=(n_rows // tm,),
        in_specs=[pl.BlockSpec((tm, D), lambda i: (i, 0)), pl.BlockSpec((tm, D), lambda i: (i, 0)),
                  rows.mod_spec(part, tm), _resident((1, D)), _resident((1, D))],
        out_specs=pl.BlockSpec((tm, D), lambda i: (i, 0)),
        out_shape=jax.ShapeDtypeStruct((n_rows, D), F32),
        compiler_params=_cparams("parallel"),
        name="residual_ln",
    )(xa, f, mod, ln_g.reshape(1, D), ln_b.reshape(1, D))


def _moe_ln(rows, xa, mod, router, w_gu, w_down, ln_g, ln_b, n_rows):
    h, gates_t = _router(rows, xa, mod, router, n_rows)
    return _residual_ln(rows, xa, _moe(h, gates_t, w_gu, w_down), mod, 5, ln_g, ln_b, n_rows)


def _gmlp_kernel(x_ref, sh_ref, sc_ref, gate_ref, win_ref, bin_ref, vg_ref, vb_ref, ws_ref, bs_ref, wout_ref,
                 bout_ref, g_ref, b_ref, out_ref, gated_sc):
    tm = x_ref.shape[0]
    inner = wout_ref.shape[0]
    gdim = inner // CMLP_GROUPS
    x = x_ref[...]
    h = (x * (1.0 + sc_ref[0]) + sh_ref[0]).astype(BF16)
    v = jax.nn.gelu(jnp.dot(h, win_ref[:, inner:], preferred_element_type=F32) + bin_ref[:, inner:])
    v = _layer_norm(v, vg_ref[...], vb_ref[...]).astype(BF16)
    for gi in range(CMLP_GROUPS):
        cs = slice(gi * gdim, (gi + 1) * gdim)
        u = jax.nn.gelu(jnp.dot(h, win_ref[:, cs], preferred_element_type=F32) + bin_ref[:, cs])
        w_s = ws_ref[gi]
        bias = bs_ref[:, gi:gi + 1]
        for c in range(tm // CMLP_CHUNK):
            rs = slice(c * CMLP_CHUNK, (c + 1) * CMLP_CHUNK)
            mixed = jnp.dot(w_s, v[rs, cs], preferred_element_type=F32) + bias
            gated_sc[rs, cs] = (u[rs] * mixed).astype(BF16)
    y = jnp.dot(gated_sc[...], wout_ref[...], preferred_element_type=F32) + bout_ref[...]
    out_ref[...] = _layer_norm(DEEPNORM_ALPHA * x + gate_ref[0] * y, g_ref[...], b_ref[...])


def _gmlp_ln(rows, xa, mod, w_in, b_in, vg, vb, w_s, b_s, w_out, b_out, ln_g, ln_b, n_rows):
    D = rows.D
    inner = w_out.shape[0]
    tm = rows.tile((512, 256, 128))
    return pl.pallas_call(
        _gmlp_kernel,
        grid=(n_rows // tm,),
        in_specs=[pl.BlockSpec((tm, D), lambda i: (i, 0)), rows.mod_spec(0, tm), rows.mod_spec(1, tm),
                  rows.mod_spec(2, tm), _resident(w_in.shape), _resident((1, 2 * inner)), _resident((1, inner)),
                  _resident((1, inner)), _resident(w_s.shape), _resident((CMLP_CHUNK, CMLP_GROUPS)),
                  _resident(w_out.shape), _resident((1, D)), _resident((1, D)), _resident((1, D))],
        out_specs=pl.BlockSpec((tm, D), lambda i: (i, 0)),
        out_shape=jax.ShapeDtypeStruct((n_rows, D), F32),
        scratch_shapes=[pltpu.VMEM((tm, inner), BF16)],
        compiler_params=_cparams("parallel"),
        name="gmlp_ln",
    )(xa, mod, mod, mod, w_in, b_in.reshape(1, -1), vg.reshape(1, -1), vb.reshape(1, -1), w_s, b_s.T, w_out,
      b_out.reshape(1, D), ln_g.reshape(1, D), ln_b.reshape(1, D))


def _ret_proj_kernel(x_ref, sh_ref, sc_ref, w_ref, cos_ref, sin_ref, q_ref, k_ref, v_ref, sg_ref, *, nk, nv, dk):
    h = (x_ref[...] * (1.0 + sc_ref[0]) + sh_ref[0]).astype(BF16)
    cos, sin = cos_ref[...], sin_ref[...]
    k_scale = dk ** -0.5

    def rope_store(dst_ref, col0, scale):
        full = jnp.dot(h, w_ref[:, col0:col0 + nk], preferred_element_type=F32) * scale
        for j in range(nk // LANES):
            t = full[:, j * LANES:(j + 1) * LANES]
            ts = slice((j % (dk // LANES)) * LANES, (j % (dk // LANES) + 1) * LANES)
            dst_ref[:, j * LANES:(j + 1) * LANES] = (t * cos[:, ts] + _swap_pairs(t, dk // 4) * sin[:, ts]).astype(BF16)

    rope_store(q_ref, 0, 1.0)
    rope_store(k_ref, nk, k_scale)
    blk = 512
    for j in range(nv // blk):
        v_ref[:, j * blk:(j + 1) * blk] = jnp.dot(
            h, w_ref[:, 2 * nk + j * blk:2 * nk + (j + 1) * blk], preferred_element_type=F32).astype(BF16)
        gt = jnp.dot(h, w_ref[:, 2 * nk + nv + j * blk:2 * nk + nv + (j + 1) * blk], preferred_element_type=F32)
        sg_ref[:, j * blk:(j + 1) * blk] = (gt * jax.nn.sigmoid(gt)).astype(BF16)


def _ret_project(rows, xa, mod, w, cos, sin):
    D = rows.D
    nk = D
    nv = (w.shape[1] - 2 * nk) // 2
    dk = nk // RET_HEADS
    tm = rows.tile((512, 256))
    row_spec = lambda width: pl.BlockSpec((tm, width), lambda i: (i, 0))
    return pl.pallas_call(
        functools.partial(_ret_proj_kernel, nk=nk, nv=nv, dk=dk),
        grid=(rows.n // tm,),
        in_specs=[row_spec(D), rows.mod_spec(0, tm), rows.mod_spec(1, tm), _resident(w.shape),
                  rows.pos_spec(tm, dk), rows.pos_spec(tm, dk)],
        out_specs=[row_spec(nk), row_spec(nk), row_spec(nv), row_spec(nv)],
        out_shape=[jax.ShapeDtypeStruct((rows.n, nk), BF16), jax.ShapeDtypeStruct((rows.n, nk), BF16),
                   jax.ShapeDtypeStruct((rows.n, nv), BF16), jax.ShapeDtypeStruct((rows.n, nv), BF16)],
        compiler_params=_cparams("parallel"),
        name="ret_proj",
    )(xa, mod, mod, w, cos, sin)


RET_SCAN_CHUNK = 256


def _ret_scan_kernel(lg_ref, q_ref, k_ref, v_ref, sg_ref, sf_ref, sb_ref, o_ref, sf_out, sb_out,
                     state_sc, part_sc, inner_sc, cross_sc, weight_sc):
    ch = RET_SCAN_CHUNK
    n_chunks = q_ref.shape[0] // ch
    dk, dv = q_ref.shape[1], v_ref.shape[1]
    head = pl.program_id(1)

    chunk_decay = []
    for d in range(2):
        lg = lg_ref[d, head]
        ii = lax.broadcasted_iota(jnp.int32, (ch, ch), 0)
        jj = lax.broadcasted_iota(jnp.int32, (ch, ch), 1)
        dist = (ii - jj) if d == 0 else (jj - ii)
        inner_sc[d] = jnp.where(dist >= 0, jnp.exp(jnp.maximum(dist, 0).astype(F32) * lg), 0.0)
        row_v = lax.broadcasted_iota(jnp.int32, (ch, dv), 0)
        row_k = lax.broadcasted_iota(jnp.int32, (ch, dk), 0)
        cross_sc[d] = jnp.exp(((row_v if d == 0 else ch - 1 - row_v) + 1).astype(F32) * lg)
        weight_sc[d] = jnp.exp((ch - 1 - (row_k if d == 0 else ch - 1 - row_k)).astype(F32) * lg)
        chunk_decay.append(jnp.exp(jnp.full((1, 1), ch, F32) * lg))
    state_sc[0] = sf_ref[0, 0]
    state_sc[1] = sb_ref[0, 0]

    def visit(d, c):
        rows = pl.ds(pl.multiple_of(c * ch, ch), ch)
        q, k, v = q_ref[rows, :], k_ref[rows, :], v_ref[rows, :]
        scores = lax.dot_general(q, k, (((1,), (1,)), ((), ())), preferred_element_type=F32) * inner_sc[d]
        state = state_sc[d]
        out = (jnp.dot(scores.astype(BF16), v, preferred_element_type=F32)
               + jnp.dot(q, state.astype(BF16), preferred_element_type=F32) * cross_sc[d])
        kw = (k.astype(F32) * weight_sc[d]).astype(BF16)
        state_sc[d] = state * chunk_decay[d] + lax.dot_general(
            kw, v, (((0,), (0,)), ((), ())), preferred_element_type=F32)
        return rows, out

    def finish(rows, o):
        o = o * lax.rsqrt(jnp.mean(o * o, axis=-1, keepdims=True) + RMS_EPS)
        o_ref[rows, :] = (sg_ref[rows, :].astype(F32) * o).astype(BF16)

    def first_visits(t, carry):
        for d, c in ((0, t), (1, n_chunks - 1 - t)):
            rows, out = visit(d, c)
            part_sc[rows, :] = out
        return carry

    def second_visits(t, carry):
        for d, c in ((0, t), (1, n_chunks - 1 - t)):
            rows, out = visit(d, c)
            finish(rows, out + part_sc[rows, :])
        return carry

    lax.fori_loop(0, n_chunks // 2, first_visits, 0)
    if n_chunks % 2 == 1:
        rows, out_f = visit(0, n_chunks // 2)
        _, out_b = visit(1, n_chunks // 2)
        finish(rows, out_f + out_b)
    lax.fori_loop((n_chunks + 1) // 2, n_chunks, second_visits, 0)
    sf_out[0, 0] = state_sc[0]
    sb_out[0, 0] = state_sc[1]


def _ret_scan_segment(rows, log_gamma, q, k, v, sg, state_f, state_b, *, seg_len, first_block):
    B = rows.B
    dk = q.shape[1] // RET_HEADS
    dv = v.shape[1] // RET_HEADS
    ch = RET_SCAN_CHUNK
    assert seg_len % ch == 0
    kspec = pl.BlockSpec((seg_len, dk), lambda b, h: (first_block + b, h))
    vspec = pl.BlockSpec((seg_len, dv), lambda b, h: (first_block + b, h))
    sspec = pl.BlockSpec((1, 1, dk, dv), lambda b, h: (b, h, 0, 0))
    state_shape = jax.ShapeDtypeStruct((B, RET_HEADS, dk, dv), F32)
    return pl.pallas_call(
        _ret_scan_kernel,
        grid=(B, RET_HEADS),
        in_specs=[pl.BlockSpec(memory_space=pltpu.SMEM), kspec, kspec, vspec, vspec, sspec, sspec],
        out_specs=[pl.BlockSpec((seg_len, dv), lambda b, h: (b, h)), sspec, sspec],
        out_shape=[jax.ShapeDtypeStruct((B * seg_len, v.shape[1]), BF16), state_shape, state_shape],
        scratch_shapes=[pltpu.VMEM((2, dk, dv), F32), pltpu.VMEM((seg_len, dv), F32),
                        pltpu.VMEM((2, ch, ch), F32), pltpu.VMEM((2, ch, dv), F32), pltpu.VMEM((2, ch, dk), F32)],
        compiler_params=_cparams("parallel", "parallel"),
        name="ret_scan",
    )(log_gamma, q, k, v, sg, state_f, state_b)


def _retention(rows, log_gamma, q, k, v, sg):
    B, S, C = rows.B, rows.S, rows.C
    assert rows.n_lat % C == 0
    dk, dv = q.shape[1] // RET_HEADS, v.shape[1] // RET_HEADS
    zeros = jnp.zeros((B, RET_HEADS, dk, dv), F32)
    o_ctx, state_f, state_b = _ret_scan_segment(rows, log_gamma, q, k, v, sg, zeros, zeros,
                                                seg_len=C, first_block=rows.n_lat // C)
    o_lat, _, _ = _ret_scan_segment(rows, log_gamma, q, k, v, sg, state_f, state_b, seg_len=S, first_block=0)
    return [o_lat, o_ctx]


def kernel(x, c, ctx, c_ctx, l0_ada_w, l0_ada_b, l0_ln1_g, l0_ln1_b, l0_ln2_g, l0_ln2_b, l0_attn_wqkv, l0_attn_q_norm, l0_attn_k_norm, l0_attn_wo, l0_ffn_w_gu, l0_ffn_w_down, l1_ada_w, l1_ada_b, l1_ln1_g, l1_ln1_b, l1_ln2_g, l1_ln2_b, l1_cmlp_w_in, l1_cmlp_b_in, l1_cmlp_v_norm_g, l1_cmlp_v_norm_b, l1_cmlp_w_s, l1_cmlp_b_s, l1_cmlp_w_out, l1_cmlp_b_out, l1_moe_router, l1_moe_w_gu, l1_moe_w_down, l2_ada_w, l2_ada_b, l2_ln1_g, l2_ln1_b, l2_ln2_g, l2_ln2_b, l2_ret_wqkvg, l2_ret_decay, l2_ret_wo, l2_ffn_w_gu, l2_ffn_w_down, l3_ada_w, l3_ada_b, l3_ln1_g, l3_ln1_b, l3_ln2_g, l3_ln2_b, l3_attn_wqkv, l3_attn_q_norm, l3_attn_k_norm, l3_attn_wo, l3_moe_router, l3_moe_w_gu, l3_moe_w_down):
    B, S, D = x.shape
    C = ctx.shape[1]
    rows = _Rows(B, S, C, D)
    assert S % GRID_W == 0 and C % RET_CHUNK == 0 and S % RET_CHUNK == 0

    xa = jnp.concatenate([x.reshape(rows.n_lat, D), ctx.reshape(rows.n_ctx, D)], axis=0)
    n_cond = -(-(B + 1) // 8) * 8
    cond = jnp.concatenate([c, c_ctx[None, :], jnp.zeros((n_cond - B - 1, D), F32)], axis=0)

    def modulation(ada_w, ada_b):
        return _ada(cond, ada_w, ada_b).reshape(n_cond * 6, 1, D)

    pad = rows.tile((512, 256))
    attn_cos, attn_sin = _rope_tables(S, ATTN_HEAD_DIM, pad)
    ret_cos, ret_sin = _rope_tables(S, D // RET_HEADS, pad)
    zero_bias = jnp.zeros((D,), F32)
    bf = lambda w: w.astype(BF16)

    def attention_layer(xa, mod, wqkv, qg, kg, wo, ln_g, ln_b, with_ctx):
        q, k, v = _attn_project(rows, xa, mod, bf(wqkv), qg, kg, attn_cos, attn_sin)
        o = _attention(rows, q, k, v, with_ctx)
        return _proj_ln(rows, [o], bf(wo), zero_bias, xa, mod, ln_g, ln_b)

    def retention_layer(xa, mod, wqkvg, decay, wo, ln_g, ln_b):
        log_gamma = -jnp.exp(decay.astype(F32))
        q, k, v, sg = _ret_project(rows, xa, mod, bf(wqkvg), ret_cos, ret_sin)
        o_segments = _retention(rows, log_gamma, q, k, v, sg)
        return _proj_ln(rows, o_segments, bf(wo), zero_bias, xa, mod, ln_g, ln_b)

    mod = modulation(l0_ada_w, l0_ada_b)
    xa = attention_layer(xa, mod, l0_attn_wqkv, l0_attn_q_norm, l0_attn_k_norm, l0_attn_wo, l0_ln1_g, l0_ln1_b, True)
    xa = _ffn_ln(rows, xa, mod, bf(l0_ffn_w_gu), bf(l0_ffn_w_down), l0_ln2_g, l0_ln2_b, rows.n)

    mod = modulation(l1_ada_w, l1_ada_b)
    xa = _gmlp_ln(rows, xa, mod, bf(l1_cmlp_w_in), l1_cmlp_b_in, l1_cmlp_v_norm_g, l1_cmlp_v_norm_b,
                  bf(l1_cmlp_w_s), l1_cmlp_b_s, bf(l1_cmlp_w_out), l1_cmlp_b_out, l1_ln1_g, l1_ln1_b, rows.n)
    xa = _moe_ln(rows, xa, mod, l1_moe_router, bf(l1_moe_w_gu), bf(l1_moe_w_down), l1_ln2_g, l1_ln2_b, rows.n)

    mod = modulation(l2_ada_w, l2_ada_b)
    xa = retention_layer(xa, mod, l2_ret_wqkvg, l2_ret_decay, l2_ret_wo, l2_ln1_g, l2_ln1_b)
    xa = _ffn_ln(rows, xa, mod, bf(l2_ffn_w_gu), bf(l2_ffn_w_down), l2_ln2_g, l2_ln2_b, rows.n)

    mod = modulation(l3_ada_w, l3_ada_b)
    xl = attention_layer(xa, mod, l3_attn_wqkv, l3_attn_q_norm, l3_attn_k_norm, l3_attn_wo, l3_ln1_g, l3_ln1_b, False)
    xl = _moe_ln(rows, xl, mod, l3_moe_router, bf(l3_moe_w_gu), bf(l3_moe_w_down), l3_ln2_g, l3_ln2_b, rows.n_lat)
    return xl.reshape(B, S, D)
```

```python
import functools

import jax
import jax.numpy as jnp
from jax import lax
from jax.experimental import pallas as pl
from jax.experimental.pallas import tpu as pltpu

F32 = jnp.float32
BF16 = jnp.bfloat16

DEPTH = 4
GRID_W = 64
ROPE_THETA = 10000.0
DEEPNORM_ALPHA = (2 * DEPTH) ** 0.25
LN_EPS = 1e-5
RMS_EPS = 1e-6
ATTN_HEAD_DIM = 128
ATTN_KV_HEADS = 2
CMLP_CHUNK = 128
CMLP_GROUPS = 8
RET_HEADS = 4
RET_CHUNK = 128
MOE_EXPERTS = 8

LANES = 128
VMEM_LIMIT_BYTES = 56 * 1024 * 1024
NEG_BIG = -1e30


def _cparams(*sem):
    return pltpu.CompilerParams(dimension_semantics=sem, vmem_limit_bytes=VMEM_LIMIT_BYTES)


def _resident(shape):
    nd = len(shape)
    return pl.BlockSpec(shape, lambda *_: (0,) * nd)


def _layer_norm(z, g, b):
    mu = jnp.mean(z, axis=-1, keepdims=True)
    zc = z - mu
    var = jnp.mean(zc * zc, axis=-1, keepdims=True)
    return zc * lax.rsqrt(var + LN_EPS) * g + b


def _pick_tile(candidates, *extents):
    for t in candidates:
        if all(e % t == 0 for e in extents):
            return t
    raise ValueError(f"no tile in {candidates} divides {extents}")


class _Rows:
    def __init__(self, batch, seq, ctx_len, dim):
        self.B, self.S, self.C, self.D = batch, seq, ctx_len, dim
        self.n_lat = batch * seq
        self.n_ctx = batch * ctx_len
        self.n = self.n_lat + self.n_ctx

    def tile(self, candidates):
        return _pick_tile(candidates, self.S, self.n_ctx)

    def mod_spec(self, part, tm):
        S, B = self.S, self.B
        return pl.BlockSpec((1, 1, self.D), lambda i, *_: (jnp.minimum(i * tm // S, B) * 6 + part, 0, 0))

    def pos_spec(self, tm, width):
        n_lat_tiles, per_seq = self.n_lat // tm, self.S // tm
        return pl.BlockSpec((tm, width), lambda i, *_: (jnp.where(i < n_lat_tiles, i % per_seq, per_seq), 0))


def _dot_3pass(a, b):
    def split(t):
        hi = t.astype(BF16)
        return hi, (t - hi.astype(F32)).astype(BF16)

    (a_hi, a_lo), (b_hi, b_lo) = split(a), split(b)
    rows = a.shape[0]
    by_b_hi = jnp.dot(jnp.concatenate([a_hi, a_lo], axis=0), b_hi, preferred_element_type=F32)
    return by_b_hi[:rows] + by_b_hi[rows:] + jnp.dot(a_hi, b_lo, preferred_element_type=F32)


def _ada_kernel(c_ref, w_ref, b_ref, o_ref):
    c = c_ref[...]
    s = c * jax.nn.sigmoid(c)
    o_ref[...] = _dot_3pass(s, w_ref[...]) + b_ref[...]


def _ada(cond, w, b):
    r, d = cond.shape
    n = w.shape[1]
    tn = _pick_tile((1536, 1024, 512, 128), n)
    return pl.pallas_call(
        _ada_kernel,
        grid=(n // tn,),
        in_specs=[_resident((r, d)), pl.BlockSpec((d, tn), lambda j: (0, j)), pl.BlockSpec((1, tn), lambda j: (0, j))],
        out_specs=pl.BlockSpec((r, tn), lambda j: (0, j)),
        out_shape=jax.ShapeDtypeStruct((r, n), F32),
        compiler_params=_cparams("parallel"),
        name="ada",
    )(cond, w, b.reshape(1, n))


def _rope_tables(seq, head_dim, pad_rows):
    nf = head_dim // 4
    inv_freq = ROPE_THETA ** (-jnp.arange(nf, dtype=F32) / nf)
    t = jnp.arange(seq, dtype=jnp.int32)
    row = (t // GRID_W).astype(F32)[:, None] * inv_freq
    col = (t % GRID_W).astype(F32)[:, None] * inv_freq
    cos = jnp.concatenate([jnp.cos(row), jnp.cos(row), jnp.cos(col), jnp.cos(col)], axis=-1)
    sin = jnp.concatenate([-jnp.sin(row), jnp.sin(row), -jnp.sin(col), jnp.sin(col)], axis=-1)
    cos = jnp.concatenate([cos, jnp.ones((pad_rows, head_dim), F32)], axis=0)
    sin = jnp.concatenate([sin, jnp.zeros((pad_rows, head_dim), F32)], axis=0)
    return cos, sin


def _swap_pairs(y, half):
    if 2 * half == LANES:
        return pltpu.roll(y, half, 1)
    lane = lax.broadcasted_iota(jnp.int32, y.shape, 1)
    return jnp.where(lane % (2 * half) < half, pltpu.roll(y, LANES - half, 1), pltpu.roll(y, half, 1))


KEY_BLOCK = 256
LOG2_E = 1.4426950408889634


def _qkv_kernel(x_ref, sh_ref, sc_ref, w_ref, qg_ref, kg_ref, cos_ref, sin_ref, q_ref, kt_ref, v_ref, *, nq, nkv):
    dh = ATTN_HEAD_DIM
    h = (x_ref[...] * (1.0 + sc_ref[0]) + sh_ref[0]).astype(BF16)
    cos, sin = cos_ref[...], sin_ref[...]
    scale = dh ** -0.5 * LOG2_E

    def normed_rope(t, g):
        y = t * lax.rsqrt(jnp.mean(t * t, axis=-1, keepdims=True) + RMS_EPS) * g
        return y * cos + _swap_pairs(y, dh // 4) * sin

    t = jnp.dot(h, w_ref[...], preferred_element_type=F32)
    for hd in range(nq // dh):
        q_ref[:, hd * dh:(hd + 1) * dh] = (normed_rope(t[:, hd * dh:(hd + 1) * dh], qg_ref[...]) * scale).astype(BF16)
    k = jnp.concatenate([normed_rope(t[:, nq + hd * dh:nq + (hd + 1) * dh], kg_ref[...]) for hd in range(nkv // dh)],
                        axis=1)
    for j in range(kt_ref.shape[0]):
        kt_ref[j] = k[j * KEY_BLOCK:(j + 1) * KEY_BLOCK, :].T.astype(BF16)
    v_ref[...] = t[:, nq + nkv:].astype(BF16)


def _attn_project(rows, xa, mod, w, qg, kg, cos, sin):
    D = rows.D
    nkv = ATTN_KV_HEADS * ATTN_HEAD_DIM
    nq = w.shape[1] - 2 * nkv
    tm = rows.tile((512, 256))
    row_spec = lambda width: pl.BlockSpec((tm, width), lambda i: (i, 0))
    return pl.pallas_call(
        functools.partial(_qkv_kernel, nq=nq, nkv=nkv),
        grid=(rows.n // tm,),
        in_specs=[row_spec(D), rows.mod_spec(0, tm), rows.mod_spec(1, tm), _resident(w.shape),
                  _resident((1, ATTN_HEAD_DIM)), _resident((1, ATTN_HEAD_DIM)),
                  rows.pos_spec(tm, ATTN_HEAD_DIM), rows.pos_spec(tm, ATTN_HEAD_DIM)],
        out_specs=[row_spec(nq), pl.BlockSpec((tm // KEY_BLOCK, nkv, KEY_BLOCK), lambda i: (i, 0, 0)), row_spec(nkv)],
        out_shape=[jax.ShapeDtypeStruct((rows.n, nq), BF16),
                   jax.ShapeDtypeStruct((rows.n // KEY_BLOCK, nkv, KEY_BLOCK), BF16),
                   jax.ShapeDtypeStruct((rows.n, nkv), BF16)],
        compiler_params=_cparams("parallel"),
        name="attn_qkv",
    )(xa, mod, mod, w, qg.reshape(1, -1), kg.reshape(1, -1), cos, sin)


def _attn_kernel(q_ref, kc_ref, vc_ref, kl_ref, vl_ref, o_ref, m_sc, acc_sc, *, tq, blocks_per_chunk, n_lat_tiles):
    dh = ATTN_HEAD_DIM
    group = q_ref.shape[1] // dh // ATTN_KV_HEADS
    is_latent = pl.program_id(1) < n_lat_tiles
    n_lat_chunks = jnp.where(is_latent, kl_ref.shape[0] // blocks_per_chunk, 0)
    chunk = blocks_per_chunk * KEY_BLOCK

    heads = range(ATTN_KV_HEADS)
    head_rows = group * tq
    gs = [slice(g * dh, (g + 1) * dh) for g in heads]
    qs = [jnp.concatenate([q_ref[:, (g * group + j) * dh:(g * group + j + 1) * dh] for j in range(group)], axis=0)
          for g in heads]
    m_sc[...] = jnp.full(m_sc.shape, NEG_BIG, F32)
    acc_sc[...] = jnp.zeros(acc_sc.shape, F32)

    def online_softmax_step(kts, vs):
        width = kts[0].shape[1]
        s = jnp.concatenate([jnp.dot(qs[g], kts[g], preferred_element_type=F32) for g in heads], axis=0)
        m_old = m_sc[...]
        m_new = jnp.maximum(m_old, jnp.max(s, axis=-1, keepdims=True))
        p = jnp.exp2((s - jnp.concatenate([m_new] * (width // LANES), axis=1)).astype(BF16))
        alpha = jnp.exp2(m_old - m_new)
        ones = jnp.ones((width, dh), BF16)
        pv = jnp.concatenate([jnp.dot(p[g * head_rows:(g + 1) * head_rows], jnp.concatenate([vs[g], ones], axis=1),
                                      preferred_element_type=F32) for g in heads], axis=0)
        acc_sc[...] = jnp.concatenate([alpha, alpha], axis=1) * acc_sc[...] + pv
        m_sc[...] = m_new

    for c in range(kc_ref.shape[0]):
        online_softmax_step([kc_ref[c, gs[g], :] for g in heads],
                            [vc_ref[c * KEY_BLOCK:(c + 1) * KEY_BLOCK, gs[g]] for g in heads])

    def latent_chunk(c, carry):
        r0 = pl.multiple_of(c * chunk, chunk)
        online_softmax_step(
            [jnp.concatenate([kl_ref[c * blocks_per_chunk + j, gs[g], :] for j in range(blocks_per_chunk)], axis=1)
             for g in heads],
            [vl_ref[pl.ds(r0, chunk), gs[g]] for g in heads])
        return carry

    lax.fori_loop(0, n_lat_chunks, latent_chunk, 0)
    acc = acc_sc[...]
    o = acc[:, :dh] / acc[:, dh:]
    for hd in range(ATTN_KV_HEADS * group):
        o_ref[:, hd * dh:(hd + 1) * dh] = o[hd * tq:(hd + 1) * tq].astype(BF16)


def _attention(rows, q, kt, v, with_ctx):
    B, S, C = rows.B, rows.S, rows.C
    nq = q.shape[1]
    nkv = v.shape[1]
    assert S % KEY_BLOCK == 0 and C % KEY_BLOCK == 0 and rows.n_lat % C == 0
    tq = _pick_tile((256, 128), S, C)
    blocks_per_chunk = 2 if S % (2 * KEY_BLOCK) == 0 else 1
    lat_tiles, ctx_tiles = S // tq, C // tq
    tiles = lat_tiles + (ctx_tiles if with_ctx else 0)
    group = nq // nkv

    def q_map(b, i):
        return (jnp.where(i < lat_tiles, b * lat_tiles + i, rows.n_lat // tq + b * ctx_tiles + (i - lat_tiles)), 0)

    kt_ctx = pl.BlockSpec((C // KEY_BLOCK, nkv, KEY_BLOCK), lambda b, i: (rows.n_lat // C + b, 0, 0))
    kt_lat = pl.BlockSpec((S // KEY_BLOCK, nkv, KEY_BLOCK), lambda b, i: (b, 0, 0))
    v_ctx = pl.BlockSpec((C, nkv), lambda b, i: (rows.n_lat // C + b, 0))
    v_lat = pl.BlockSpec((S, nkv), lambda b, i: (b, 0))
    return pl.pallas_call(
        functools.partial(_attn_kernel, tq=tq, blocks_per_chunk=blocks_per_chunk, n_lat_tiles=lat_tiles),
        grid=(B, tiles),
        in_specs=[pl.BlockSpec((tq, nq), q_map), kt_ctx, v_ctx, kt_lat, v_lat],
        out_specs=pl.BlockSpec((tq, nq), q_map),
        out_shape=jax.ShapeDtypeStruct((rows.n if with_ctx else rows.n_lat, nq), BF16),
        scratch_shapes=[pltpu.VMEM((ATTN_KV_HEADS * group * tq, LANES), F32),
                        pltpu.VMEM((ATTN_KV_HEADS * group * tq, 2 * ATTN_HEAD_DIM), F32)],
        compiler_params=_cparams("parallel", "parallel"),
        name="attn_core",
    )(q, kt, v, kt, v)


def _proj_ln_kernel(*refs, first_tiles):
    n_seg = len(first_tiles) - 1
    o_refs = refs[:n_seg]
    w_ref, bias_ref, x_ref, gate_ref, g_ref, b_ref, out_ref = refs[n_seg:]
    i = pl.program_id(0)
    for seg, o_ref in enumerate(o_refs):
        @pl.when((i >= first_tiles[seg]) & (i < first_tiles[seg + 1]))
        def _():
            y = jnp.dot(o_ref[...], w_ref[...], preferred_element_type=F32) + bias_ref[...]
            out_ref[...] = _layer_norm(DEEPNORM_ALPHA * x_ref[...] + gate_ref[0] * y, g_ref[...], b_ref[...])


def _proj_ln(rows, o_segments, w, bias, xa, mod, ln_g, ln_b):
    D = rows.D
    kdim = w.shape[0]
    tm = rows.tile((512, 256, 128))
    first_tiles = [0]
    for o in o_segments:
        assert o.shape[0] % tm == 0
        first_tiles.append(first_tiles[-1] + o.shape[0] // tm)
    n_rows = first_tiles[-1] * tm

    def segment_spec(seg):
        first, count = first_tiles[seg], first_tiles[seg + 1] - first_tiles[seg]
        return pl.BlockSpec((tm, kdim), lambda i: (jnp.clip(i - first, 0, count - 1), 0))

    return pl.pallas_call(
        functools.partial(_proj_ln_kernel, first_tiles=tuple(first_tiles)),
        grid=(n_rows // tm,),
        in_specs=[segment_spec(seg) for seg in range(len(o_segments))] + [
            _resident(w.shape), _resident((1, D)), pl.BlockSpec((tm, D), lambda i: (i, 0)), rows.mod_spec(2, tm),
            _resident((1, D)), _resident((1, D))],
        out_specs=pl.BlockSpec((tm, D), lambda i: (i, 0)),
        out_shape=jax.ShapeDtypeStruct((n_rows, D), F32),
        compiler_params=_cparams("parallel"),
        name="proj_ln",
    )(*o_segments, w, bias.reshape(1, D), xa, mod, ln_g.reshape(1, D), ln_b.reshape(1, D))


def _ffn_kernel(x_ref, sh_ref, sc_ref, gate_ref, wg_ref, wu_ref, wd_ref, g_ref, b_ref, out_ref, h_sc, acc_sc):
    f = pl.program_id(1)

    @pl.when(f == 0)
    def _():
        h_sc[...] = (x_ref[...] * (1.0 + sc_ref[0]) + sh_ref[0]).astype(BF16)
        acc_sc[...] = jnp.zeros(acc_sc.shape, F32)

    h = h_sc[...]
    gt = jnp.dot(h, wg_ref[...], preferred_element_type=F32)
    up = jnp.dot(h, wu_ref[...], preferred_element_type=F32)
    act = (gt * jax.nn.sigmoid(gt) * up).astype(BF16)
    acc_sc[...] += jnp.dot(act, wd_ref[...], preferred_element_type=F32)

    @pl.when(f == pl.num_programs(1) - 1)
    def _():
        out_ref[...] = _layer_norm(DEEPNORM_ALPHA * x_ref[...] + gate_ref[0] * acc_sc[...], g_ref[...], b_ref[...])


def _ffn_ln(rows, xa, mod, w_gu, w_down, ln_g, ln_b, n_rows):
    D = rows.D
    F = w_down.shape[0]
    tm = rows.tile((512, 256, 128))
    tf = _pick_tile((1792, 512, 256, 128), F)
    nf = F // tf
    return pl.pallas_call(
        _ffn_kernel,
        grid=(n_rows // tm, nf),
        in_specs=[pl.BlockSpec((tm, D), lambda i, f: (i, 0)),
                  rows.mod_spec(3, tm), rows.mod_spec(4, tm), rows.mod_spec(5, tm),
                  pl.BlockSpec((D, tf), lambda i, f: (0, f)), pl.BlockSpec((D, tf), lambda i, f: (0, nf + f)),
                  pl.BlockSpec((tf, D), lambda i, f: (f, 0)), _resident((1, D)), _resident((1, D))],
        out_specs=pl.BlockSpec((tm, D), lambda i, f: (i, 0)),
        out_shape=jax.ShapeDtypeStruct((n_rows, D), F32),
        scratch_shapes=[pltpu.VMEM((tm, D), BF16), pltpu.VMEM((tm, D), F32)],
        compiler_params=_cparams("parallel", "arbitrary"),
        name="ffn_ln",
    )(xa, mod, mod, mod, w_gu, w_gu, w_down, ln_g.reshape(1, D), ln_b.reshape(1, D))


MOE_ROW_BLOCKS = (512, 256, 128)


def _router_kernel(x_ref, sh_ref, sc_ref, r_ref, h_ref, gates_ref, *, n_experts):
    h = x_ref[...] * (1.0 + sc_ref[0]) + sh_ref[0]
    h_ref[...] = h.astype(BF16)
    logits = _dot_3pass(h, r_ref[...])
    lane = lax.broadcasted_iota(jnp.int32, logits.shape, 1)
    lowest = jnp.finfo(F32).min
    lg = jnp.where(lane < n_experts, logits, lowest)
    m1 = jnp.max(lg, axis=-1, keepdims=True)
    i1 = jnp.min(jnp.where(lg == m1, lane, LANES), axis=-1, keepdims=True)
    lg2 = jnp.where(lane == i1, lowest, lg)
    m2 = jnp.max(lg2, axis=-1, keepdims=True)
    i2 = jnp.min(jnp.where(lg2 == m2, lane, LANES), axis=-1, keepdims=True)
    e2 = jnp.exp(m2 - m1)
    denom = 1.0 + e2
    gates = jnp.where(lane == i1, 1.0 / denom, 0.0) + jnp.where(lane == i2, e2 / denom, 0.0)
    gates_ref[...] = gates.T[:n_experts, :]


def _router(rows, xa, mod, router, n_rows):
    D = rows.D
    n_experts = router.shape[1]
    tm = rows.tile((512, 256, 128))
    router_pad = jnp.pad(router, ((0, 0), (0, LANES - n_experts)))
    return pl.pallas_call(
        functools.partial(_router_kernel, n_experts=n_experts),
        grid=(n_rows // tm,),
        in_specs=[pl.BlockSpec((tm, D), lambda i: (i, 0)), rows.mod_spec(3, tm), rows.mod_spec(4, tm),
                  _resident((D, LANES))],
        out_specs=[pl.BlockSpec((tm, D), lambda i: (i, 0)), pl.BlockSpec((n_experts, tm), lambda i: (0, i))],
        out_shape=[jax.ShapeDtypeStruct((n_rows, D), BF16), jax.ShapeDtypeStruct((n_experts, n_rows), F32)],
        compiler_params=_cparams("parallel"),
        name="moe_router",
    )(xa, mod, mod, router_pad)


def _plan_kernel(g_ref, rank_ref, cnt_ref):
    n_experts, tile = g_ref.shape
    routed = g_ref[...] > 0.0
    ones = jnp.where(routed, 1.0, 0.0)
    before = jnp.where(lax.broadcasted_iota(jnp.int32, (LANES, LANES), 0)
                       < lax.broadcasted_iota(jnp.int32, (LANES, LANES), 1), 1.0, 0.0)
    seen = jnp.zeros((n_experts, 1), F32)
    for c in range(tile // LANES):
        cs = slice(c * LANES, (c + 1) * LANES)
        rank = jnp.dot(ones[:, cs], before, preferred_element_type=F32) + seen
        rank_ref[:, cs] = jnp.where(routed[:, cs], rank, -1.0).astype(jnp.int32)
        seen = seen + jnp.sum(ones[:, cs], axis=1, keepdims=True)
    cnt_ref[0] = jnp.broadcast_to(seen, (n_experts, LANES)).astype(jnp.int32)


def _moe_plan(gates_t, tile):
    n_experts, n_rows = gates_t.shape
    n_tiles = n_rows // tile
    rank, counts = pl.pallas_call(
        _plan_kernel,
        grid=(n_tiles,),
        in_specs=[pl.BlockSpec((n_experts, tile), lambda s: (0, s))],
        out_specs=[pl.BlockSpec((n_experts, tile), lambda s: (0, s)),
                   pl.BlockSpec((1, n_experts, LANES), lambda s: (s, 0, 0))],
        out_shape=[jax.ShapeDtypeStruct((n_experts, n_rows), jnp.int32),
                   jax.ShapeDtypeStruct((n_tiles, n_experts, LANES), jnp.int32)],
        compiler_params=_cparams("parallel"),
        name="moe_plan",
    )(gates_t)
    count = counts[:, :, 0].reshape(-1)
    smallest = MOE_ROW_BLOCKS[-1]
    return rank, (count + smallest - 1) // smallest * smallest


def _moe_kernel(nrows_ref, h_ref, rank_ref, gate_ref, wg_ref, wu_ref, wd_ref, out_ref, xg_sc, y_sc):
    s, e, f = pl.program_id(0), pl.program_id(1), pl.program_id(2)
    tile = h_ref.shape[0]
    n_rows = nrows_ref[s * pl.num_programs(1) + e]
    big = MOE_ROW_BLOCKS[0]

    @pl.when((e == 0) & (f == 0))
    def _():
        out_ref[...] = jnp.zeros(out_ref.shape, F32)

    def for_each_block(fn):
        def body(r, carry):
            fn(r * big, big)
            return carry

        lax.fori_loop(0, n_rows // big, body, 0)
        for size in MOE_ROW_BLOCKS[1:]:
            @pl.when(n_rows // size % 2 == 1)
            def _():
                fn(n_rows // (2 * size) * (2 * size), size)

    def rows_at(start, size):
        return pl.ds(pl.multiple_of(start, MOE_ROW_BLOCKS[-1]), size)

    def slot_matches(start, size):
        slot = lax.broadcasted_iota(jnp.int32, (size, tile), 0) + start
        return slot == rank_ref[pl.ds(e, 1), :]

    def gather(start, size):
        onehot = jnp.where(slot_matches(start, size), 1.0, 0.0).astype(BF16)
        xg_sc[rows_at(start, size), :] = jnp.dot(onehot, h_ref[...], preferred_element_type=F32).astype(BF16)
        y_sc[rows_at(start, size), :] = jnp.zeros((size, y_sc.shape[1]), F32)

    def expert(start, size):
        xb = xg_sc[rows_at(start, size), :]
        gt = jnp.dot(xb, wg_ref[0], preferred_element_type=F32)
        up = jnp.dot(xb, wu_ref[0], preferred_element_type=F32)
        act = (gt * jax.nn.sigmoid(gt) * up).astype(BF16)
        y_sc[rows_at(start, size), :] += jnp.dot(act, wd_ref[0], preferred_element_type=F32)

    def scatter(start, size):
        weighted = jnp.where(slot_matches(start, size), gate_ref[pl.ds(e, 1), :], 0.0).astype(BF16)
        out_ref[...] += lax.dot_general(weighted, y_sc[rows_at(start, size), :].astype(BF16),
                                        (((0,), (0,)), ((), ())), preferred_element_type=F32)

    @pl.when(f == 0)
    def _():
        for_each_block(gather)

    for_each_block(expert)

    @pl.when(f == pl.num_programs(2) - 1)
    def _():
        for_each_block(scatter)


def _moe(h, gates_t, w_gu, w_down):
    n_rows, D = h.shape
    E, F = w_down.shape[0], w_down.shape[1]
    tile = _pick_tile((2176, 2048, 1024, 512, 256, 128), n_rows)
    tf = _pick_tile((512, 256, 128), F)
    nf = F // tf
    rank, n_block_rows = _moe_plan(gates_t, tile)
    max_rows = -(-tile // MOE_ROW_BLOCKS[-1]) * MOE_ROW_BLOCKS[-1]
    grid_spec = pltpu.PrefetchScalarGridSpec(
        num_scalar_prefetch=1,
        grid=(n_rows // tile, E, nf),
        in_specs=[pl.BlockSpec((tile, D), lambda s, e, f, *_: (s, 0)),
                  pl.BlockSpec((E, tile), lambda s, e, f, *_: (0, s)),
                  pl.BlockSpec((E, tile), lambda s, e, f, *_: (0, s)),
                  pl.BlockSpec((1, D, tf), lambda s, e, f, *_: (e, 0, f)),
                  pl.BlockSpec((1, D, tf), lambda s, e, f, *_: (e, 0, nf + f)),
                  pl.BlockSpec((1, tf, D), lambda s, e, f, *_: (e, f, 0))],
        out_specs=pl.BlockSpec((tile, D), lambda s, e, f, *_: (s, 0)),
        scratch_shapes=[pltpu.VMEM((max_rows, D), BF16), pltpu.VMEM((max_rows, D), F32)],
    )
    return pl.pallas_call(
        _moe_kernel,
        grid_spec=grid_spec,
        out_shape=jax.ShapeDtypeStruct((n_rows, D), F32),
        compiler_params=_cparams("parallel", "arbitrary", "arbitrary"),
        name="moe_experts",
    )(n_block_rows, h, rank, gates_t, w_gu, w_gu, w_down)


def _residual_ln_kernel(x_ref, f_ref, gate_ref, g_ref, b_ref, out_ref):
    out_ref[...] = _layer_norm(DEEPNORM_ALPHA * x_ref[...] + gate_ref[0] * f_ref[...], g_ref[...], b_ref[...])


def _residual_ln(rows, xa, f, mod, part, ln_g, ln_b, n_rows):
    D = rows.D
    tm = rows.tile((512, 256, 128))
    return pl.pallas_call(
        _residual_ln_kernel,
        grid=(n_rows // tm,),
        in_specs=[pl.BlockSpec((tm, D), lambda i: (i, 0)), pl.BlockSpec((tm, D), lambda i: (i, 0)),
                  rows.mod_spec(part, tm), _resident((1, D)), _resident((1, D))],
        out_specs=pl.BlockSpec((tm, D), lambda i: (i, 0)),
        out_shape=jax.ShapeDtypeStruct((n_rows, D), F32),
        compiler_params=_cparams("parallel"),
        name="residual_ln",
    )(xa, f, mod, ln_g.reshape(1, D), ln_b.reshape(1, D))


def _moe_ln(rows, xa, mod, router, w_gu, w_down, ln_g, ln_b, n_rows):
    h, gates_t = _router(rows, xa, mod, router, n_rows)
    return _residual_ln(rows, xa, _moe(h, gates_t, w_gu, w_down), mod, 5, ln_g, ln_b, n_rows)


def _gmlp_kernel(x_ref, sh_ref, sc_ref, gate_ref, win_ref, bin_ref, vg_ref, vb_ref, ws_ref, bs_ref, wout_ref,
                 bout_ref, g_ref, b_ref, out_ref, gated_sc):
    tm = x_ref.shape[0]
    inner = wout_ref.shape[0]
    gdim = inner // CMLP_GROUPS
    x = x_ref[...]
    h = (x * (1.0 + sc_ref[0]) + sh_ref[0]).astype(BF16)
    v = jax.nn.gelu(jnp.dot(h, win_ref[:, inner:], preferred_element_type=F32) + bin_ref[:, inner:])
    v = _layer_norm(v, vg_ref[...], vb_ref[...]).astype(BF16)
    for gi in range(CMLP_GROUPS):
        cs = slice(gi * gdim, (gi + 1) * gdim)
        u = jax.nn.gelu(jnp.dot(h, win_ref[:, cs], preferred_element_type=F32) + bin_ref[:, cs])
        w_s = ws_ref[gi]
        bias = bs_ref[:, gi:gi + 1]
        for c in range(tm // CMLP_CHUNK):
            rs = slice(c * CMLP_CHUNK, (c + 1) * CMLP_CHUNK)
            mixed = jnp.dot(w_s, v[rs, cs], preferred_element_type=F32) + bias
            gated_sc[rs, cs] = (u[rs] * mixed).astype(BF16)
    y = jnp.dot(gated_sc[...], wout_ref[...], preferred_element_type=F32) + bout_ref[...]
    out_ref[...] = _layer_norm(DEEPNORM_ALPHA * x + gate_ref[0] * y, g_ref[...], b_ref[...])


def _gmlp_ln(rows, xa, mod, w_in, b_in, vg, vb, w_s, b_s, w_out, b_out, ln_g, ln_b, n_rows):
    D = rows.D
    inner = w_out.shape[0]
    tm = rows.tile((512, 256, 128))
    return pl.pallas_call(
        _gmlp_kernel,
        grid=(n_rows // tm,),
        in_specs=[pl.BlockSpec((tm, D), lambda i: (i, 0)), rows.mod_spec(0, tm), rows.mod_spec(1, tm),
                  rows.mod_spec(2, tm), _resident(w_in.shape), _resident((1, 2 * inner)), _resident((1, inner)),
                  _resident((1, inner)), _resident(w_s.shape), _resident((CMLP_CHUNK, CMLP_GROUPS)),
                  _resident(w_out.shape), _resident((1, D)), _resident((1, D)), _resident((1, D))],
        out_specs=pl.BlockSpec((tm, D), lambda i: (i, 0)),
        out_shape=jax.ShapeDtypeStruct((n_rows, D), F32),
        scratch_shapes=[pltpu.VMEM((tm, inner), BF16)],
        compiler_params=_cparams("parallel"),
        name="gmlp_ln",
    )(xa, mod, mod, mod, w_in, b_in.reshape(1, -1), vg.reshape(1, -1), vb.reshape(1, -1), w_s, b_s.T, w_out,
      b_out.reshape(1, D), ln_g.reshape(1, D), ln_b.reshape(1, D))


def _ret_proj_kernel(x_ref, sh_ref, sc_ref, w_ref, cos_ref, sin_ref, q_ref, k_ref, v_ref, sg_ref, *, nk, nv, dk):
    h = (x_ref[...] * (1.0 + sc_ref[0]) + sh_ref[0]).astype(BF16)
    cos, sin = cos_ref[...], sin_ref[...]
    k_scale = dk ** -0.5

    def rope_store(dst_ref, col0, scale):
        full = jnp.dot(h, w_ref[:, col0:col0 + nk], preferred_element_type=F32) * scale
        for j in range(nk // LANES):
            t = full[:, j * LANES:(j + 1) * LANES]
            ts = slice((j % (dk // LANES)) * LANES, (j % (dk // LANES) + 1) * LANES)
            dst_ref[:, j * LANES:(j + 1) * LANES] = (t * cos[:, ts] + _swap_pairs(t, dk // 4) * sin[:, ts]).astype(BF16)

    rope_store(q_ref, 0, 1.0)
    rope_store(k_ref, nk, k_scale)
    blk = 512
    for j in range(nv // blk):
        v_ref[:, j * blk:(j + 1) * blk] = jnp.dot(
            h, w_ref[:, 2 * nk + j * blk:2 * nk + (j + 1) * blk], preferred_element_type=F32).astype(BF16)
        gt = jnp.dot(h, w_ref[:, 2 * nk + nv + j * blk:2 * nk + nv + (j + 1) * blk], preferred_element_type=F32)
        sg_ref[:, j * blk:(j + 1) * blk] = (gt * jax.nn.sigmoid(gt)).astype(BF16)


def _ret_project(rows, xa, mod, w, cos, sin):
    D = rows.D
    nk = D
    nv = (w.shape[1] - 2 * nk) // 2
    dk = nk // RET_HEADS
    tm = rows.tile((512, 256))
    row_spec = lambda width: pl.BlockSpec((tm, width), lambda i: (i, 0))
    return pl.pallas_call(
        functools.partial(_ret_proj_kernel, nk=nk, nv=nv, dk=dk),
        grid=(rows.n // tm,),
        in_specs=[row_spec(D), rows.mod_spec(0, tm), rows.mod_spec(1, tm), _resident(w.shape),
                  rows.pos_spec(tm, dk), rows.pos_spec(tm, dk)],
        out_specs=[row_spec(nk), row_spec(nk), row_spec(nv), row_spec(nv)],
        out_shape=[jax.ShapeDtypeStruct((rows.n, nk), BF16), jax.ShapeDtypeStruct((rows.n, nk), BF16),
                   jax.ShapeDtypeStruct((rows.n, nv), BF16), jax.ShapeDtypeStruct((rows.n, nv), BF16)],
        compiler_params=_cparams("parallel"),
        name="ret_proj",
    )(xa, mod, mod, w, cos, sin)


RET_SCAN_CHUNK = 256


def _ret_scan_kernel(lg_ref, q_ref, k_ref, v_ref, sg_ref, sf_ref, sb_ref, o_ref, sf_out, sb_out,
                     state_sc, part_sc, inner_sc, cross_sc, weight_sc):
    ch = RET_SCAN_CHUNK
    n_chunks = q_ref.shape[0] // ch
    dk, dv = q_ref.shape[1], v_ref.shape[1]
    head = pl.program_id(1)

    chunk_decay = []
    for d in range(2):
        lg = lg_ref[d, head]
        ii = lax.broadcasted_iota(jnp.int32, (ch, ch), 0)
        jj = lax.broadcasted_iota(jnp.int32, (ch, ch), 1)
        dist = (ii - jj) if d == 0 else (jj - ii)
        inner_sc[d] = jnp.where(dist >= 0, jnp.exp(jnp.maximum(dist, 0).astype(F32) * lg), 0.0)
        row_v = lax.broadcasted_iota(jnp.int32, (ch, dv), 0)
        row_k = lax.broadcasted_iota(jnp.int32, (ch, dk), 0)
        cross_sc[d] = jnp.exp(((row_v if d == 0 else ch - 1 - row_v) + 1).astype(F32) * lg)
        weight_sc[d] = jnp.exp((ch - 1 - (row_k if d == 0 else ch - 1 - row_k)).astype(F32) * lg)
        chunk_decay.append(jnp.exp(jnp.full((1, 1), ch, F32) * lg))
    state_sc[0] = sf_ref[0, 0]
    state_sc[1] = sb_ref[0, 0]

    def visit(d, c):
        rows = pl.ds(pl.multiple_of(c * ch, ch), ch)
        q, k, v = q_ref[rows, :], k_ref[rows, :], v_ref[rows, :]
        scores = lax.dot_general(q, k, (((1,), (1,)), ((), ())), preferred_element_type=F32) * inner_sc[d]
        state = state_sc[d]
        out = (jnp.dot(scores.astype(BF16), v, preferred_element_type=F32)
               + jnp.dot(q, state.astype(BF16), preferred_element_type=F32) * cross_sc[d])
        kw = (k.astype(F32) * weight_sc[d]).astype(BF16)
        state_sc[d] = state * chunk_decay[d] + lax.dot_general(
            kw, v, (((0,), (0,)), ((), ())), preferred_element_type=F32)
        return rows, out

    def finish(rows, o):
        o = o * lax.rsqrt(jnp.mean(o * o, axis=-1, keepdims=True) + RMS_EPS)
        o_ref[rows, :] = (sg_ref[rows, :].astype(F32) * o).astype(BF16)

    def first_visits(t, carry):
        for d, c in ((0, t), (1, n_chunks - 1 - t)):
            rows, out = visit(d, c)
            part_sc[rows, :] = out
        return carry

    def second_visits(t, carry):
        for d, c in ((0, t), (1, n_chunks - 1 - t)):
            rows, out = visit(d, c)
            finish(rows, out + part_sc[rows, :])
        return carry

    lax.fori_loop(0, n_chunks // 2, first_visits, 0)
    if n_chunks % 2 == 1:
        rows, out_f = visit(0, n_chunks // 2)
        _, out_b = visit(1, n_chunks // 2)
        finish(rows, out_f + out_b)
    lax.fori_loop((n_chunks + 1) // 2, n_chunks, second_visits, 0)
    sf_out[0, 0] = state_sc[0]
    sb_out[0, 0] = state_sc[1]


def _ret_scan_segment(rows, log_gamma, q, k, v, sg, state_f, state_b, *, seg_len, first_block):
    B = rows.B
    dk = q.shape[1] // RET_HEADS
    dv = v.shape[1] // RET_HEADS
    ch = RET_SCAN_CHUNK
    assert seg_len % ch == 0
    kspec = pl.BlockSpec((seg_len, dk), lambda b, h: (first_block + b, h))
    vspec = pl.BlockSpec((seg_len, dv), lambda b, h: (first_block + b, h))
    sspec = pl.BlockSpec((1, 1, dk, dv), lambda b, h: (b, h, 0, 0))
    state_shape = jax.ShapeDtypeStruct((B, RET_HEADS, dk, dv), F32)
    return pl.pallas_call(
        _ret_scan_kernel,
        grid=(B, RET_HEADS),
        in_specs=[pl.BlockSpec(memory_space=pltpu.SMEM), kspec, kspec, vspec, vspec, sspec, sspec],
        out_specs=[pl.BlockSpec((seg_len, dv), lambda b, h: (b, h)), sspec, sspec],
        out_shape=[jax.ShapeDtypeStruct((B * seg_len, v.shape[1]), BF16), state_shape, state_shape],
        scratch_shapes=[pltpu.VMEM((2, dk, dv), F32), pltpu.VMEM((seg_len, dv), F32),
                        pltpu.VMEM((2, ch, ch), F32), pltpu.VMEM((2, ch, dv), F32), pltpu.VMEM((2, ch, dk), F32)],
        compiler_params=_cparams("parallel", "parallel"),
        name="ret_scan",
    )(log_gamma, q, k, v, sg, state_f, state_b)


def _retention(rows, log_gamma, q, k, v, sg):
    B, S, C = rows.B, rows.S, rows.C
    assert rows.n_lat % C == 0
    dk, dv = q.shape[1] // RET_HEADS, v.shape[1] // RET_HEADS
    zeros = jnp.zeros((B, RET_HEADS, dk, dv), F32)
    o_ctx, state_f, state_b = _ret_scan_segment(rows, log_gamma, q, k, v, sg, zeros, zeros,
                                                seg_len=C, first_block=rows.n_lat // C)
    o_lat, _, _ = _ret_scan_segment(rows, log_gamma, q, k, v, sg, state_f, state_b, seg_len=S, first_block=0)
    return [o_lat, o_ctx]


def kernel(x, c, ctx, c_ctx, l0_ada_w, l0_ada_b, l0_ln1_g, l0_ln1_b, l0_ln2_g, l0_ln2_b, l0_attn_wqkv, l0_attn_q_norm, l0_attn_k_norm, l0_attn_wo, l0_ffn_w_gu, l0_ffn_w_down, l1_ada_w, l1_ada_b, l1_ln1_g, l1_ln1_b, l1_ln2_g, l1_ln2_b, l1_cmlp_w_in, l1_cmlp_b_in, l1_cmlp_v_norm_g, l1_cmlp_v_norm_b, l1_cmlp_w_s, l1_cmlp_b_s, l1_cmlp_w_out, l1_cmlp_b_out, l1_moe_router, l1_moe_w_gu, l1_moe_w_down, l2_ada_w, l2_ada_b, l2_ln1_g, l2_ln1_b, l2_ln2_g, l2_ln2_b, l2_ret_wqkvg, l2_ret_decay, l2_ret_wo, l2_ffn_w_gu, l2_ffn_w_down, l3_ada_w, l3_ada_b, l3_ln1_g, l3_ln1_b, l3_ln2_g, l3_ln2_b, l3_attn_wqkv, l3_attn_q_norm, l3_attn_k_norm, l3_attn_wo, l3_moe_router, l3_moe_w_gu, l3_moe_w_down):
    B, S, D = x.shape
    C = ctx.shape[1]
    rows = _Rows(B, S, C, D)
    assert S % GRID_W == 0 and C % RET_CHUNK == 0 and S % RET_CHUNK == 0

    xa = jnp.concatenate([x.reshape(rows.n_lat, D), ctx.reshape(rows.n_ctx, D)], axis=0)
    n_cond = -(-(B + 1) // 8) * 8
    cond = jnp.concatenate([c, c_ctx[None, :], jnp.zeros((n_cond - B - 1, D), F32)], axis=0)

    def modulation(ada_w, ada_b):
        return _ada(cond, ada_w, ada_b).reshape(n_cond * 6, 1, D)

    pad = rows.tile((512, 256))
    attn_cos, attn_sin = _rope_tables(S, ATTN_HEAD_DIM, pad)
    ret_cos, ret_sin = _rope_tables(S, D // RET_HEADS, pad)
    zero_bias = jnp.zeros((D,), F32)
    bf = lambda w: w.astype(BF16)

    def attention_layer(xa, mod, wqkv, qg, kg, wo, ln_g, ln_b, with_ctx):
        q, k, v = _attn_project(rows, xa, mod, bf(wqkv), qg, kg, attn_cos, attn_sin)
        o = _attention(rows, q, k, v, with_ctx)
        return _proj_ln(rows, [o], bf(wo), zero_bias, xa, mod, ln_g, ln_b)

    def retention_layer(xa, mod, wqkvg, decay, wo, ln_g, ln_b):
        log_gamma = -jnp.exp(decay.astype(F32))
        q, k, v, sg = _ret_project(rows, xa, mod, bf(wqkvg), ret_cos, ret_sin)
        o_segments = _retention(rows, log_gamma, q, k, v, sg)
        return _proj_ln(rows, o_segments, bf(wo), zero_bias, xa, mod, ln_g, ln_b)

    mod = modulation(l0_ada_w, l0_ada_b)
    xa = attention_layer(xa, mod, l0_attn_wqkv, l0_attn_q_norm, l0_attn_k_norm, l0_attn_wo, l0_ln1_g, l0_ln1_b, True)
    xa = _ffn_ln(rows, xa, mod, bf(l0_ffn_w_gu), bf(l0_ffn_w_down), l0_ln2_g, l0_ln2_b, rows.n)

    mod = modulation(l1_ada_w, l1_ada_b)
    xa = _gmlp_ln(rows, xa, mod, bf(l1_cmlp_w_in), l1_cmlp_b_in, l1_cmlp_v_norm_g, l1_cmlp_v_norm_b,
                  bf(l1_cmlp_w_s), l1_cmlp_b_s, bf(l1_cmlp_w_out), l1_cmlp_b_out, l1_ln1_g, l1_ln1_b, rows.n)
    xa = _moe_ln(rows, xa, mod, l1_moe_router, bf(l1_moe_w_gu), bf(l1_moe_w_down), l1_ln2_g, l1_ln2_b, rows.n)

    mod = modulation(l2_ada_w, l2_ada_b)
    xa = retention_layer(xa, mod, l2_ret_wqkvg, l2_ret_decay, l2_ret_wo, l2_ln1_g, l2_ln1_b)
    xa = _ffn_ln(rows, xa, mod, bf(l2_ffn_w_gu), bf(l2_ffn_w_down), l2_ln2_g, l2_ln2_b, rows.n)

    mod = modulation(l3_ada_w, l3_ada_b)
    xl = attention_layer(xa, mod, l3_attn_wqkv, l3_attn_q_norm, l3_attn_k_norm, l3_attn_wo, l3_ln1_g, l3_ln1_b, False)
    xl = _moe_ln(rows, xl, mod, l3_moe_router, bf(l3_moe_w_gu), bf(l3_moe_w_down), l3_ln2_g, l3_ln2_b, rows.n_lat)
    return xl.reshape(B, S, D)
```

```python
import functools

import jax
import jax.numpy as jnp
from jax import lax
from jax.experimental import pallas as pl
from jax.experimental.pallas import tpu as pltpu

F32 = jnp.float32
BF16 = jnp.bfloat16

DEPTH = 4
GRID_W = 64
ROPE_THETA = 10000.0
DEEPNORM_ALPHA = (2 * DEPTH) ** 0.25
LN_EPS = 1e-5
RMS_EPS = 1e-6
ATTN_HEAD_DIM = 128
ATTN_KV_HEADS = 2
CMLP_CHUNK = 128
CMLP_GROUPS = 8
RET_HEADS = 4
RET_CHUNK = 128
MOE_EXPERTS = 8

LANES = 128
VMEM_LIMIT_BYTES = 56 * 1024 * 1024
NEG_BIG = -1e30


def _cparams(*sem):
    return pltpu.CompilerParams(dimension_semantics=sem, vmem_limit_bytes=VMEM_LIMIT_BYTES)


def _resident(shape):
    nd = len(shape)
    return pl.BlockSpec(shape, lambda *_: (0,) * nd)


def _layer_norm(z, g, b):
    mu = jnp.mean(z, axis=-1, keepdims=True)
    zc = z - mu
    var = jnp.mean(zc * zc, axis=-1, keepdims=True)
    return zc * lax.rsqrt(var + LN_EPS) * g + b


def _pick_tile(candidates, *extents):
    for t in candidates:
        if all(e % t == 0 for e in extents):
            return t
    raise ValueError(f"no tile in {candidates} divides {extents}")


class _Rows:
    def __init__(self, batch, seq, ctx_len, dim):
        self.B, self.S, self.C, self.D = batch, seq, ctx_len, dim
        self.n_lat = batch * seq
        self.n_ctx = batch * ctx_len
        self.n = self.n_lat + self.n_ctx

    def tile(self, candidates):
        return _pick_tile(candidates, self.S, self.n_ctx)

    def mod_spec(self, part, tm):
        S, B = self.S, self.B
        return pl.BlockSpec((1, 1, self.D), lambda i, *_: (jnp.minimum(i * tm // S, B) * 6 + part, 0, 0))

    def pos_spec(self, tm, width):
        n_lat_tiles, per_seq = self.n_lat // tm, self.S // tm
        return pl.BlockSpec((tm, width), lambda i, *_: (jnp.where(i < n_lat_tiles, i % per_seq, per_seq), 0))


def _dot_3pass(a, b):
    def split(t):
        hi = t.astype(BF16)
        return hi, (t - hi.astype(F32)).astype(BF16)

    (a_hi, a_lo), (b_hi, b_lo) = split(a), split(b)
    rows = a.shape[0]
    by_b_hi = jnp.dot(jnp.concatenate([a_hi, a_lo], axis=0), b_hi, preferred_element_type=F32)
    return by_b_hi[:rows] + by_b_hi[rows:] + jnp.dot(a_hi, b_lo, preferred_element_type=F32)


def _ada_kernel(c_ref, w_ref, b_ref, o_ref):
    c = c_ref[...]
    s = c * jax.nn.sigmoid(c)
    o_ref[...] = _dot_3pass(s, w_ref[...]) + b_ref[...]


def _ada(cond, w, b):
    r, d = cond.shape
    n = w.shape[1]
    tn = _pick_tile((1536, 1024, 512, 128), n)
    return pl.pallas_call(
        _ada_kernel,
        grid=(n // tn,),
        in_specs=[_resident((r, d)), pl.BlockSpec((d, tn), lambda j: (0, j)), pl.BlockSpec((1, tn), lambda j: (0, j))],
        out_specs=pl.BlockSpec((r, tn), lambda j: (0, j)),
        out_shape=jax.ShapeDtypeStruct((r, n), F32),
        compiler_params=_cparams("parallel"),
        name="ada",
    )(cond, w, b.reshape(1, n))


def _rope_tables(seq, head_dim, pad_rows):
    nf = head_dim // 4
    inv_freq = ROPE_THETA ** (-jnp.arange(nf, dtype=F32) / nf)
    t = jnp.arange(seq, dtype=jnp.int32)
    row = (t // GRID_W).astype(F32)[:, None] * inv_freq
    col = (t % GRID_W).astype(F32)[:, None] * inv_freq
    cos = jnp.concatenate([jnp.cos(row), jnp.cos(row), jnp.cos(col), jnp.cos(col)], axis=-1)
    sin = jnp.concatenate([-jnp.sin(row), jnp.sin(row), -jnp.sin(col), jnp.sin(col)], axis=-1)
    cos = jnp.concatenate([cos, jnp.ones((pad_rows, head_dim), F32)], axis=0)
    sin = jnp.concatenate([sin, jnp.zeros((pad_rows, head_dim), F32)], axis=0)
    return cos, sin


def _swap_pairs(y, half):
    if 2 * half == LANES:
        return pltpu.roll(y, half, 1)
    lane = lax.broadcasted_iota(jnp.int32, y.shape, 1)
    return jnp.where(lane % (2 * half) < half, pltpu.roll(y, LANES - half, 1), pltpu.roll(y, half, 1))


KEY_BLOCK = 256
LOG2_E = 1.4426950408889634


def _qkv_kernel(x_ref, sh_ref, sc_ref, w_ref, qg_ref, kg_ref, cos_ref, sin_ref, q_ref, kt_ref, v_ref, *, nq, nkv):
    dh = ATTN_HEAD_DIM
    h = (x_ref[...] * (1.0 + sc_ref[0]) + sh_ref[0]).astype(BF16)
    cos, sin = cos_ref[...], sin_ref[...]
    scale = dh ** -0.5 * LOG2_E

    def normed_rope(t, g):
        y = t * lax.rsqrt(jnp.mean(t * t, axis=-1, keepdims=True) + RMS_EPS) * g
        return y * cos + _swap_pairs(y, dh // 4) * sin

    t = jnp.dot(h, w_ref[...], preferred_element_type=F32)
    for hd in range(nq // dh):
        q_ref[:, hd * dh:(hd + 1) * dh] = (normed_rope(t[:, hd * dh:(hd + 1) * dh], qg_ref[...]) * scale).astype(BF16)
    k = jnp.concatenate([normed_rope(t[:, nq + hd * dh:nq + (hd + 1) * dh], kg_ref[...]) for hd in range(nkv // dh)],
                        axis=1)
    for j in range(kt_ref.shape[0]):
        kt_ref[j] = k[j * KEY_BLOCK:(j + 1) * KEY_BLOCK, :].T.astype(BF16)
    v_ref[...] = t[:, nq + nkv:].astype(BF16)


def _attn_project(rows, xa, mod, w, qg, kg, cos, sin):
    D = rows.D
    nkv = ATTN_KV_HEADS * ATTN_HEAD_DIM
    nq = w.shape[1] - 2 * nkv
    tm = rows.tile((512, 256))
    row_spec = lambda width: pl.BlockSpec((tm, width), lambda i: (i, 0))
    return pl.pallas_call(
        functools.partial(_qkv_kernel, nq=nq, nkv=nkv),
        grid=(rows.n // tm,),
        in_specs=[row_spec(D), rows.mod_spec(0, tm), rows.mod_spec(1, tm), _resident(w.shape),
                  _resident((1, ATTN_HEAD_DIM)), _resident((1, ATTN_HEAD_DIM)),
                  rows.pos_spec(tm, ATTN_HEAD_DIM), rows.pos_spec(tm, ATTN_HEAD_DIM)],
        out_specs=[row_spec(nq), pl.BlockSpec((tm // KEY_BLOCK, nkv, KEY_BLOCK), lambda i: (i, 0, 0)), row_spec(nkv)],
        out_shape=[jax.ShapeDtypeStruct((rows.n, nq), BF16),
                   jax.ShapeDtypeStruct((rows.n // KEY_BLOCK, nkv, KEY_BLOCK), BF16),
                   jax.ShapeDtypeStruct((rows.n, nkv), BF16)],
        compiler_params=_cparams("parallel"),
        name="attn_qkv",
    )(xa, mod, mod, w, qg.reshape(1, -1), kg.reshape(1, -1), cos, sin)


def _attn_kernel(q_ref, kc_ref, vc_ref, kl_ref, vl_ref, o_ref, m_sc, acc_sc, *, tq, blocks_per_chunk, n_lat_tiles):
    dh = ATTN_HEAD_DIM
    group = q_ref.shape[1] // dh // ATTN_KV_HEADS
    is_latent = pl.program_id(1) < n_lat_tiles
    n_lat_chunks = jnp.where(is_latent, kl_ref.shape[0] // blocks_per_chunk, 0)
    chunk = blocks_per_chunk * KEY_BLOCK

    heads = range(ATTN_KV_HEADS)
    head_rows = group * tq
    gs = [slice(g * dh, (g + 1) * dh) for g in heads]
    qs = [jnp.concatenate([q_ref[:, (g * group + j) * dh:(g * group + j + 1) * dh] for j in range(group)], axis=0)
          for g in heads]
    m_sc[...] = jnp.full(m_sc.shape, NEG_BIG, F32)
    acc_sc[...] = jnp.zeros(acc_sc.shape, F32)

    def online_softmax_step(kts, vs):
        width = kts[0].shape[1]
        s = jnp.concatenate([jnp.dot(qs[g], kts[g], preferred_element_type=F32) for g in heads], axis=0)
        m_old = m_sc[...]
        m_new = jnp.maximum(m_old, jnp.max(s, axis=-1, keepdims=True))
        p = jnp.exp2((s - jnp.concatenate([m_new] * (width // LANES), axis=1)).astype(BF16))
        alpha = jnp.exp2(m_old - m_new)
        ones = jnp.ones((width, dh), BF16)
        pv = jnp.concatenate([jnp.dot(p[g * head_rows:(g + 1) * head_rows], jnp.concatenate([vs[g], ones], axis=1),
                                      preferred_element_type=F32) for g in heads], axis=0)
        acc_sc[...] = jnp.concatenate([alpha, alpha], axis=1) * acc_sc[...] + pv
        m_sc[...] = m_new

    for c in range(kc_ref.shape[0]):
        online_softmax_step([kc_ref[c, gs[g], :] for g in heads],
                            [vc_ref[c * KEY_BLOCK:(c + 1) * KEY_BLOCK, gs[g]] for g in heads])

    def latent_chunk(c, carry):
        r0 = pl.multiple_of(c * chunk, chunk)
        online_softmax_step(
            [jnp.concatenate([kl_ref[c * blocks_per_chunk + j, gs[g], :] for j in range(blocks_per_chunk)], axis=1)
             for g in heads],
            [vl_ref[pl.ds(r0, chunk), gs[g]] for g in heads])
        return carry

    lax.fori_loop(0, n_lat_chunks, latent_chunk, 0)
    acc = acc_sc[...]
    o = acc[:, :dh] / acc[:, dh:]
    for hd in range(ATTN_KV_HEADS * group):
        o_ref[:, hd * dh:(hd + 1) * dh] = o[hd * tq:(hd + 1) * tq].astype(BF16)


def _attention(rows, q, kt, v, with_ctx):
    B, S, C = rows.B, rows.S, rows.C
    nq = q.shape[1]
    nkv = v.shape[1]
    assert S % KEY_BLOCK == 0 and C % KEY_BLOCK == 0 and rows.n_lat % C == 0
    tq = _pick_tile((256, 128), S, C)
    blocks_per_chunk = next(n for n in (4, 2, 1) if S % (n * KEY_BLOCK) == 0)
    lat_tiles, ctx_tiles = S // tq, C // tq
    tiles = lat_tiles + (ctx_tiles if with_ctx else 0)
    group = nq // nkv

    def q_map(b, i):
        return (jnp.where(i < lat_tiles, b * lat_tiles + i, rows.n_lat // tq + b * ctx_tiles + (i - lat_tiles)), 0)

    kt_ctx = pl.BlockSpec((C // KEY_BLOCK, nkv, KEY_BLOCK), lambda b, i: (rows.n_lat // C + b, 0, 0))
    kt_lat = pl.BlockSpec((S // KEY_BLOCK, nkv, KEY_BLOCK), lambda b, i: (b, 0, 0))
    v_ctx = pl.BlockSpec((C, nkv), lambda b, i: (rows.n_lat // C + b, 0))
    v_lat = pl.BlockSpec((S, nkv), lambda b, i: (b, 0))
    return pl.pallas_call(
        functools.partial(_attn_kernel, tq=tq, blocks_per_chunk=blocks_per_chunk, n_lat_tiles=lat_tiles),
        grid=(B, tiles),
        in_specs=[pl.BlockSpec((tq, nq), q_map), kt_ctx, v_ctx, kt_lat, v_lat],
        out_specs=pl.BlockSpec((tq, nq), q_map),
        out_shape=jax.ShapeDtypeStruct((rows.n if with_ctx else rows.n_lat, nq), BF16),
        scratch_shapes=[pltpu.VMEM((ATTN_KV_HEADS * group * tq, LANES), F32),
                        pltpu.VMEM((ATTN_KV_HEADS * group * tq, 2 * ATTN_HEAD_DIM), F32)],
        compiler_params=_cparams("parallel", "parallel"),
        name="attn_core",
    )(q, kt, v, kt, v)


def _proj_ln_kernel(*refs, first_tiles):
    n_seg = len(first_tiles) - 1
    o_refs = refs[:n_seg]
    w_ref, bias_ref, x_ref, gate_ref, g_ref, b_ref, out_ref = refs[n_seg:]
    i = pl.program_id(0)
    for seg, o_ref in enumerate(o_refs):
        @pl.when((i >= first_tiles[seg]) & (i < first_tiles[seg + 1]))
        def _():
            y = jnp.dot(o_ref[...], w_ref[...], preferred_element_type=F32) + bias_ref[...]
            out_ref[...] = _layer_norm(DEEPNORM_ALPHA * x_ref[...] + gate_ref[0] * y, g_ref[...], b_ref[...])


def _proj_ln(rows, o_segments, w, bias, xa, mod, ln_g, ln_b):
    D = rows.D
    kdim = w.shape[0]
    tm = rows.tile((512, 256, 128))
    first_tiles = [0]
    for o in o_segments:
        assert o.shape[0] % tm == 0
        first_tiles.append(first_tiles[-1] + o.shape[0] // tm)
    n_rows = first_tiles[-1] * tm

    def segment_spec(seg):
        first, count = first_tiles[seg], first_tiles[seg + 1] - first_tiles[seg]
        return pl.BlockSpec((tm, kdim), lambda i: (jnp.clip(i - first, 0, count - 1), 0))

    return pl.pallas_call(
        functools.partial(_proj_ln_kernel, first_tiles=tuple(first_tiles)),
        grid=(n_rows // tm,),
        in_specs=[segment_spec(seg) for seg in range(len(o_segments))] + [
            _resident(w.shape), _resident((1, D)), pl.BlockSpec((tm, D), lambda i: (i, 0)), rows.mod_spec(2, tm),
            _resident((1, D)), _resident((1, D))],
        out_specs=pl.BlockSpec((tm, D), lambda i: (i, 0)),
        out_shape=jax.ShapeDtypeStruct((n_rows, D), F32),
        compiler_params=_cparams("parallel"),
        name="proj_ln",
    )(*o_segments, w, bias.reshape(1, D), xa, mod, ln_g.reshape(1, D), ln_b.reshape(1, D))


def _ffn_kernel(x_ref, sh_ref, sc_ref, gate_ref, wg_ref, wu_ref, wd_ref, g_ref, b_ref, out_ref, h_sc, acc_sc):
    f = pl.program_id(1)

    @pl.when(f == 0)
    def _():
        h_sc[...] = (x_ref[...] * (1.0 + sc_ref[0]) + sh_ref[0]).astype(BF16)
        acc_sc[...] = jnp.zeros(acc_sc.shape, F32)

    h = h_sc[...]
    gt = jnp.dot(h, wg_ref[...], preferred_element_type=F32)
    up = jnp.dot(h, wu_ref[...], preferred_element_type=F32)
    act = (gt * jax.nn.sigmoid(gt) * up).astype(BF16)
    acc_sc[...] += jnp.dot(act, wd_ref[...], preferred_element_type=F32)

    @pl.when(f == pl.num_programs(1) - 1)
    def _():
        out_ref[...] = _layer_norm(DEEPNORM_ALPHA * x_ref[...] + gate_ref[0] * acc_sc[...], g_ref[...], b_ref[...])


def _ffn_ln(rows, xa, mod, w_gu, w_down, ln_g, ln_b, n_rows):
    D = rows.D
    F = w_down.shape[0]
    tm = rows.tile((512, 256, 128))
    tf = _pick_tile((1792, 512, 256, 128), F)
    nf = F // tf
    return pl.pallas_call(
        _ffn_kernel,
        grid=(n_rows // tm, nf),
        in_specs=[pl.BlockSpec((tm, D), lambda i, f: (i, 0)),
                  rows.mod_spec(3, tm), rows.mod_spec(4, tm), rows.mod_spec(5, tm),
                  pl.BlockSpec((D, tf), lambda i, f: (0, f)), pl.BlockSpec((D, tf), lambda i, f: (0, nf + f)),
                  pl.BlockSpec((tf, D), lambda i, f: (f, 0)), _resident((1, D)), _resident((1, D))],
        out_specs=pl.BlockSpec((tm, D), lambda i, f: (i, 0)),
        out_shape=jax.ShapeDtypeStruct((n_rows, D), F32),
        scratch_shapes=[pltpu.VMEM((tm, D), BF16), pltpu.VMEM((tm, D), F32)],
        compiler_params=_cparams("parallel", "arbitrary"),
        name="ffn_ln",
    )(xa, mod, mod, mod, w_gu, w_gu, w_down, ln_g.reshape(1, D), ln_b.reshape(1, D))


MOE_ROW_BLOCKS = (512, 256, 128)


def _router_kernel(x_ref, sh_ref, sc_ref, r_ref, h_ref, gates_ref, *, n_experts):
    h = x_ref[...] * (1.0 + sc_ref[0]) + sh_ref[0]
    h_ref[...] = h.astype(BF16)
    logits = _dot_3pass(h, r_ref[...])
    lane = lax.broadcasted_iota(jnp.int32, logits.shape, 1)
    lowest = jnp.finfo(F32).min
    lg = jnp.where(lane < n_experts, logits, lowest)
    m1 = jnp.max(lg, axis=-1, keepdims=True)
    i1 = jnp.min(jnp.where(lg == m1, lane, LANES), axis=-1, keepdims=True)
    lg2 = jnp.where(lane == i1, lowest, lg)
    m2 = jnp.max(lg2, axis=-1, keepdims=True)
    i2 = jnp.min(jnp.where(lg2 == m2, lane, LANES), axis=-1, keepdims=True)
    e2 = jnp.exp(m2 - m1)
    denom = 1.0 + e2
    gates = jnp.where(lane == i1, 1.0 / denom, 0.0) + jnp.where(lane == i2, e2 / denom, 0.0)
    gates_ref[...] = gates.T[:n_experts, :]


def _router(rows, xa, mod, router, n_rows):
    D = rows.D
    n_experts = router.shape[1]
    tm = rows.tile((512, 256, 128))
    router_pad = jnp.pad(router, ((0, 0), (0, LANES - n_experts)))
    return pl.pallas_call(
        functools.partial(_router_kernel, n_experts=n_experts),
        grid=(n_rows // tm,),
        in_specs=[pl.BlockSpec((tm, D), lambda i: (i, 0)), rows.mod_spec(3, tm), rows.mod_spec(4, tm),
                  _resident((D, LANES))],
        out_specs=[pl.BlockSpec((tm, D), lambda i: (i, 0)), pl.BlockSpec((n_experts, tm), lambda i: (0, i))],
        out_shape=[jax.ShapeDtypeStruct((n_rows, D), BF16), jax.ShapeDtypeStruct((n_experts, n_rows), F32)],
        compiler_params=_cparams("parallel"),
        name="moe_router",
    )(xa, mod, mod, router_pad)


def _plan_kernel(g_ref, rank_ref, cnt_ref):
    n_experts, tile = g_ref.shape
    routed = g_ref[...] > 0.0
    ones = jnp.where(routed, 1.0, 0.0)
    before = jnp.where(lax.broadcasted_iota(jnp.int32, (LANES, LANES), 0)
                       < lax.broadcasted_iota(jnp.int32, (LANES, LANES), 1), 1.0, 0.0)
    seen = jnp.zeros((n_experts, 1), F32)
    for c in range(tile // LANES):
        cs = slice(c * LANES, (c + 1) * LANES)
        rank = jnp.dot(ones[:, cs], before, preferred_element_type=F32) + seen
        rank_ref[:, cs] = jnp.where(routed[:, cs], rank, -1.0).astype(jnp.int32)
        seen = seen + jnp.sum(ones[:, cs], axis=1, keepdims=True)
    cnt_ref[0] = jnp.broadcast_to(seen, (n_experts, LANES)).astype(jnp.int32)


def _moe_plan(gates_t, tile):
    n_experts, n_rows = gates_t.shape
    n_tiles = n_rows // tile
    rank, counts = pl.pallas_call(
        _plan_kernel,
        grid=(n_tiles,),
        in_specs=[pl.BlockSpec((n_experts, tile), lambda s: (0, s))],
        out_specs=[pl.BlockSpec((n_experts, tile), lambda s: (0, s)),
                   pl.BlockSpec((1, n_experts, LANES), lambda s: (s, 0, 0))],
        out_shape=[jax.ShapeDtypeStruct((n_experts, n_rows), jnp.int32),
                   jax.ShapeDtypeStruct((n_tiles, n_experts, LANES), jnp.int32)],
        compiler_params=_cparams("parallel"),
        name="moe_plan",
    )(gates_t)
    count = counts[:, :, 0].reshape(-1)
    smallest = MOE_ROW_BLOCKS[-1]
    return rank, (count + smallest - 1) // smallest * smallest


def _moe_kernel(nrows_ref, h_ref, rank_ref, gate_ref, wg_ref, wu_ref, wd_ref, out_ref, xg_sc, y_sc):
    s, e, f = pl.program_id(0), pl.program_id(1), pl.program_id(2)
    tile = h_ref.shape[0]
    n_rows = nrows_ref[s * pl.num_programs(1) + e]
    big = MOE_ROW_BLOCKS[0]

    @pl.when((e == 0) & (f == 0))
    def _():
        out_ref[...] = jnp.zeros(out_ref.shape, F32)

    def for_each_block(fn):
        def body(r, carry):
            fn(r * big, big)
            return carry

        lax.fori_loop(0, n_rows // big, body, 0)
        for size in MOE_ROW_BLOCKS[1:]:
            @pl.when(n_rows // size % 2 == 1)
            def _():
                fn(n_rows // (2 * size) * (2 * size), size)

    def rows_at(start, size):
        return pl.ds(pl.multiple_of(start, MOE_ROW_BLOCKS[-1]), size)

    def slot_matches(start, size):
        slot = lax.broadcasted_iota(jnp.int32, (size, tile), 0) + start
        return slot == rank_ref[pl.ds(e, 1), :]

    def gather(start, size):
        onehot = jnp.where(slot_matches(start, size), 1.0, 0.0).astype(BF16)
        xg_sc[rows_at(start, size), :] = jnp.dot(onehot, h_ref[...], preferred_element_type=F32).astype(BF16)
        y_sc[rows_at(start, size), :] = jnp.zeros((size, y_sc.shape[1]), F32)

    def expert(start, size):
        xb = xg_sc[rows_at(start, size), :]
        gt = jnp.dot(xb, wg_ref[0], preferred_element_type=F32)
        up = jnp.dot(xb, wu_ref[0], preferred_element_type=F32)
        act = (gt * jax.nn.sigmoid(gt) * up).astype(BF16)
        y_sc[rows_at(start, size), :] += jnp.dot(act, wd_ref[0], preferred_element_type=F32)

    def scatter(start, size):
        weighted = jnp.where(slot_matches(start, size), gate_ref[pl.ds(e, 1), :], 0.0).astype(BF16)
        out_ref[...] += lax.dot_general(weighted, y_sc[rows_at(start, size), :].astype(BF16),
                                        (((0,), (0,)), ((), ())), preferred_element_type=F32)

    @pl.when(f == 0)
    def _():
        for_each_block(gather)

    for_each_block(expert)

    @pl.when(f == pl.num_programs(2) - 1)
    def _():
        for_each_block(scatter)


def _moe(h, gates_t, w_gu, w_down):
    n_rows, D = h.shape
    E, F = w_down.shape[0], w_down.shape[1]
    tile = _pick_tile((2176, 2048, 1024, 512, 256, 128), n_rows)
    tf = _pick_tile((512, 256, 128), F)
    nf = F // tf
    rank, n_block_rows = _moe_plan(gates_t, tile)
    max_rows = -(-tile // MOE_ROW_BLOCKS[-1]) * MOE_ROW_BLOCKS[-1]
    grid_spec = pltpu.PrefetchScalarGridSpec(
        num_scalar_prefetch=1,
        grid=(n_rows // tile, E, nf),
        in_specs=[pl.BlockSpec((tile, D), lambda s, e, f, *_: (s, 0)),
                  pl.BlockSpec((E, tile), lambda s, e, f, *_: (0, s)),
                  pl.BlockSpec((E, tile), lambda s, e, f, *_: (0, s)),
                  pl.BlockSpec((1, D, tf), lambda s, e, f, *_: (e, 0, f)),
                  pl.BlockSpec((1, D, tf), lambda s, e, f, *_: (e, 0, nf + f)),
                  pl.BlockSpec((1, tf, D), lambda s, e, f, *_: (e, f, 0))],
        out_specs=pl.BlockSpec((tile, D), lambda s, e, f, *_: (s, 0)),
        scratch_shapes=[pltpu.VMEM((max_rows, D), BF16), pltpu.VMEM((max_rows, D), F32)],
    )
    return pl.pallas_call(
        _moe_kernel,
        grid_spec=grid_spec,
        out_shape=jax.ShapeDtypeStruct((n_rows, D), F32),
        compiler_params=_cparams("parallel", "arbitrary", "arbitrary"),
        name="moe_experts",
    )(n_block_rows, h, rank, gates_t, w_gu, w_gu, w_down)


def _residual_ln_kernel(x_ref, f_ref, gate_ref, g_ref, b_ref, out_ref):
    out_ref[...] = _layer_norm(DEEPNORM_ALPHA * x_ref[...] + gate_ref[0] * f_ref[...], g_ref[...], b_ref[...])


def _residual_ln(rows, xa, f, mod, part, ln_g, ln_b, n_rows):
    D = rows.D
    tm = rows.tile((512, 256, 128))
    return pl.pallas_call(
        _residual_ln_kernel,
        grid=(n_rows // tm,),
        in_specs=[pl.BlockSpec((tm, D), lambda i: (i, 0)), pl.BlockSpec((tm, D), lambda i: (i, 0)),
                  rows.mod_spec(part, tm), _resident((1, D)), _resident((1, D))],
        out_specs=pl.BlockSpec((tm, D), lambda i: (i, 0)),
        out_shape=jax.ShapeDtypeStruct((n_rows, D), F32),
        compiler_params=_cparams("parallel"),
        name="residual_ln",
    )(xa, f, mod, ln_g.reshape(1, D), ln_b.reshape(1, D))


def _moe_ln(rows, xa, mod, router, w_gu, w_down, ln_g, ln_b, n_rows):
    h, gates_t = _router(rows, xa, mod, router, n_rows)
    return _residual_ln(rows, xa, _moe(h, gates_t, w_gu, w_down), mod, 5, ln_g, ln_b, n_rows)


def _gmlp_kernel(x_ref, sh_ref, sc_ref, gate_ref, win_ref, bin_ref, vg_ref, vb_ref, ws_ref, bs_ref, wout_ref,
                 bout_ref, g_ref, b_ref, out_ref, gated_sc):
    tm = x_ref.shape[0]
    inner = wout_ref.shape[0]
    gdim = inner // CMLP_GROUPS
    x = x_ref[...]
    h = (x * (1.0 + sc_ref[0]) + sh_ref[0]).astype(BF16)
    v = jax.nn.gelu(jnp.dot(h, win_ref[:, inner:], preferred_element_type=F32) + bin_ref[:, inner:])
    v = _layer_norm(v, vg_ref[...], vb_ref[...]).astype(BF16)
    for gi in range(CMLP_GROUPS):
        cs = slice(gi * gdim, (gi + 1) * gdim)
        u = jax.nn.gelu(jnp.dot(h, win_ref[:, cs], preferred_element_type=F32) + bin_ref[:, cs])
        w_s = ws_ref[gi]
        bias = bs_ref[:, gi:gi + 1]
        for c in range(tm // CMLP_CHUNK):
            rs = slice(c * CMLP_CHUNK, (c + 1) * CMLP_CHUNK)
            mixed = jnp.dot(w_s, v[rs, cs], preferred_element_type=F32) + bias
            gated_sc[rs, cs] = (u[rs] * mixed).astype(BF16)
    y = jnp.dot(gated_sc[...], wout_ref[...], preferred_element_type=F32) + bout_ref[...]
    out_ref[...] = _layer_norm(DEEPNORM_ALPHA * x + gate_ref[0] * y, g_ref[...], b_ref[...])


def _gmlp_ln(rows, xa, mod, w_in, b_in, vg, vb, w_s, b_s, w_out, b_out, ln_g, ln_b, n_rows):
    D = rows.D
    inner = w_out.shape[0]
    tm = rows.tile((512, 256, 128))
    return pl.pallas_call(
        _gmlp_kernel,
        grid=(n_rows // tm,),
        in_specs=[pl.BlockSpec((tm, D), lambda i: (i, 0)), rows.mod_spec(0, tm), rows.mod_spec(1, tm),
                  rows.mod_spec(2, tm), _resident(w_in.shape), _resident((1, 2 * inner)), _resident((1, inner)),
                  _resident((1, inner)), _resident(w_s.shape), _resident((CMLP_CHUNK, CMLP_GROUPS)),
                  _resident(w_out.shape), _resident((1, D)), _resident((1, D)), _resident((1, D))],
        out_specs=pl.BlockSpec((tm, D), lambda i: (i, 0)),
        out_shape=jax.ShapeDtypeStruct((n_rows, D), F32),
        scratch_shapes=[pltpu.VMEM((tm, inner), BF16)],
        compiler_params=_cparams("parallel"),
        name="gmlp_ln",
    )(xa, mod, mod, mod, w_in, b_in.reshape(1, -1), vg.reshape(1, -1), vb.reshape(1, -1), w_s, b_s.T, w_out,
      b_out.reshape(1, D), ln_g.reshape(1, D), ln_b.reshape(1, D))


def _ret_proj_kernel(x_ref, sh_ref, sc_ref, w_ref, cos_ref, sin_ref, q_ref, k_ref, v_ref, sg_ref, *, nk, nv, dk):
    h = (x_ref[...] * (1.0 + sc_ref[0]) + sh_ref[0]).astype(BF16)
    cos, sin = cos_ref[...], sin_ref[...]
    k_scale = dk ** -0.5

    def rope_store(dst_ref, col0, scale):
        full = jnp.dot(h, w_ref[:, col0:col0 + nk], preferred_element_type=F32) * scale
        for j in range(nk // LANES):
            t = full[:, j * LANES:(j + 1) * LANES]
            ts = slice((j % (dk // LANES)) * LANES, (j % (dk // LANES) + 1) * LANES)
            dst_ref[:, j * LANES:(j + 1) * LANES] = (t * cos[:, ts] + _swap_pairs(t, dk // 4) * sin[:, ts]).astype(BF16)

    rope_store(q_ref, 0, 1.0)
    rope_store(k_ref, nk, k_scale)
    blk = 512
    for j in range(nv // blk):
        v_ref[:, j * blk:(j + 1) * blk] = jnp.dot(
            h, w_ref[:, 2 * nk + j * blk:2 * nk + (j + 1) * blk], preferred_element_type=F32).astype(BF16)
        gt = jnp.dot(h, w_ref[:, 2 * nk + nv + j * blk:2 * nk + nv + (j + 1) * blk], preferred_element_type=F32)
        sg_ref[:, j * blk:(j + 1) * blk] = (gt * jax.nn.sigmoid(gt)).astype(BF16)


def _ret_project(rows, xa, mod, w, cos, sin):
    D = rows.D
    nk = D
    nv = (w.shape[1] - 2 * nk) // 2
    dk = nk // RET_HEADS
    tm = rows.tile((512, 256))
    row_spec = lambda width: pl.BlockSpec((tm, width), lambda i: (i, 0))
    return pl.pallas_call(
        functools.partial(_ret_proj_kernel, nk=nk, nv=nv, dk=dk),
        grid=(rows.n // tm,),
        in_specs=[row_spec(D), rows.mod_spec(0, tm), rows.mod_spec(1, tm), _resident(w.shape),
                  rows.pos_spec(tm, dk), rows.pos_spec(tm, dk)],
        out_specs=[row_spec(nk), row_spec(nk), row_spec(nv), row_spec(nv)],
        out_shape=[jax.ShapeDtypeStruct((rows.n, nk), BF16), jax.ShapeDtypeStruct((rows.n, nk), BF16),
                   jax.ShapeDtypeStruct((rows.n, nv), BF16), jax.ShapeDtypeStruct((rows.n, nv), BF16)],
        compiler_params=_cparams("parallel"),
        name="ret_proj",
    )(xa, mod, mod, w, cos, sin)


RET_SCAN_CHUNK = 256


def _ret_scan_kernel(lg_ref, q_ref, k_ref, v_ref, sg_ref, sf_ref, sb_ref, o_ref, sf_out, sb_out,
                     state_sc, part_sc, inner_sc, cross_sc, weight_sc):
    ch = RET_SCAN_CHUNK
    n_chunks = q_ref.shape[0] // ch
    dk, dv = q_ref.shape[1], v_ref.shape[1]
    head = pl.program_id(1)

    chunk_decay = []
    for d in range(2):
        lg = lg_ref[d, head]
        ii = lax.broadcasted_iota(jnp.int32, (ch, ch), 0)
        jj = lax.broadcasted_iota(jnp.int32, (ch, ch), 1)
        dist = (ii - jj) if d == 0 else (jj - ii)
        inner_sc[d] = jnp.where(dist >= 0, jnp.exp(jnp.maximum(dist, 0).astype(F32) * lg), 0.0)
        row_v = lax.broadcasted_iota(jnp.int32, (ch, dv), 0)
        row_k = lax.broadcasted_iota(jnp.int32, (ch, dk), 0)
        cross_sc[d] = jnp.exp(((row_v if d == 0 else ch - 1 - row_v) + 1).astype(F32) * lg)
        weight_sc[d] = jnp.exp((ch - 1 - (row_k if d == 0 else ch - 1 - row_k)).astype(F32) * lg)
        chunk_decay.append(jnp.exp(jnp.full((1, 1), ch, F32) * lg))
    state_sc[0] = sf_ref[0, 0]
    state_sc[1] = sb_ref[0, 0]

    def visit(d, c):
        rows = pl.ds(pl.multiple_of(c * ch, ch), ch)
        q, k, v = q_ref[rows, :], k_ref[rows, :], v_ref[rows, :]
        scores = lax.dot_general(q, k, (((1,), (1,)), ((), ())), preferred_element_type=F32) * inner_sc[d]
        state = state_sc[d]
        out = (jnp.dot(scores.astype(BF16), v, preferred_element_type=F32)
               + jnp.dot(q, state.astype(BF16), preferred_element_type=F32) * cross_sc[d])
        kw = (k.astype(F32) * weight_sc[d]).astype(BF16)
        state_sc[d] = state * chunk_decay[d] + lax.dot_general(
            kw, v, (((0,), (0,)), ((), ())), preferred_element_type=F32)
        return rows, out

    def finish(rows, o):
        o = o * lax.rsqrt(jnp.mean(o * o, axis=-1, keepdims=True) + RMS_EPS)
        o_ref[rows, :] = (sg_ref[rows, :].astype(F32) * o).astype(BF16)

    def first_visits(t, carry):
        for d, c in ((0, t), (1, n_chunks - 1 - t)):
            rows, out = visit(d, c)
            part_sc[rows, :] = out
        return carry

    def second_visits(t, carry):
        for d, c in ((0, t), (1, n_chunks - 1 - t)):
            rows, out = visit(d, c)
            finish(rows, out + part_sc[rows, :])
        return carry

    lax.fori_loop(0, n_chunks // 2, first_visits, 0)
    if n_chunks % 2 == 1:
        rows, out_f = visit(0, n_chunks // 2)
        _, out_b = visit(1, n_chunks // 2)
        finish(rows, out_f + out_b)
    lax.fori_loop((n_chunks + 1) // 2, n_chunks, second_visits, 0)
    sf_out[0, 0] = state_sc[0]
    sb_out[0, 0] = state_sc[1]


def _ret_scan_segment(rows, log_gamma, q, k, v, sg, state_f, state_b, *, seg_len, first_block):
    B = rows.B
    dk = q.shape[1] // RET_HEADS
    dv = v.shape[1] // RET_HEADS
    ch = RET_SCAN_CHUNK
    assert seg_len % ch == 0
    kspec = pl.BlockSpec((seg_len, dk), lambda b, h: (first_block + b, h))
    vspec = pl.BlockSpec((seg_len, dv), lambda b, h: (first_block + b, h))
    sspec = pl.BlockSpec((1, 1, dk, dv), lambda b, h: (b, h, 0, 0))
    state_shape = jax.ShapeDtypeStruct((B, RET_HEADS, dk, dv), F32)
    return pl.pallas_call(
        _ret_scan_kernel,
        grid=(B, RET_HEADS),
        in_specs=[pl.BlockSpec(memory_space=pltpu.SMEM), kspec, kspec, vspec, vspec, sspec, sspec],
        out_specs=[pl.BlockSpec((seg_len, dv), lambda b, h: (b, h)), sspec, sspec],
        out_shape=[jax.ShapeDtypeStruct((B * seg_len, v.shape[1]), BF16), state_shape, state_shape],
        scratch_shapes=[pltpu.VMEM((2, dk, dv), F32), pltpu.VMEM((seg_len, dv), F32),
                        pltpu.VMEM((2, ch, ch), F32), pltpu.VMEM((2, ch, dv), F32), pltpu.VMEM((2, ch, dk), F32)],
        compiler_params=_cparams("parallel", "parallel"),
        name="ret_scan",
    )(log_gamma, q, k, v, sg, state_f, state_b)


def _retention(rows, log_gamma, q, k, v, sg):
    B, S, C = rows.B, rows.S, rows.C
    assert rows.n_lat % C == 0
    dk, dv = q.shape[1] // RET_HEADS, v.shape[1] // RET_HEADS
    zeros = jnp.zeros((B, RET_HEADS, dk, dv), F32)
    o_ctx, state_f, state_b = _ret_scan_segment(rows, log_gamma, q, k, v, sg, zeros, zeros,
                                                seg_len=C, first_block=rows.n_lat // C)
    o_lat, _, _ = _ret_scan_segment(rows, log_gamma, q, k, v, sg, state_f, state_b, seg_len=S, first_block=0)
    return [o_lat, o_ctx]


def kernel(x, c, ctx, c_ctx, l0_ada_w, l0_ada_b, l0_ln1_g, l0_ln1_b, l0_ln2_g, l0_ln2_b, l0_attn_wqkv, l0_attn_q_norm, l0_attn_k_norm, l0_attn_wo, l0_ffn_w_gu, l0_ffn_w_down, l1_ada_w, l1_ada_b, l1_ln1_g, l1_ln1_b, l1_ln2_g, l1_ln2_b, l1_cmlp_w_in, l1_cmlp_b_in, l1_cmlp_v_norm_g, l1_cmlp_v_norm_b, l1_cmlp_w_s, l1_cmlp_b_s, l1_cmlp_w_out, l1_cmlp_b_out, l1_moe_router, l1_moe_w_gu, l1_moe_w_down, l2_ada_w, l2_ada_b, l2_ln1_g, l2_ln1_b, l2_ln2_g, l2_ln2_b, l2_ret_wqkvg, l2_ret_decay, l2_ret_wo, l2_ffn_w_gu, l2_ffn_w_down, l3_ada_w, l3_ada_b, l3_ln1_g, l3_ln1_b, l3_ln2_g, l3_ln2_b, l3_attn_wqkv, l3_attn_q_norm, l3_attn_k_norm, l3_attn_wo, l3_moe_router, l3_moe_w_gu, l3_moe_w_down):
    B, S, D = x.shape
    C = ctx.shape[1]
    rows = _Rows(B, S, C, D)
    assert S % GRID_W == 0 and C % RET_CHUNK == 0 and S % RET_CHUNK == 0

    xa = jnp.concatenate([x.reshape(rows.n_lat, D), ctx.reshape(rows.n_ctx, D)], axis=0)
    n_cond = -(-(B + 1) // 8) * 8
    cond = jnp.concatenate([c, c_ctx[None, :], jnp.zeros((n_cond - B - 1, D), F32)], axis=0)

    def modulation(ada_w, ada_b):
        return _ada(cond, ada_w, ada_b).reshape(n_cond * 6, 1, D)

    pad = rows.tile((512, 256))
    attn_cos, attn_sin = _rope_tables(S, ATTN_HEAD_DIM, pad)
    ret_cos, ret_sin = _rope_tables(S, D // RET_HEADS, pad)
    zero_bias = jnp.zeros((D,), F32)
    bf = lambda w: w.astype(BF16)

    def attention_layer(xa, mod, wqkv, qg, kg, wo, ln_g, ln_b, with_ctx):
        q, k, v = _attn_project(rows, xa, mod, bf(wqkv), qg, kg, attn_cos, attn_sin)
        o = _attention(rows, q, k, v, with_ctx)
        return _proj_ln(rows, [o], bf(wo), zero_bias, xa, mod, ln_g, ln_b)

    def retention_layer(xa, mod, wqkvg, decay, wo, ln_g, ln_b):
        log_gamma = -jnp.exp(decay.astype(F32))
        q, k, v, sg = _ret_project(rows, xa, mod, bf(wqkvg), ret_cos, ret_sin)
        o_segments = _retention(rows, log_gamma, q, k, v, sg)
        return _proj_ln(rows, o_segments, bf(wo), zero_bias, xa, mod, ln_g, ln_b)

    mod = modulation(l0_ada_w, l0_ada_b)
    xa = attention_layer(xa, mod, l0_attn_wqkv, l0_attn_q_norm, l0_attn_k_norm, l0_attn_wo, l0_ln1_g, l0_ln1_b, True)
    xa = _ffn_ln(rows, xa, mod, bf(l0_ffn_w_gu), bf(l0_ffn_w_down), l0_ln2_g, l0_ln2_b, rows.n)

    mod = modulation(l1_ada_w, l1_ada_b)
    xa = _gmlp_ln(rows, xa, mod, bf(l1_cmlp_w_in), l1_cmlp_b_in, l1_cmlp_v_norm_g, l1_cmlp_v_norm_b,
                  bf(l1_cmlp_w_s), l1_cmlp_b_s, bf(l1_cmlp_w_out), l1_cmlp_b_out, l1_ln1_g, l1_ln1_b, rows.n)
    xa = _moe_ln(rows, xa, mod, l1_moe_router, bf(l1_moe_w_gu), bf(l1_moe_w_down), l1_ln2_g, l1_ln2_b, rows.n)

    mod = modulation(l2_ada_w, l2_ada_b)
    xa = retention_layer(xa, mod, l2_ret_wqkvg, l2_ret_decay, l2_ret_wo, l2_ln1_g, l2_ln1_b)
    xa = _ffn_ln(rows, xa, mod, bf(l2_ffn_w_gu), bf(l2_ffn_w_down), l2_ln2_g, l2_ln2_b, rows.n)

    mod = modulation(l3_ada_w, l3_ada_b)
    xl = attention_layer(xa, mod, l3_attn_wqkv, l3_attn_q_norm, l3_attn_k_norm, l3_attn_wo, l3_ln1_g, l3_ln1_b, False)
    xl = _moe_ln(rows, xl, mod, l3_moe_router, bf(l3_moe_w_gu), bf(l3_moe_w_down), l3_ln2_g, l3_ln2_b, rows.n_lat)
    return xl.reshape(B, S, D)
```

```python
import functools

import jax
import jax.numpy as jnp
from jax import lax
from jax.experimental import pallas as pl
from jax.experimental.pallas import tpu as pltpu

F32 = jnp.float32
BF16 = jnp.bfloat16

DEPTH = 4
GRID_W = 64
ROPE_THETA = 10000.0
DEEPNORM_ALPHA = (2 * DEPTH) ** 0.25
LN_EPS = 1e-5
RMS_EPS = 1e-6
ATTN_HEAD_DIM = 128
ATTN_KV_HEADS = 2
CMLP_CHUNK = 128
CMLP_GROUPS = 8
RET_HEADS = 4
RET_CHUNK = 128
MOE_EXPERTS = 8

LANES = 128
VMEM_LIMIT_BYTES = 56 * 1024 * 1024
NEG_BIG = -1e30


def _cparams(*sem):
    return pltpu.CompilerParams(dimension_semantics=sem, vmem_limit_bytes=VMEM_LIMIT_BYTES)


def _resident(shape):
    nd = len(shape)
    return pl.BlockSpec(shape, lambda *_: (0,) * nd)


def _layer_norm(z, g, b):
    mu = jnp.mean(z, axis=-1, keepdims=True)
    zc = z - mu
    var = jnp.mean(zc * zc, axis=-1, keepdims=True)
    return zc * lax.rsqrt(var + LN_EPS) * g + b


def _pick_tile(candidates, *extents):
    for t in candidates:
        if all(e % t == 0 for e in extents):
            return t
    raise ValueError(f"no tile in {candidates} divides {extents}")


class _Rows:
    def __init__(self, batch, seq, ctx_len, dim):
        self.B, self.S, self.C, self.D = batch, seq, ctx_len, dim
        self.n_lat = batch * seq
        self.n_ctx = batch * ctx_len
        self.n = self.n_lat + self.n_ctx

    def tile(self, candidates):
        return _pick_tile(candidates, self.S, self.n_ctx)

    def mod_spec(self, part, tm):
        S, B = self.S, self.B
        return pl.BlockSpec((1, 1, self.D), lambda i, *_: (jnp.minimum(i * tm // S, B) * 6 + part, 0, 0))

    def pos_spec(self, tm, width):
        n_lat_tiles, per_seq = self.n_lat // tm, self.S // tm
        return pl.BlockSpec((tm, width), lambda i, *_: (jnp.where(i < n_lat_tiles, i % per_seq, per_seq), 0))


def _dot_3pass(a, b):
    def split(t):
        hi = t.astype(BF16)
        return hi, (t - hi.astype(F32)).astype(BF16)

    (a_hi, a_lo), (b_hi, b_lo) = split(a), split(b)
    rows = a.shape[0]
    by_b_hi = jnp.dot(jnp.concatenate([a_hi, a_lo], axis=0), b_hi, preferred_element_type=F32)
    return by_b_hi[:rows] + by_b_hi[rows:] + jnp.dot(a_hi, b_lo, preferred_element_type=F32)


def _ada_kernel(c_ref, w_ref, b_ref, o_ref):
    c = c_ref[...]
    s = c * jax.nn.sigmoid(c)
    o_ref[...] = _dot_3pass(s, w_ref[...]) + b_ref[...]


def _ada(cond, w, b):
    r, d = cond.shape
    n = w.shape[1]
    tn = _pick_tile((1536, 1024, 512, 128), n)
    return pl.pallas_call(
        _ada_kernel,
        grid=(n // tn,),
        in_specs=[_resident((r, d)), pl.BlockSpec((d, tn), lambda j: (0, j)), pl.BlockSpec((1, tn), lambda j: (0, j))],
        out_specs=pl.BlockSpec((r, tn), lambda j: (0, j)),
        out_shape=jax.ShapeDtypeStruct((r, n), F32),
        compiler_params=_cparams("parallel"),
        name="ada",
    )(cond, w, b.reshape(1, n))


def _rope_tables(seq, head_dim, pad_rows):
    nf = head_dim // 4
    inv_freq = ROPE_THETA ** (-jnp.arange(nf, dtype=F32) / nf)
    t = jnp.arange(seq, dtype=jnp.int32)
    row = (t // GRID_W).astype(F32)[:, None] * inv_freq
    col = (t % GRID_W).astype(F32)[:, None] * inv_freq
    cos = jnp.concatenate([jnp.cos(row), jnp.cos(row), jnp.cos(col), jnp.cos(col)], axis=-1)
    sin = jnp.concatenate([-jnp.sin(row), jnp.sin(row), -jnp.sin(col), jnp.sin(col)], axis=-1)
    cos = jnp.concatenate([cos, jnp.ones((pad_rows, head_dim), F32)], axis=0)
    sin = jnp.concatenate([sin, jnp.zeros((pad_rows, head_dim), F32)], axis=0)
    return cos, sin


def _swap_pairs(y, half):
    if 2 * half == LANES:
        return pltpu.roll(y, half, 1)
    lane = lax.broadcasted_iota(jnp.int32, y.shape, 1)
    return jnp.where(lane % (2 * half) < half, pltpu.roll(y, LANES - half, 1), pltpu.roll(y, half, 1))


KEY_BLOCK = 256
LOG2_E = 1.4426950408889634


def _qkv_kernel(x_ref, sh_ref, sc_ref, w_ref, qg_ref, kg_ref, cos_ref, sin_ref, q_ref, kt_ref, v_ref, *, nq, nkv):
    dh = ATTN_HEAD_DIM
    h = (x_ref[...] * (1.0 + sc_ref[0]) + sh_ref[0]).astype(BF16)
    cos, sin = cos_ref[...], sin_ref[...]
    scale = dh ** -0.5 * LOG2_E

    def normed_rope(t, g):
        y = t * lax.rsqrt(jnp.mean(t * t, axis=-1, keepdims=True) + RMS_EPS) * g
        return y * cos + _swap_pairs(y, dh // 4) * sin

    t = jnp.dot(h, w_ref[...], preferred_element_type=F32)
    for hd in range(nq // dh):
        q_ref[:, hd * dh:(hd + 1) * dh] = (normed_rope(t[:, hd * dh:(hd + 1) * dh], qg_ref[...]) * scale).astype(BF16)
    k = jnp.concatenate([normed_rope(t[:, nq + hd * dh:nq + (hd + 1) * dh], kg_ref[...]) for hd in range(nkv // dh)],
                        axis=1)
    for j in range(kt_ref.shape[0]):
        kt_ref[j] = k[j * KEY_BLOCK:(j + 1) * KEY_BLOCK, :].T.astype(BF16)
    v_ref[...] = t[:, nq + nkv:].astype(BF16)


def _attn_project(rows, xa, mod, w, qg, kg, cos, sin):
    D = rows.D
    nkv = ATTN_KV_HEADS * ATTN_HEAD_DIM
    nq = w.shape[1] - 2 * nkv
    tm = rows.tile((512, 256))
    row_spec = lambda width: pl.BlockSpec((tm, width), lambda i: (i, 0))
    return pl.pallas_call(
        functools.partial(_qkv_kernel, nq=nq, nkv=nkv),
        grid=(rows.n // tm,),
        in_specs=[row_spec(D), rows.mod_spec(0, tm), rows.mod_spec(1, tm), _resident(w.shape),
                  _resident((1, ATTN_HEAD_DIM)), _resident((1, ATTN_HEAD_DIM)),
                  rows.pos_spec(tm, ATTN_HEAD_DIM), rows.pos_spec(tm, ATTN_HEAD_DIM)],
        out_specs=[row_spec(nq), pl.BlockSpec((tm // KEY_BLOCK, nkv, KEY_BLOCK), lambda i: (i, 0, 0)), row_spec(nkv)],
        out_shape=[jax.ShapeDtypeStruct((rows.n, nq), BF16),
                   jax.ShapeDtypeStruct((rows.n // KEY_BLOCK, nkv, KEY_BLOCK), BF16),
                   jax.ShapeDtypeStruct((rows.n, nkv), BF16)],
        compiler_params=_cparams("parallel"),
        name="attn_qkv",
    )(xa, mod, mod, w, qg.reshape(1, -1), kg.reshape(1, -1), cos, sin)


def _attn_kernel(q_ref, kc_ref, vc_ref, kl_ref, vl_ref, o_ref, m_sc, acc_sc, *, tq, blocks_per_chunk, n_lat_tiles):
    dh = ATTN_HEAD_DIM
    group = q_ref.shape[1] // dh // ATTN_KV_HEADS
    is_latent = pl.program_id(1) < n_lat_tiles
    n_lat_chunks = jnp.where(is_latent, kl_ref.shape[0] // blocks_per_chunk, 0)
    chunk = blocks_per_chunk * KEY_BLOCK

    heads = range(ATTN_KV_HEADS)
    head_rows = group * tq
    gs = [slice(g * dh, (g + 1) * dh) for g in heads]
    qs = [jnp.concatenate([q_ref[:, (g * group + j) * dh:(g * group + j + 1) * dh] for j in range(group)], axis=0)
          for g in heads]
    m_sc[...] = jnp.full(m_sc.shape, NEG_BIG, F32)
    acc_sc[...] = jnp.zeros(acc_sc.shape, F32)

    def online_softmax_step(kts, vs):
        width = kts[0].shape[1]
        s = jnp.concatenate([jnp.dot(qs[g], kts[g], preferred_element_type=F32) for g in heads], axis=0)
        m_old = m_sc[...]
        m_new = jnp.maximum(m_old, jnp.max(s, axis=-1, keepdims=True))
        p = jnp.exp2((s - jnp.concatenate([m_new] * (width // LANES), axis=1)).astype(BF16))
        alpha = jnp.exp2(m_old - m_new)
        ones = jnp.ones((width, dh), BF16)
        pv = jnp.concatenate([jnp.dot(p[g * head_rows:(g + 1) * head_rows], jnp.concatenate([vs[g], ones], axis=1),
                                      preferred_element_type=F32) for g in heads], axis=0)
        acc_sc[...] = jnp.concatenate([alpha, alpha], axis=1) * acc_sc[...] + pv
        m_sc[...] = m_new

    for c in range(kc_ref.shape[0]):
        online_softmax_step([kc_ref[c, gs[g], :] for g in heads],
                            [vc_ref[c * KEY_BLOCK:(c + 1) * KEY_BLOCK, gs[g]] for g in heads])

    def latent_chunk(c, carry):
        r0 = pl.multiple_of(c * chunk, chunk)
        online_softmax_step(
            [jnp.concatenate([kl_ref[c * blocks_per_chunk + j, gs[g], :] for j in range(blocks_per_chunk)], axis=1)
             for g in heads],
            [vl_ref[pl.ds(r0, chunk), gs[g]] for g in heads])
        return carry

    lax.fori_loop(0, n_lat_chunks, latent_chunk, 0)
    acc = acc_sc[...]
    o = acc[:, :dh] / acc[:, dh:]
    for hd in range(ATTN_KV_HEADS * group):
        o_ref[:, hd * dh:(hd + 1) * dh] = o[hd * tq:(hd + 1) * tq].astype(BF16)


def _attention(rows, q, kt, v, with_ctx):
    B, S, C = rows.B, rows.S, rows.C
    nq = q.shape[1]
    nkv = v.shape[1]
    assert S % KEY_BLOCK == 0 and C % KEY_BLOCK == 0 and rows.n_lat % C == 0
    tq = _pick_tile((256, 128), S, C)
    blocks_per_chunk = next(n for n in (8, 4, 2, 1) if S % (n * KEY_BLOCK) == 0)
    lat_tiles, ctx_tiles = S // tq, C // tq
    tiles = lat_tiles + (ctx_tiles if with_ctx else 0)
    group = nq // nkv

    def q_map(b, i):
        return (jnp.where(i < lat_tiles, b * lat_tiles + i, rows.n_lat // tq + b * ctx_tiles + (i - lat_tiles)), 0)

    kt_ctx = pl.BlockSpec((C // KEY_BLOCK, nkv, KEY_BLOCK), lambda b, i: (rows.n_lat // C + b, 0, 0))
    kt_lat = pl.BlockSpec((S // KEY_BLOCK, nkv, KEY_BLOCK), lambda b, i: (b, 0, 0))
    v_ctx = pl.BlockSpec((C, nkv), lambda b, i: (rows.n_lat // C + b, 0))
    v_lat = pl.BlockSpec((S, nkv), lambda b, i: (b, 0))
    return pl.pallas_call(
        functools.partial(_attn_kernel, tq=tq, blocks_per_chunk=blocks_per_chunk, n_lat_tiles=lat_tiles),
        grid=(B, tiles),
        in_specs=[pl.BlockSpec((tq, nq), q_map), kt_ctx, v_ctx, kt_lat, v_lat],
        out_specs=pl.BlockSpec((tq, nq), q_map),
        out_shape=jax.ShapeDtypeStruct((rows.n if with_ctx else rows.n_lat, nq), BF16),
        scratch_shapes=[pltpu.VMEM((ATTN_KV_HEADS * group * tq, LANES), F32),
                        pltpu.VMEM((ATTN_KV_HEADS * group * tq, 2 * ATTN_HEAD_DIM), F32)],
        compiler_params=_cparams("parallel", "parallel"),
        name="attn_core",
    )(q, kt, v, kt, v)


def _proj_ln_kernel(*refs, first_tiles):
    n_seg = len(first_tiles) - 1
    o_refs = refs[:n_seg]
    w_ref, bias_ref, x_ref, gate_ref, g_ref, b_ref, out_ref = refs[n_seg:]
    i = pl.program_id(0)
    for seg, o_ref in enumerate(o_refs):
        @pl.when((i >= first_tiles[seg]) & (i < first_tiles[seg + 1]))
        def _():
            y = jnp.dot(o_ref[...], w_ref[...], preferred_element_type=F32) + bias_ref[...]
            out_ref[...] = _layer_norm(DEEPNORM_ALPHA * x_ref[...] + gate_ref[0] * y, g_ref[...], b_ref[...])


def _proj_ln(rows, o_segments, w, bias, xa, mod, ln_g, ln_b):
    D = rows.D
    kdim = w.shape[0]
    tm = rows.tile((512, 256, 128))
    first_tiles = [0]
    for o in o_segments:
        assert o.shape[0] % tm == 0
        first_tiles.append(first_tiles[-1] + o.shape[0] // tm)
    n_rows = first_tiles[-1] * tm

    def segment_spec(seg):
        first, count = first_tiles[seg], first_tiles[seg + 1] - first_tiles[seg]
        return pl.BlockSpec((tm, kdim), lambda i: (jnp.clip(i - first, 0, count - 1), 0))

    return pl.pallas_call(
        functools.partial(_proj_ln_kernel, first_tiles=tuple(first_tiles)),
        grid=(n_rows // tm,),
        in_specs=[segment_spec(seg) for seg in range(len(o_segments))] + [
            _resident(w.shape), _resident((1, D)), pl.BlockSpec((tm, D), lambda i: (i, 0)), rows.mod_spec(2, tm),
            _resident((1, D)), _resident((1, D))],
        out_specs=pl.BlockSpec((tm, D), lambda i: (i, 0)),
        out_shape=jax.ShapeDtypeStruct((n_rows, D), F32),
        compiler_params=_cparams("parallel"),
        name="proj_ln",
    )(*o_segments, w, bias.reshape(1, D), xa, mod, ln_g.reshape(1, D), ln_b.reshape(1, D))


def _ffn_kernel(x_ref, sh_ref, sc_ref, gate_ref, wg_ref, wu_ref, wd_ref, g_ref, b_ref, out_ref, h_sc, acc_sc):
    f = pl.program_id(1)

    @pl.when(f == 0)
    def _():
        h_sc[...] = (x_ref[...] * (1.0 + sc_ref[0]) + sh_ref[0]).astype(BF16)
        acc_sc[...] = jnp.zeros(acc_sc.shape, F32)

    h = h_sc[...]
    gt = jnp.dot(h, wg_ref[...], preferred_element_type=F32)
    up = jnp.dot(h, wu_ref[...], preferred_element_type=F32)
    act = (gt * jax.nn.sigmoid(gt) * up).astype(BF16)
    acc_sc[...] += jnp.dot(act, wd_ref[...], preferred_element_type=F32)

    @pl.when(f == pl.num_programs(1) - 1)
    def _():
        out_ref[...] = _layer_norm(DEEPNORM_ALPHA * x_ref[...] + gate_ref[0] * acc_sc[...], g_ref[...], b_ref[...])


def _ffn_ln(rows, xa, mod, w_gu, w_down, ln_g, ln_b, n_rows):
    D = rows.D
    F = w_down.shape[0]
    tm = rows.tile((512, 256, 128))
    tf = _pick_tile((1792, 512, 256, 128), F)
    nf = F // tf
    return pl.pallas_call(
        _ffn_kernel,
        grid=(n_rows // tm, nf),
        in_specs=[pl.BlockSpec((tm, D), lambda i, f: (i, 0)),
                  rows.mod_spec(3, tm), rows.mod_spec(4, tm), rows.mod_spec(5, tm),
                  pl.BlockSpec((D, tf), lambda i, f: (0, f)), pl.BlockSpec((D, tf), lambda i, f: (0, nf + f)),
                  pl.BlockSpec((tf, D), lambda i, f: (f, 0)), _resident((1, D)), _resident((1, D))],
        out_specs=pl.BlockSpec((tm, D), lambda i, f: (i, 0)),
        out_shape=jax.ShapeDtypeStruct((n_rows, D), F32),
        scratch_shapes=[pltpu.VMEM((tm, D), BF16), pltpu.VMEM((tm, D), F32)],
        compiler_params=_cparams("parallel", "arbitrary"),
        name="ffn_ln",
    )(xa, mod, mod, mod, w_gu, w_gu, w_down, ln_g.reshape(1, D), ln_b.reshape(1, D))


MOE_ROW_BLOCKS = (512, 256, 128)


def _router_kernel(x_ref, sh_ref, sc_ref, r_ref, h_ref, gates_ref, *, n_experts):
    h = x_ref[...] * (1.0 + sc_ref[0]) + sh_ref[0]
    h_ref[...] = h.astype(BF16)
    logits = _dot_3pass(h, r_ref[...])
    lane = lax.broadcasted_iota(jnp.int32, logits.shape, 1)
    lowest = jnp.finfo(F32).min
    lg = jnp.where(lane < n_experts, logits, lowest)
    m1 = jnp.max(lg, axis=-1, keepdims=True)
    i1 = jnp.min(jnp.where(lg == m1, lane, LANES), axis=-1, keepdims=True)
    lg2 = jnp.where(lane == i1, lowest, lg)
    m2 = jnp.max(lg2, axis=-1, keepdims=True)
    i2 = jnp.min(jnp.where(lg2 == m2, lane, LANES), axis=-1, keepdims=True)
    e2 = jnp.exp(m2 - m1)
    denom = 1.0 + e2
    gates = jnp.where(lane == i1, 1.0 / denom, 0.0) + jnp.where(lane == i2, e2 / denom, 0.0)
    gates_ref[...] = gates.T[:n_experts, :]


def _router(rows, xa, mod, router, n_rows):
    D = rows.D
    n_experts = router.shape[1]
    tm = rows.tile((512, 256, 128))
    router_pad = jnp.pad(router, ((0, 0), (0, LANES - n_experts)))
    return pl.pallas_call(
        functools.partial(_router_kernel, n_experts=n_experts),
        grid=(n_rows // tm,),
        in_specs=[pl.BlockSpec((tm, D), lambda i: (i, 0)), rows.mod_spec(3, tm), rows.mod_spec(4, tm),
                  _resident((D, LANES))],
        out_specs=[pl.BlockSpec((tm, D), lambda i: (i, 0)), pl.BlockSpec((n_experts, tm), lambda i: (0, i))],
        out_shape=[jax.ShapeDtypeStruct((n_rows, D), BF16), jax.ShapeDtypeStruct((n_experts, n_rows), F32)],
        compiler_params=_cparams("parallel"),
        name="moe_router",
    )(xa, mod, mod, router_pad)


def _plan_kernel(g_ref, rank_ref, cnt_ref):
    n_experts, tile = g_ref.shape
    routed = g_ref[...] > 0.0
    ones = jnp.where(routed, 1.0, 0.0)
    before = jnp.where(lax.broadcasted_iota(jnp.int32, (LANES, LANES), 0)
                       < lax.broadcasted_iota(jnp.int32, (LANES, LANES), 1), 1.0, 0.0)
    seen = jnp.zeros((n_experts, 1), F32)
    for c in range(tile // LANES):
        cs = slice(c * LANES, (c + 1) * LANES)
        rank = jnp.dot(ones[:, cs], before, preferred_element_type=F32) + seen
        rank_ref[:, cs] = jnp.where(routed[:, cs], rank, -1.0).astype(jnp.int32)
        seen = seen + jnp.sum(ones[:, cs], axis=1, keepdims=True)
    cnt_ref[0] = jnp.broadcast_to(seen, (n_experts, LANES)).astype(jnp.int32)


def _moe_plan(gates_t, tile):
    n_experts, n_rows = gates_t.shape
    n_tiles = n_rows // tile
    rank, counts = pl.pallas_call(
        _plan_kernel,
        grid=(n_tiles,),
        in_specs=[pl.BlockSpec((n_experts, tile), lambda s: (0, s))],
        out_specs=[pl.BlockSpec((n_experts, tile), lambda s: (0, s)),
                   pl.BlockSpec((1, n_experts, LANES), lambda s: (s, 0, 0))],
        out_shape=[jax.ShapeDtypeStruct((n_experts, n_rows), jnp.int32),
                   jax.ShapeDtypeStruct((n_tiles, n_experts, LANES), jnp.int32)],
        compiler_params=_cparams("parallel"),
        name="moe_plan",
    )(gates_t)
    count = counts[:, :, 0].reshape(-1)
    smallest = MOE_ROW_BLOCKS[-1]
    return rank, (count + smallest - 1) // smallest * smallest


def _moe_kernel(nrows_ref, h_ref, rank_ref, gate_ref, wg_ref, wu_ref, wd_ref, out_ref, xg_sc, y_sc):
    s, e, f = pl.program_id(0), pl.program_id(1), pl.program_id(2)
    tile = h_ref.shape[0]
    n_rows = nrows_ref[s * pl.num_programs(1) + e]
    big = MOE_ROW_BLOCKS[0]

    @pl.when((e == 0) & (f == 0))
    def _():
        out_ref[...] = jnp.zeros(out_ref.shape, F32)

    def for_each_block(fn):
        def body(r, carry):
            fn(r * big, big)
            return carry

        lax.fori_loop(0, n_rows // big, body, 0)
        for size in MOE_ROW_BLOCKS[1:]:
            @pl.when(n_rows // size % 2 == 1)
            def _():
                fn(n_rows // (2 * size) * (2 * size), size)

    def rows_at(start, size):
        return pl.ds(pl.multiple_of(start, MOE_ROW_BLOCKS[-1]), size)

    def slot_matches(start, size):
        slot = lax.broadcasted_iota(jnp.int32, (size, tile), 0) + start
        return slot == rank_ref[pl.ds(e, 1), :]

    def gather(start, size):
        onehot = jnp.where(slot_matches(start, size), 1.0, 0.0).astype(BF16)
        xg_sc[rows_at(start, size), :] = jnp.dot(onehot, h_ref[...], preferred_element_type=F32).astype(BF16)
        y_sc[rows_at(start, size), :] = jnp.zeros((size, y_sc.shape[1]), F32)

    def expert(start, size):
        xb = xg_sc[rows_at(start, size), :]
        gt = jnp.dot(xb, wg_ref[0], preferred_element_type=F32)
        up = jnp.dot(xb, wu_ref[0], preferred_element_type=F32)
        act = (gt * jax.nn.sigmoid(gt) * up).astype(BF16)
        y_sc[rows_at(start, size), :] += jnp.dot(act, wd_ref[0], preferred_element_type=F32)

    def scatter(start, size):
        weighted = jnp.where(slot_matches(start, size), gate_ref[pl.ds(e, 1), :], 0.0).astype(BF16)
        out_ref[...] += lax.dot_general(weighted, y_sc[rows_at(start, size), :].astype(BF16),
                                        (((0,), (0,)), ((), ())), preferred_element_type=F32)

    @pl.when(f == 0)
    def _():
        for_each_block(gather)

    for_each_block(expert)

    @pl.when(f == pl.num_programs(2) - 1)
    def _():
        for_each_block(scatter)


def _moe(h, gates_t, w_gu, w_down):
    n_rows, D = h.shape
    E, F = w_down.shape[0], w_down.shape[1]
    tile = _pick_tile((2176, 2048, 1024, 512, 256, 128), n_rows)
    tf = _pick_tile((512, 256, 128), F)
    nf = F // tf
    rank, n_block_rows = _moe_plan(gates_t, tile)
    max_rows = -(-tile // MOE_ROW_BLOCKS[-1]) * MOE_ROW_BLOCKS[-1]
    grid_spec = pltpu.PrefetchScalarGridSpec(
        num_scalar_prefetch=1,
        grid=(n_rows // tile, E, nf),
        in_specs=[pl.BlockSpec((tile, D), lambda s, e, f, *_: (s, 0)),
                  pl.BlockSpec((E, tile), lambda s, e, f, *_: (0, s)),
                  pl.BlockSpec((E, tile), lambda s, e, f, *_: (0, s)),
                  pl.BlockSpec((1, D, tf), lambda s, e, f, *_: (e, 0, f)),
                  pl.BlockSpec((1, D, tf), lambda s, e, f, *_: (e, 0, nf + f)),
                  pl.BlockSpec((1, tf, D), lambda s, e, f, *_: (e, f, 0))],
        out_specs=pl.BlockSpec((tile, D), lambda s, e, f, *_: (s, 0)),
        scratch_shapes=[pltpu.VMEM((max_rows, D), BF16), pltpu.VMEM((max_rows, D), F32)],
    )
    return pl.pallas_call(
        _moe_kernel,
        grid_spec=grid_spec,
        out_shape=jax.ShapeDtypeStruct((n_rows, D), F32),
        compiler_params=_cparams("parallel", "arbitrary", "arbitrary"),
        name="moe_experts",
    )(n_block_rows, h, rank, gates_t, w_gu, w_gu, w_down)


def _residual_ln_kernel(x_ref, f_ref, gate_ref, g_ref, b_ref, out_ref):
    out_ref[...] = _layer_norm(DEEPNORM_ALPHA * x_ref[...] + gate_ref[0] * f_ref[...], g_ref[...], b_ref[...])


def _residual_ln(rows, xa, f, mod, part, ln_g, ln_b, n_rows):
    D = rows.D
    tm = rows.tile((512, 256, 128))
    return pl.pallas_call(
        _residual_ln_kernel,
        grid=(n_rows // tm,),
        in_specs=[pl.BlockSpec((tm, D), lambda i: (i, 0)), pl.BlockSpec((tm, D), lambda i: (i, 0)),
                  rows.mod_spec(part, tm), _resident((1, D)), _resident((1, D))],
        out_specs=pl.BlockSpec((tm, D), lambda i: (i, 0)),
        out_shape=jax.ShapeDtypeStruct((n_rows, D), F32),
        compiler_params=_cparams("parallel"),
        name="residual_ln",
    )(xa, f, mod, ln_g.reshape(1, D), ln_b.reshape(1, D))


def _moe_ln(rows, xa, mod, router, w_gu, w_down, ln_g, ln_b, n_rows):
    h, gates_t = _router(rows, xa, mod, router, n_rows)
    return _residual_ln(rows, xa, _moe(h, gates_t, w_gu, w_down), mod, 5, ln_g, ln_b, n_rows)


def _gmlp_kernel(x_ref, sh_ref, sc_ref, gate_ref, win_ref, bin_ref, vg_ref, vb_ref, ws_ref, bs_ref, wout_ref,
                 bout_ref, g_ref, b_ref, out_ref, gated_sc):
    tm = x_ref.shape[0]
    inner = wout_ref.shape[0]
    gdim = inner // CMLP_GROUPS
    x = x_ref[...]
    h = (x * (1.0 + sc_ref[0]) + sh_ref[0]).astype(BF16)
    v = jax.nn.gelu(jnp.dot(h, win_ref[:, inner:], preferred_element_type=F32) + bin_ref[:, inner:])
    v = _layer_norm(v, vg_ref[...], vb_ref[...]).astype(BF16)
    for gi in range(CMLP_GROUPS):
        cs = slice(gi * gdim, (gi + 1) * gdim)
        u = jax.nn.gelu(jnp.dot(h, win_ref[:, cs], preferred_element_type=F32) + bin_ref[:, cs])
        w_s = ws_ref[gi]
        bias = bs_ref[:, gi:gi + 1]
        for c in range(tm // CMLP_CHUNK):
            rs = slice(c * CMLP_CHUNK, (c + 1) * CMLP_CHUNK)
            mixed = jnp.dot(w_s, v[rs, cs], preferred_element_type=F32) + bias
            gated_sc[rs, cs] = (u[rs] * mixed).astype(BF16)
    y = jnp.dot(gated_sc[...], wout_ref[...], preferred_element_type=F32) + bout_ref[...]
    out_ref[...] = _layer_norm(DEEPNORM_ALPHA * x + gate_ref[0] * y, g_ref[...], b_ref[...])


def _gmlp_ln(rows, xa, mod, w_in, b_in, vg, vb, w_s, b_s, w_out, b_out, ln_g, ln_b, n_rows):
    D = rows.D
    inner = w_out.shape[0]
    tm = rows.tile((512, 256, 128))
    return pl.pallas_call(
        _gmlp_kernel,
        grid=(n_rows // tm,),
        in_specs=[pl.BlockSpec((tm, D), lambda i: (i, 0)), rows.mod_spec(0, tm), rows.mod_spec(1, tm),
                  rows.mod_spec(2, tm), _resident(w_in.shape), _resident((1, 2 * inner)), _resident((1, inner)),
                  _resident((1, inner)), _resident(w_s.shape), _resident((CMLP_CHUNK, CMLP_GROUPS)),
                  _resident(w_out.shape), _resident((1, D)), _resident((1, D)), _resident((1, D))],
        out_specs=pl.BlockSpec((tm, D), lambda i: (i, 0)),
        out_shape=jax.ShapeDtypeStruct((n_rows, D), F32),
        scratch_shapes=[pltpu.VMEM((tm, inner), BF16)],
        compiler_params=_cparams("parallel"),
        name="gmlp_ln",
    )(xa, mod, mod, mod, w_in, b_in.reshape(1, -1), vg.reshape(1, -1), vb.reshape(1, -1), w_s, b_s.T, w_out,
      b_out.reshape(1, D), ln_g.reshape(1, D), ln_b.reshape(1, D))


def _ret_proj_kernel(x_ref, sh_ref, sc_ref, w_ref, cos_ref, sin_ref, q_ref, k_ref, v_ref, sg_ref, *, nk, nv, dk):
    h = (x_ref[...] * (1.0 + sc_ref[0]) + sh_ref[0]).astype(BF16)
    cos, sin = cos_ref[...], sin_ref[...]
    k_scale = dk ** -0.5

    def rope_store(dst_ref, col0, scale):
        full = jnp.dot(h, w_ref[:, col0:col0 + nk], preferred_element_type=F32) * scale
        for j in range(nk // LANES):
            t = full[:, j * LANES:(j + 1) * LANES]
            ts = slice((j % (dk // LANES)) * LANES, (j % (dk // LANES) + 1) * LANES)
            dst_ref[:, j * LANES:(j + 1) * LANES] = (t * cos[:, ts] + _swap_pairs(t, dk // 4) * sin[:, ts]).astype(BF16)

    rope_store(q_ref, 0, 1.0)
    rope_store(k_ref, nk, k_scale)
    blk = 512
    for j in range(nv // blk):
        v_ref[:, j * blk:(j + 1) * blk] = jnp.dot(
            h, w_ref[:, 2 * nk + j * blk:2 * nk + (j + 1) * blk], preferred_element_type=F32).astype(BF16)
        gt = jnp.dot(h, w_ref[:, 2 * nk + nv + j * blk:2 * nk + nv + (j + 1) * blk], preferred_element_type=F32)
        sg_ref[:, j * blk:(j + 1) * blk] = (gt * jax.nn.sigmoid(gt)).astype(BF16)


def _ret_project(rows, xa, mod, w, cos, sin):
    D = rows.D
    nk = D
    nv = (w.shape[1] - 2 * nk) // 2
    dk = nk // RET_HEADS
    tm = rows.tile((512, 256))
    row_spec = lambda width: pl.BlockSpec((tm, width), lambda i: (i, 0))
    return pl.pallas_call(
        functools.partial(_ret_proj_kernel, nk=nk, nv=nv, dk=dk),
        grid=(rows.n // tm,),
        in_specs=[row_spec(D), rows.mod_spec(0, tm), rows.mod_spec(1, tm), _resident(w.shape),
                  rows.pos_spec(tm, dk), rows.pos_spec(tm, dk)],
        out_specs=[row_spec(nk), row_spec(nk), row_spec(nv), row_spec(nv)],
        out_shape=[jax.ShapeDtypeStruct((rows.n, nk), BF16), jax.ShapeDtypeStruct((rows.n, nk), BF16),
                   jax.ShapeDtypeStruct((rows.n, nv), BF16), jax.ShapeDtypeStruct((rows.n, nv), BF16)],
        compiler_params=_cparams("parallel"),
        name="ret_proj",
    )(xa, mod, mod, w, cos, sin)


RET_SCAN_CHUNK = 256


def _ret_scan_kernel(lg_ref, q_ref, k_ref, v_ref, sg_ref, sf_ref, sb_ref, o_ref, sf_out, sb_out,
                     state_sc, part_sc, inner_sc, cross_sc, weight_sc):
    ch = RET_SCAN_CHUNK
    n_chunks = q_ref.shape[0] // ch
    dk, dv = q_ref.shape[1], v_ref.shape[1]
    head = pl.program_id(1)

    chunk_decay = []
    for d in range(2):
        lg = lg_ref[d, head]
        ii = lax.broadcasted_iota(jnp.int32, (ch, ch), 0)
        jj = lax.broadcasted_iota(jnp.int32, (ch, ch), 1)
        dist = (ii - jj) if d == 0 else (jj - ii)
        inner_sc[d] = jnp.where(dist >= 0, jnp.exp(jnp.maximum(dist, 0).astype(F32) * lg), 0.0)
        row_v = lax.broadcasted_iota(jnp.int32, (ch, dv), 0)
        row_k = lax.broadcasted_iota(jnp.int32, (ch, dk), 0)
        cross_sc[d] = jnp.exp(((row_v if d == 0 else ch - 1 - row_v) + 1).astype(F32) * lg)
        weight_sc[d] = jnp.exp((ch - 1 - (row_k if d == 0 else ch - 1 - row_k)).astype(F32) * lg)
        chunk_decay.append(jnp.exp(jnp.full((1, 1), ch, F32) * lg))
    state_sc[0] = sf_ref[0, 0]
    state_sc[1] = sb_ref[0, 0]

    def visit(d, c):
        rows = pl.ds(pl.multiple_of(c * ch, ch), ch)
        q, k, v = q_ref[rows, :], k_ref[rows, :], v_ref[rows, :]
        scores = lax.dot_general(q, k, (((1,), (1,)), ((), ())), preferred_element_type=F32) * inner_sc[d]
        state = state_sc[d]
        out = (jnp.dot(scores.astype(BF16), v, preferred_element_type=F32)
               + jnp.dot(q, state.astype(BF16), preferred_element_type=F32) * cross_sc[d])
        kw = (k.astype(F32) * weight_sc[d]).astype(BF16)
        state_sc[d] = state * chunk_decay[d] + lax.dot_general(
            kw, v, (((0,), (0,)), ((), ())), preferred_element_type=F32)
        return rows, out

    def finish(rows, o):
        o = o * lax.rsqrt(jnp.mean(o * o, axis=-1, keepdims=True) + RMS_EPS)
        o_ref[rows, :] = (sg_ref[rows, :].astype(F32) * o).astype(BF16)

    def first_visits(t, carry):
        for d, c in ((0, t), (1, n_chunks - 1 - t)):
            rows, out = visit(d, c)
            part_sc[rows, :] = out
        return carry

    def second_visits(t, carry):
        for d, c in ((0, t), (1, n_chunks - 1 - t)):
            rows, out = visit(d, c)
            finish(rows, out + part_sc[rows, :])
        return carry

    lax.fori_loop(0, n_chunks // 2, first_visits, 0)
    if n_chunks % 2 == 1:
        rows, out_f = visit(0, n_chunks // 2)
        _, out_b = visit(1, n_chunks // 2)
        finish(rows, out_f + out_b)
    lax.fori_loop((n_chunks + 1) // 2, n_chunks, second_visits, 0)
    sf_out[0, 0] = state_sc[0]
    sb_out[0, 0] = state_sc[1]


def _ret_scan_segment(rows, log_gamma, q, k, v, sg, state_f, state_b, *, seg_len, first_block):
    B = rows.B
    dk = q.shape[1] // RET_HEADS
    dv = v.shape[1] // RET_HEADS
    ch = RET_SCAN_CHUNK
    assert seg_len % ch == 0
    kspec = pl.BlockSpec((seg_len, dk), lambda b, h: (first_block + b, h))
    vspec = pl.BlockSpec((seg_len, dv), lambda b, h: (first_block + b, h))
    sspec = pl.BlockSpec((1, 1, dk, dv), lambda b, h: (b, h, 0, 0))
    state_shape = jax.ShapeDtypeStruct((B, RET_HEADS, dk, dv), F32)
    return pl.pallas_call(
        _ret_scan_kernel,
        grid=(B, RET_HEADS),
        in_specs=[pl.BlockSpec(memory_space=pltpu.SMEM), kspec, kspec, vspec, vspec, sspec, sspec],
        out_specs=[pl.BlockSpec((seg_len, dv), lambda b, h: (b, h)), sspec, sspec],
        out_shape=[jax.ShapeDtypeStruct((B * seg_len, v.shape[1]), BF16), state_shape, state_shape],
        scratch_shapes=[pltpu.VMEM((2, dk, dv), F32), pltpu.VMEM((seg_len, dv), F32),
                        pltpu.VMEM((2, ch, ch), F32), pltpu.VMEM((2, ch, dv), F32), pltpu.VMEM((2, ch, dk), F32)],
        compiler_params=_cparams("parallel", "parallel"),
        name="ret_scan",
    )(log_gamma, q, k, v, sg, state_f, state_b)


def _retention(rows, log_gamma, q, k, v, sg):
    B, S, C = rows.B, rows.S, rows.C
    assert rows.n_lat % C == 0
    dk, dv = q.shape[1] // RET_HEADS, v.shape[1] // RET_HEADS
    zeros = jnp.zeros((B, RET_HEADS, dk, dv), F32)
    o_ctx, state_f, state_b = _ret_scan_segment(rows, log_gamma, q, k, v, sg, zeros, zeros,
                                                seg_len=C, first_block=rows.n_lat // C)
    o_lat, _, _ = _ret_scan_segment(rows, log_gamma, q, k, v, sg, state_f, state_b, seg_len=S, first_block=0)
    return [o_lat, o_ctx]


def kernel(x, c, ctx, c_ctx, l0_ada_w, l0_ada_b, l0_ln1_g, l0_ln1_b, l0_ln2_g, l0_ln2_b, l0_attn_wqkv, l0_attn_q_norm, l0_attn_k_norm, l0_attn_wo, l0_ffn_w_gu, l0_ffn_w_down, l1_ada_w, l1_ada_b, l1_ln1_g, l1_ln1_b, l1_ln2_g, l1_ln2_b, l1_cmlp_w_in, l1_cmlp_b_in, l1_cmlp_v_norm_g, l1_cmlp_v_norm_b, l1_cmlp_w_s, l1_cmlp_b_s, l1_cmlp_w_out, l1_cmlp_b_out, l1_moe_router, l1_moe_w_gu, l1_moe_w_down, l2_ada_w, l2_ada_b, l2_ln1_g, l2_ln1_b, l2_ln2_g, l2_ln2_b, l2_ret_wqkvg, l2_ret_decay, l2_ret_wo, l2_ffn_w_gu, l2_ffn_w_down, l3_ada_w, l3_ada_b, l3_ln1_g, l3_ln1_b, l3_ln2_g, l3_ln2_b, l3_attn_wqkv, l3_attn_q_norm, l3_attn_k_norm, l3_attn_wo, l3_moe_router, l3_moe_w_gu, l3_moe_w_down):
    B, S, D = x.shape
    C = ctx.shape[1]
    rows = _Rows(B, S, C, D)
    assert S % GRID_W == 0 and C % RET_CHUNK == 0 and S % RET_CHUNK == 0

    xa = jnp.concatenate([x.reshape(rows.n_lat, D), ctx.reshape(rows.n_ctx, D)], axis=0)
    n_cond = -(-(B + 1) // 8) * 8
    cond = jnp.concatenate([c, c_ctx[None, :], jnp.zeros((n_cond - B - 1, D), F32)], axis=0)

    def modulation(ada_w, ada_b):
        return _ada(cond, ada_w, ada_b).reshape(n_cond * 6, 1, D)

    pad = rows.tile((512, 256))
    attn_cos, attn_sin = _rope_tables(S, ATTN_HEAD_DIM, pad)
    ret_cos, ret_sin = _rope_tables(S, D // RET_HEADS, pad)
    zero_bias = jnp.zeros((D,), F32)
    bf = lambda w: w.astype(BF16)

    def attention_layer(xa, mod, wqkv, qg, kg, wo, ln_g, ln_b, with_ctx):
        q, k, v = _attn_project(rows, xa, mod, bf(wqkv), qg, kg, attn_cos, attn_sin)
        o = _attention(rows, q, k, v, with_ctx)
        return _proj_ln(rows, [o], bf(wo), zero_bias, xa, mod, ln_g, ln_b)

    def retention_layer(xa, mod, wqkvg, decay, wo, ln_g, ln_b):
        log_gamma = -jnp.exp(decay.astype(F32))
        q, k, v, sg = _ret_project(rows, xa, mod, bf(wqkvg), ret_cos, ret_sin)
        o_segments = _retention(rows, log_gamma, q, k, v, sg)
        return _proj_ln(rows, o_segments, bf(wo), zero_bias, xa, mod, ln_g, ln_b)

    mod = modulation(l0_ada_w, l0_ada_b)
    xa = attention_layer(xa, mod, l0_attn_wqkv, l0_attn_q_norm, l0_attn_k_norm, l0_attn_wo, l0_ln1_g, l0_ln1_b, True)
    xa = _ffn_ln(rows, xa, mod, bf(l0_ffn_w_gu), bf(l0_ffn_w_down), l0_ln2_g, l0_ln2_b, rows.n)

    mod = modulation(l1_ada_w, l1_ada_b)
    xa = _gmlp_ln(rows, xa, mod, bf(l1_cmlp_w_in), l1_cmlp_b_in, l1_cmlp_v_norm_g, l1_cmlp_v_norm_b,
                  bf(l1_cmlp_w_s), l1_cmlp_b_s, bf(l1_cmlp_w_out), l1_cmlp_b_out, l1_ln1_g, l1_ln1_b, rows.n)
    xa = _moe_ln(rows, xa, mod, l1_moe_router, bf(l1_moe_w_gu), bf(l1_moe_w_down), l1_ln2_g, l1_ln2_b, rows.n)

    mod = modulation(l2_ada_w, l2_ada_b)
    xa = retention_layer(xa, mod, l2_ret_wqkvg, l2_ret_decay, l2_ret_wo, l2_ln1_g, l2_ln1_b)
    xa = _ffn_ln(rows, xa, mod, bf(l2_ffn_w_gu), bf(l2_ffn_w_down), l2_ln2_g, l2_ln2_b, rows.n)

    mod = modulation(l3_ada_w, l3_ada_b)
    xl = attention_layer(xa, mod, l3_attn_wqkv, l3_attn_q_norm, l3_attn_k_norm, l3_attn_wo, l3_ln1_g, l3_ln1_b, False)
    xl = _moe_ln(rows, xl, mod, l3_moe_router, bf(l3_moe_w_gu), bf(l3_moe_w_down), l3_ln2_g, l3_ln2_b, rows.n_lat)
    return xl.reshape(B, S, D)
```
